```python
import math
import jax
import jax.numpy as jnp
from jax import lax
import numpy as np

D_MODEL = 2048
BATCH = 4
SEQ = 2048
DEPTH = 4
DEC_BATCH = 8
DEC_SEQ = 4
PAST_LEN = 16384
PAGE_SIZE = 128

HEAD_DIM = 64
SB_HEADS = D_MODEL // 128
SB_WIDTH = SB_HEADS * HEAD_DIM
SB_QBLOCK = 128
NSA_HEADS = D_MODEL // 128
NSA_KV_HEADS = 4
NSA_GROUP = NSA_HEADS // NSA_KV_HEADS
NSA_WIDTH = NSA_HEADS * HEAD_DIM
NSA_KV_WIDTH = NSA_KV_HEADS * HEAD_DIM
CMP_STRIDE = 16
CMP_BLOCK = 2 * CMP_STRIDE
CMP_HIDDEN = 64
SEL_BLOCK = 64
N_SEL = 16
SEL_QBLOCK = 64
WINDOW = 512
WIN_QBLOCK = 128
N_BUCKETS = 32
MAX_DISTANCE = 1024
RWKV_HEADS = D_MODEL // 128
RWKV_WIDTH = RWKV_HEADS * HEAD_DIM
DECAY_LORA = 96
ICLR_LORA = 96
GATE_LORA = 256
RWKV_COLS = 3 * RWKV_WIDTH + DECAY_LORA + ICLR_LORA + GATE_LORA
GN_EPS = 64e-5
N_BRANCH = 3
BRANCH_WIDTH = SB_WIDTH
D_FF = (8 * D_MODEL // 3 + 127) // 128 * 128
CONV_W = 3
RMS_EPS = 1e-6
IN_SIZES = (SB_WIDTH, SB_WIDTH, SB_WIDTH, NSA_WIDTH, 6 * NSA_KV_WIDTH, 3 * NSA_HEADS, RWKV_COLS, N_BRANCH * D_MODEL)
N_IN = 3 * SB_WIDTH + NSA_WIDTH + 6 * NSA_KV_WIDTH + 3 * NSA_HEADS + RWKV_COLS + N_BRANCH * D_MODEL
RWKV_SIZES = (RWKV_WIDTH, RWKV_WIDTH, RWKV_WIDTH, DECAY_LORA, ICLR_LORA, GATE_LORA)

kernel_name = 'hybrid_sb_nsa_rwkv7_convffn_step'


def _rmsnorm(x, g):
    xf = x.astype(jnp.float32)
    y = xf * lax.rsqrt(jnp.mean(xf * xf, axis=-1, keepdims=True) + RMS_EPS)
    return (y * g.astype(jnp.float32)).astype(x.dtype)


def _split_cols(a, sizes):
    out, start = [], 0
    for s in sizes:
        out.append(a[..., start:start + s])
        start += s
    return out


def _qblock(n, pref):
    return pref if n % pref == 0 else n


def _block_split(a, axis, blk):
    shp = a.shape
    a = a.reshape(shp[:axis] + (shp[axis] // blk, blk) + shp[axis + 1:])
    return jnp.moveaxis(a, axis, 0)


def _block_merge(a, axis):
    a = jnp.moveaxis(a, 0, axis)
    shp = a.shape
    return a.reshape(shp[:axis] + (shp[axis] * shp[axis + 1],) + shp[axis + 2:])


def _t5_bucket(dist):
    exact = N_BUCKETS // 2
    d = jnp.maximum(dist, 0)
    ratio = jnp.log(jnp.maximum(d, 1).astype(jnp.float32) / exact) / math.log(MAX_DISTANCE / exact)
    large = jnp.minimum(exact + (ratio * (N_BUCKETS - exact)).astype(jnp.int32), N_BUCKETS - 1)
    return jnp.where(d < exact, d, large)


def _masked_softmax(s, valid):
    s = jnp.where(valid, s, -jnp.inf)
    m = jnp.max(s, axis=-1, keepdims=True)
    m = jnp.where(jnp.isfinite(m), m, 0.0)
    e = jnp.where(valid, jnp.exp(s - m), 0.0)
    den = jnp.sum(e, axis=-1, keepdims=True)
    return e / jnp.where(den > 0, den, 1.0)


def _gather_pages(pool, page_table, layer):
    g = pool[page_table, layer]
    return g.reshape((g.shape[0], g.shape[1] * g.shape[2]) + g.shape[3:])


def _stick_breaking(q, k, v, q_pos0):
    B, Tq, H, D = q.shape
    Tk = k.shape[1]
    blk = _qblock(Tq, SB_QBLOCK)
    k_pos = jnp.arange(Tk)
    scale = D ** -0.5

    def one(args):
        qb, t_pos = args
        z = jnp.einsum('bqhd,bkhd->bhqk', qb, k, preferred_element_type=jnp.float32) * scale
        before = (k_pos[None, :] < t_pos[:, None])[None, None]
        log_fail = jnp.where(before, jax.nn.log_sigmoid(-z), 0.0)
        between = lax.cumsum(log_fail, axis=3, reverse=True) - log_fail
        w = jnp.where(before, jnp.exp(jax.nn.log_sigmoid(z) + between), 0.0)
        return jnp.einsum('bhqk,bkhd->bqhd', w.astype(v.dtype), v)

    t_pos = (q_pos0 + jnp.arange(Tq)).reshape(Tq // blk, blk)
    return _block_merge(lax.map(one, (_block_split(q, 1, blk), t_pos)), 1)


def _nsa_compress(raw, pe, w1, b1, w2, b2):
    B, Tk, Hk, D = raw.shape
    n_chunk = Tk // CMP_STRIDE
    chunks = raw[:, :n_chunk * CMP_STRIDE].reshape(B, n_chunk, CMP_STRIDE, Hk, D)
    w1h = w1.reshape(2, CMP_STRIDE, D, CMP_HIDDEN)
    first = jnp.einsum('bcshd,sde->bche', chunks[:, :-1], w1h[0])
    second = jnp.einsum('bcshd,sde->bche', chunks[:, 1:], w1h[1])
    pos = jnp.einsum('ld,lde->e', pe, w1)
    hid = jax.nn.gelu(first + second + pos + b1)
    return jnp.einsum('bche,ed->bchd', hid, w2) + b2


def _band_attend(qg, k, v, q_pos0, k_pos0, tab):
    B, Tq = qg.shape[:2]
    D = qg.shape[-1]
    blk = _qblock(Tq, WIN_QBLOCK)
    span = WINDOW + blk
    off = q_pos0 - k_pos0
    padw = ((0, 0), (WINDOW, 0), (0, 0), (0, 0))
    kp, vp = jnp.pad(k, padw), jnp.pad(v, padw)

    def one(args):
        qb, j = args
        start = j * blk + off
        kk = lax.dynamic_slice_in_dim(kp, start, span, axis=1)
        vv = lax.dynamic_slice_in_dim(vp, start, span, axis=1)
        pidx = start + jnp.arange(span)
        d = (q_pos0 + j * blk + jnp.arange(blk))[:, None] - (k_pos0 - WINDOW + pidx)[None, :]
        valid = (pidx >= WINDOW)[None, :] & (d >= 0) & (d < WINDOW)
        s = (jnp.einsum('btkgd,bnkd->bkgtn', qb, kk, preferred_element_type=jnp.float32) * D ** -0.5
             + jnp.transpose(tab[_t5_bucket(d)], (2, 3, 0, 1)))
        pr = _masked_softmax(s, valid)
        return jnp.einsum('bkgtn,bnkd->btkgd', pr.astype(vv.dtype), vv)

    return _block_merge(lax.map(one, (_block_split(qg, 1, blk), jnp.arange(Tq // blk))), 1)


def _nsa(q, cmp_k, cmp_v, slc_k, slc_v, win_k, win_v, gate_logits, q_pos0, win_pos0, rel_bias, p):
    B, Tq, H, D = q.shape
    Tk = slc_k.shape[1]
    f32 = jnp.float32
    scale = D ** -0.5
    qg = q.reshape(B, Tq, NSA_KV_HEADS, NSA_GROUP, D)
    t_pos = q_pos0 + jnp.arange(Tq)
    tab = rel_bias.astype(f32).reshape(N_BUCKETS, NSA_KV_HEADS, NSA_GROUP)
    gains = p['nsa_qk_gain']
    kc = _rmsnorm(_nsa_compress(cmp_k, p['cmp_pe'][0], p['cmp_w1'][0], p['cmp_b1'][0], p['cmp_w2'][0], p['cmp_b2'][0]), gains[1])
    vc = _nsa_compress(cmp_v, p['cmp_pe'][1], p['cmp_w1'][1], p['cmp_b1'][1], p['cmp_w2'][1], p['cmp_b2'][1])
    n_cmp = kc.shape[1]
    c_start = jnp.arange(n_cmp) * CMP_STRIDE
    dist = t_pos[:, None] - (c_start + CMP_BLOCK - 1)[None, :]
    s = (jnp.einsum('btkgd,bckd->bkgtc', qg, kc, preferred_element_type=f32) * scale
         + jnp.transpose(tab[_t5_bucket(dist)], (2, 3, 0, 1)))
    p_cmp = _masked_softmax(s, dist >= 0)
    o_cmp = jnp.einsum('bkgtc,bckd->btkgd', p_cmp.astype(vc.dtype), vc)
    n_sel = -(-Tk // SEL_BLOCK)
    j_start = jnp.arange(n_sel) * SEL_BLOCK
    overlap = jnp.clip(jnp.minimum(c_start[:, None] + CMP_BLOCK, j_start[None, :] + SEL_BLOCK)
                       - jnp.maximum(c_start[:, None], j_start[None, :]), 0, None).astype(f32) / CMP_BLOCK
    importance = jnp.einsum('bkgtc,cj->bktj', p_cmp, overlap)
    cur = (t_pos // SEL_BLOCK)[:, None]
    jj = jnp.arange(n_sel)[None, :]
    causal_blk = j_start[None, :] <= t_pos[:, None]
    forced = causal_blk & ((jj == 0) | (jj == cur) | (jj == cur - 1))
    score = jnp.where(forced, jnp.inf, jnp.where(causal_blk, importance, -jnp.inf))
    k_eff = min(N_SEL, n_sel)
    sel_idx = lax.top_k(score, k_eff)[1]
    pad = n_sel * SEL_BLOCK - Tk

    def to_blocks(a):
        a = jnp.pad(a, ((0, 0), (0, pad), (0, 0), (0, 0)))
        return a.reshape(B, n_sel, SEL_BLOCK, NSA_KV_HEADS, D).transpose(0, 3, 1, 2, 4)

    kb, vb = to_blocks(slc_k), to_blocks(slc_v)
    gather = jax.vmap(jax.vmap(lambda blocks, ix: blocks[ix]))
    tab_h = jnp.transpose(tab, (1, 0, 2))
    head_ix = jnp.arange(NSA_KV_HEADS)[None, :, None, None]
    n_keys = k_eff * SEL_BLOCK
    blk = _qblock(Tq, SEL_QBLOCK)

    def sel_one(args):
        qb, ib, tp = args
        gk = gather(kb, ib).reshape(B, NSA_KV_HEADS, blk, n_keys, D)
        gv = gather(vb, ib).reshape(B, NSA_KV_HEADS, blk, n_keys, D)
        kpos = (ib[..., None] * SEL_BLOCK + jnp.arange(SEL_BLOCK)).reshape(B, NSA_KV_HEADS, blk, n_keys)
        d = tp[None, None, :, None] - kpos
        bias = jnp.moveaxis(tab_h[head_ix, _t5_bucket(d)], -1, 2)
        s2 = jnp.einsum('btkgd,bktnd->bkgtn', qb, gk, preferred_element_type=f32) * scale + bias
        pr = _masked_softmax(s2, (d >= 0)[:, :, None])
        return jnp.einsum('bkgtn,bktnd->btkgd', pr.astype(gv.dtype), gv)

    o_slc = _block_merge(lax.map(sel_one, (_block_split(qg, 1, blk), _block_split(sel_idx, 2, blk),
                                           t_pos.reshape(Tq // blk, blk))), 1)
    o_win = _band_attend(qg, win_k, win_v, q_pos0, win_pos0, tab)
    g = jax.nn.sigmoid(gate_logits.astype(f32)).reshape(B, Tq, 3, NSA_KV_HEADS, NSA_GROUP)[..., None]
    out = g[:, :, 0] * o_cmp + g[:, :, 1] * o_slc + g[:, :, 2] * o_win
    return out.reshape(B, Tq, NSA_WIDTH).astype(q.dtype)


def _rwkv7(cols_in, shift_prev, S0, p):
    B, T, _ = cols_in.shape
    f32 = jnp.float32
    cols = cols_in.astype(f32)
    prev = jnp.concatenate([shift_prev.astype(f32)[:, None], cols[:, :-1]], axis=1)
    xs = cols + (prev - cols) * p['rwkv_mu']
    r, k, v, xw, xa, xg = _split_cols(xs, RWKV_SIZES)
    w_log = -jax.nn.softplus(-(p['rwkv_w0'] + jnp.tanh(xw) @ p['rwkv_w2'])) - 0.5
    decay = jnp.exp(-jnp.exp(w_log))
    a = jax.nn.sigmoid(p['rwkv_a0'] + xa @ p['rwkv_a2'])
    g = jax.nn.sigmoid(xg) @ p['rwkv_g2']
    hd = lambda t: t.reshape(B, T, RWKV_HEADS, HEAD_DIM)
    kk = hd(k * p['rwkv_kk'])
    kk = kk * lax.rsqrt(jnp.maximum(jnp.sum(kk * kk, axis=-1, keepdims=True), 1e-24))
    k = k * (1.0 + (a - 1.0) * p['rwkv_ka'])
    r, k, v, decay, a = hd(r), hd(k), hd(v), hd(decay), hd(a)
    b = kk * a

    def step(S, inp):
        r_t, w_t, k_t, v_t, kk_t, b_t = inp
        sa = jnp.einsum('bhij,bhj->bhi', S, -kk_t)
        S = S * w_t[:, :, None, :] + sa[..., None] * b_t[:, :, None, :] + v_t[..., None] * k_t[:, :, None, :]
        return S, jnp.einsum('bhij,bhj->bhi', S, r_t)

    seq = tuple(jnp.moveaxis(t, 1, 0) for t in (r, decay, k, v, kk, b))
    S_T, y = lax.scan(step, S0.astype(f32), seq)
    y = jnp.moveaxis(y, 0, 1)
    mu = jnp.mean(y, axis=-1, keepdims=True)
    var = jnp.mean(jnp.square(y - mu), axis=-1, keepdims=True)
    y = ((y - mu) * lax.rsqrt(var + GN_EPS)).reshape(B, T, RWKV_WIDTH) * p['rwkv_lnx_w'] + p['rwkv_lnx_b']
    bonus = (jnp.sum(r * k * p['rwkv_rk'], axis=-1, keepdims=True) * v).reshape(B, T, RWKV_WIDTH)
    return (y + bonus) * g, S_T, cols_in[:, -1]


def _conv_ffn(h, conv_prev, p):
    T = h.shape[1]
    u, g = _split_cols(h @ p['ffn_w_up'], (D_FF, D_FF))
    padded = jnp.concatenate([conv_prev.astype(g.dtype), g], axis=1)
    cw = p['ffn_conv_w']
    gc = p['ffn_conv_b'] + padded[:, 0:T] * cw[0]
    for j in range(1, CONV_W):
        gc = gc + padded[:, j:j + T] * cw[j]
    out = (jax.nn.silu(gc) * u) @ p['ffn_w_down']
    return out, padded[:, T:]


def _layer(x, p, rel_bias, past):
    B, T, _ = x.shape
    h = _rmsnorm(x, p['norm_mix_g'])
    sb_q, sb_k, sb_v, nsa_q, nsa_kv, nsa_g, rw_cols, merge = _split_cols(h @ p['w_in'], IN_SIZES)
    hd = lambda t, n: t.reshape(B, T, n, HEAD_DIM)
    sb_q, sb_k, sb_v = hd(sb_q, SB_HEADS), hd(sb_k, SB_HEADS), hd(sb_v, SB_HEADS)
    gains = p['nsa_qk_gain']
    nq = _rmsnorm(hd(nsa_q, NSA_HEADS), gains[0])
    kv = nsa_kv.reshape(B, T, 6, NSA_KV_HEADS, HEAD_DIM)
    cmp_new = kv[:, :, 0:2]
    slc_new = jnp.stack([_rmsnorm(kv[:, :, 2], gains[2]), kv[:, :, 3]], axis=2)
    win_new = jnp.stack([_rmsnorm(kv[:, :, 4], gains[3]), kv[:, :, 5]], axis=2)
    if past is None:
        pos0, win_pos0 = 0, 0
        sb_k_all, sb_v_all = sb_k, sb_v
        cmp_all, slc_all, win_all = cmp_new, slc_new, win_new
        S0 = jnp.zeros((B, RWKV_HEADS, HEAD_DIM, HEAD_DIM), jnp.float32)
        shift_prev = jnp.zeros((B, RWKV_COLS), x.dtype)
        conv_prev = jnp.zeros((B, CONV_W - 1, D_FF), x.dtype)
    else:
        pos0 = past['pos0']
        win_pos0 = pos0 - past['win'].shape[1]
        sb_k_all = jnp.concatenate([past['sb_k'], sb_k], axis=1)
        sb_v_all = jnp.concatenate([past['sb_v'], sb_v], axis=1)
        cmp_all = jnp.concatenate([past['cmp'], cmp_new], axis=1)
        slc_all = jnp.concatenate([past['slc'], slc_new], axis=1)
        win_all = jnp.concatenate([past['win'], win_new], axis=1)
        S0, shift_prev, conv_prev = past['rwkv'], past['shift'], past['conv']
    o_a = _stick_breaking(sb_q, sb_k_all, sb_v_all, pos0).reshape(B, T, SB_WIDTH)
    o_b = _nsa(nq, cmp_all[:, :, 0], cmp_all[:, :, 1], slc_all[:, :, 0], slc_all[:, :, 1],
               win_all[:, :, 0], win_all[:, :, 1], nsa_g, pos0, win_pos0, rel_bias, p)
    o_c, S_T, shift_new = _rwkv7(rw_cols, shift_prev, S0, p)
    gate = jax.nn.sigmoid(merge.astype(jnp.float32)).reshape(B, T, N_BRANCH, D_MODEL)
    wb = p['w_branch']
    mixed = (gate[:, :, 0] * (o_a @ wb[0]) + gate[:, :, 1] * (o_b @ wb[1])
             + gate[:, :, 2] * (o_c.astype(x.dtype) @ wb[2]))
    x = x + mixed.astype(x.dtype) @ p['w_out']
    f, conv_new = _conv_ffn(_rmsnorm(x, p['norm_ffn_g']), conv_prev, p)
    x = x + f
    n_win = min(WINDOW, win_all.shape[1])
    new = {'sb_kv': jnp.stack([sb_k, sb_v], axis=2), 'cmp_kv': cmp_new, 'slc_kv': slc_new,
           'win_kv': win_all[:, win_all.shape[1] - n_win:], 'rwkv': S_T.astype(S0.dtype),
           'shift': shift_new, 'conv': conv_new}
    return x, new


def setup_inputs(seed: int = 0) -> dict:
    key = jax.random.key(seed)
    ks = iter(jax.random.split(key, 48))

    def nrm(shape, scale=1.0):
        return jax.random.normal(next(ks), shape, jnp.float32) * scale

    def unif(shape, lo, hi):
        return jax.random.uniform(next(ks), shape, jnp.float32, lo, hi)

    n_pages = PAST_LEN // PAGE_SIZE
    n_used = DEC_BATCH * n_pages
    n_phys = n_used + max(1, n_used // 4)
    win_rows = min(WINDOW, PAST_LEN)
    page_table = jax.random.permutation(next(ks), n_phys)[:n_used].reshape(DEC_BATCH, n_pages).astype(jnp.int32)
    return {
        'x_prompt': nrm((BATCH, SEQ, D_MODEL)),
        'x_sample': nrm((DEC_BATCH, DEC_SEQ, D_MODEL)),
        'cache_sb_kv': nrm((n_phys, DEPTH, PAGE_SIZE, 2, SB_HEADS, HEAD_DIM)),
        'cache_cmp_kv': nrm((n_phys, DEPTH, PAGE_SIZE, 2, NSA_KV_HEADS, HEAD_DIM)),
        'cache_slc_kv': nrm((n_phys, DEPTH, PAGE_SIZE, 2, NSA_KV_HEADS, HEAD_DIM)),
        'cache_win_kv': nrm((DEC_BATCH, DEPTH, win_rows, 2, NSA_KV_HEADS, HEAD_DIM)),
        'state_rwkv': nrm((DEC_BATCH, DEPTH, RWKV_HEADS, HEAD_DIM, HEAD_DIM), 0.5),
        'state_rwkv_shift': nrm((DEC_BATCH, DEPTH, RWKV_COLS)),
        'state_conv': nrm((DEC_BATCH, DEPTH, CONV_W - 1, D_FF)),
        'page_table': page_table,
        'rel_bias': nrm((N_BUCKETS, NSA_HEADS), 0.5),
        'norm_mix_g': 1.0 + nrm((DEPTH, D_MODEL), 0.02),
        'norm_ffn_g': 1.0 + nrm((DEPTH, D_MODEL), 0.02),
        'w_in': nrm((DEPTH, D_MODEL, N_IN), D_MODEL ** -0.5),
        'nsa_qk_gain': 1.0 + nrm((DEPTH, 4, HEAD_DIM), 0.02),
        'cmp_pe': nrm((DEPTH, 2, CMP_BLOCK, HEAD_DIM), 0.1),
        'cmp_w1': nrm((DEPTH, 2, CMP_BLOCK, HEAD_DIM, CMP_HIDDEN), (CMP_BLOCK * HEAD_DIM) ** -0.5),
        'cmp_b1': nrm((DEPTH, 2, CMP_HIDDEN), 0.02),
        'cmp_w2': nrm((DEPTH, 2, CMP_HIDDEN, HEAD_DIM), CMP_HIDDEN ** -0.5),
        'cmp_b2': nrm((DEPTH, 2, HEAD_DIM), 0.02),
        'rwkv_mu': unif((DEPTH, RWKV_COLS), 0.0, 1.0),
        'rwkv_w0': unif((DEPTH, RWKV_WIDTH), -6.0, -1.0),
        'rwkv_w2': nrm((DEPTH, DECAY_LORA, RWKV_WIDTH), 0.5 * DECAY_LORA ** -0.5),
        'rwkv_a0': nrm((DEPTH, RWKV_WIDTH), 0.1),
        'rwkv_a2': nrm((DEPTH, ICLR_LORA, RWKV_WIDTH), ICLR_LORA ** -0.5),
        'rwkv_g2': nrm((DEPTH, GATE_LORA, RWKV_WIDTH), GATE_LORA ** -0.5),
        'rwkv_kk': 0.85 + nrm((DEPTH, RWKV_WIDTH), 0.02),
        'rwkv_ka': 1.0 + nrm((DEPTH, RWKV_WIDTH), 0.02),
        'rwkv_rk': nrm((DEPTH, RWKV_HEADS, HEAD_DIM), 0.1),
        'rwkv_lnx_w': 1.0 + nrm((DEPTH, RWKV_WIDTH), 0.02),
        'rwkv_lnx_b': nrm((DEPTH, RWKV_WIDTH), 0.02),
        'w_branch': nrm((DEPTH, N_BRANCH, BRANCH_WIDTH, D_MODEL), BRANCH_WIDTH ** -0.5),
        'w_out': nrm((DEPTH, D_MODEL, D_MODEL), D_MODEL ** -0.5),
        'ffn_w_up': nrm((DEPTH, D_MODEL, 2 * D_FF), D_MODEL ** -0.5),
        'ffn_conv_w': nrm((DEPTH, CONV_W, D_FF), CONV_W ** -0.5),
        'ffn_conv_b': nrm((DEPTH, D_FF), 0.02),
        'ffn_w_down': nrm((DEPTH, D_FF, D_MODEL), D_FF ** -0.5),
    }


def reference(x_prompt, x_sample, cache_sb_kv, cache_cmp_kv, cache_slc_kv, cache_win_kv, state_rwkv,
              state_rwkv_shift, state_conv, page_table, rel_bias, norm_mix_g, norm_ffn_g, w_in, nsa_qk_gain,
              cmp_pe, cmp_w1, cmp_b1, cmp_w2, cmp_b2, rwkv_mu, rwkv_w0, rwkv_w2, rwkv_a0, rwkv_a2, rwkv_g2,
              rwkv_kk, rwkv_ka, rwkv_rk, rwkv_lnx_w, rwkv_lnx_b, w_branch, w_out, ffn_w_up, ffn_conv_w,
              ffn_conv_b, ffn_w_down):
    stacked = {'norm_mix_g': norm_mix_g, 'norm_ffn_g': norm_ffn_g, 'w_in': w_in, 'nsa_qk_gain': nsa_qk_gain,
               'cmp_pe': cmp_pe, 'cmp_w1': cmp_w1, 'cmp_b1': cmp_b1, 'cmp_w2': cmp_w2, 'cmp_b2': cmp_b2,
               'rwkv_mu': rwkv_mu, 'rwkv_w0': rwkv_w0, 'rwkv_w2': rwkv_w2, 'rwkv_a0': rwkv_a0,
               'rwkv_a2': rwkv_a2, 'rwkv_g2': rwkv_g2, 'rwkv_kk': rwkv_kk, 'rwkv_ka': rwkv_ka,
               'rwkv_rk': rwkv_rk, 'rwkv_lnx_w': rwkv_lnx_w, 'rwkv_lnx_b': rwkv_lnx_b,
               'w_branch': w_branch, 'w_out': w_out, 'ffn_w_up': ffn_w_up, 'ffn_conv_w': ffn_conv_w,
               'ffn_conv_b': ffn_conv_b, 'ffn_w_down': ffn_w_down}
    past_len = page_table.shape[1] * cache_sb_kv.shape[2]
    names = ('sb_kv', 'cmp_kv', 'slc_kv', 'win_kv', 'rwkv', 'shift', 'conv')
    new_p = {n: [] for n in names}
    new_s = {n: [] for n in names}
    xp, xs = x_prompt, x_sample
    for layer in range(DEPTH):
        p = {name: arr[layer] for name, arr in stacked.items()}
        xp, st = _layer(xp, p, rel_bias, None)
        for n in names:
            new_p[n].append(st[n])
        sb = _gather_pages(cache_sb_kv, page_table, layer)
        past = {'pos0': past_len, 'sb_k': sb[:, :, 0], 'sb_v': sb[:, :, 1],
                'cmp': _gather_pages(cache_cmp_kv, page_table, layer),
                'slc': _gather_pages(cache_slc_kv, page_table, layer),
                'win': cache_win_kv[:, layer], 'rwkv': state_rwkv[:, layer],
                'shift': state_rwkv_shift[:, layer], 'conv': state_conv[:, layer]}
        xs, st = _layer(xs, p, rel_bias, past)
        for n in names:
            new_s[n].append(st[n])
    P = {n: jnp.stack(new_p[n], axis=1) for n in names}
    S = {n: jnp.stack(new_s[n], axis=1) for n in names}
    return (xp, xs, P['sb_kv'], P['cmp_kv'], P['slc_kv'], P['win_kv'], P['rwkv'], P['shift'], P['conv'],
            S['sb_kv'], S['cmp_kv'], S['slc_kv'], S['win_kv'], S['rwkv'], S['shift'], S['conv'])
```

```python
import functools
import math

import numpy as np
import jax
import jax.numpy as jnp
from jax import lax
from jax.experimental import pallas as pl
from jax.experimental.pallas import tpu as pltpu

F32 = jnp.float32
BF16 = jnp.bfloat16

HEAD_DIM = 64
LANES = 128
SUBLANES = 8
ROW_GROUP = 8
VMEM_LIMIT = 56 * 2 ** 20

RMS_EPS = 1e-6
GN_EPS = 64e-5
CMP_STRIDE = 16
CMP_BLOCK = 32
SEL_BLOCK = 64
N_SEL = 16
WINDOW = 512
N_BUCKETS = 32
MAX_DISTANCE = 1024
NEG = -1e30
RWKV_CHUNK = 64

NN = (((1,), (0,)), ((), ()))
NT = (((1,), (1,)), ((), ()))
TN = (((0,), (0,)), ((), ()))


def _dot(a, b, dims=NN):
    return lax.dot_general(a, b, dims, preferred_element_type=F32)


def _split(x):
    hi = x.astype(BF16)
    lo = (x - hi.astype(F32)).astype(BF16)
    return hi, lo


def _dot_hl(a, b_exact, dims=NN):
    hi, lo = _split(a)
    return _dot(hi, b_exact, dims) + _dot(lo, b_exact, dims)


def _dot3(a, b, dims=NN):
    ah, al = _split(a)
    bh, bl = _split(b)
    return _dot(ah, bh, dims) + (_dot(ah, bl, dims) + _dot(al, bh, dims))


def _iota(shape, dim):
    return lax.broadcasted_iota(jnp.int32, shape, dim)


def _pick(n, cands):
    for c in cands:
        if n % c == 0:
            return c
    raise ValueError(f"no tile for {n} in {cands}")


def _cparams(*sem):
    return pltpu.CompilerParams(dimension_semantics=sem, vmem_limit_bytes=VMEM_LIMIT)


def _seg_mat(n, seg, scale):
    r = lax.shift_right_logical(_iota((n, n), 0), int(math.log2(seg)))
    c = lax.shift_right_logical(_iota((n, n), 1), int(math.log2(seg)))
    return jnp.where(r == c, scale, 0.0).astype(BF16)


def _lo_mask():
    return _iota((1, LANES), 1) < HEAD_DIM


def _rms_rows(x, g):
    ms = jnp.mean(x * x, axis=-1, keepdims=True)
    return x * lax.rsqrt(ms + RMS_EPS) * g


def _rms_matmul_kernel(x_ref, g_ref, w_ref, o_ref, h_ref):
    @pl.when(pl.program_id(1) == 0)
    def _():
        h_ref[...] = _rms_rows(x_ref[...], g_ref[...]).astype(BF16)

    o_ref[...] = _dot(h_ref[...], w_ref[...])


def rms_matmul(x, g, w, tn=512):
    M, K = x.shape
    N = w.shape[1]
    tm = _pick(M, (1024, 512, 256, 128, 64, 32, 16, 8))
    tn = _pick(N, (tn, 256, 128))
    return pl.pallas_call(
        _rms_matmul_kernel,
        grid=(M // tm, N // tn),
        in_specs=[pl.BlockSpec((tm, K), lambda i, j: (i, 0)),
                  pl.BlockSpec((1, K), lambda i, j: (0, 0)),
                  pl.BlockSpec((K, tn), lambda i, j: (0, j))],
        out_specs=pl.BlockSpec((tm, tn), lambda i, j: (i, j)),
        out_shape=jax.ShapeDtypeStruct((M, N), F32),
        scratch_shapes=[pltpu.VMEM((tm, K), BF16)],
        compiler_params=_cparams("parallel", "arbitrary"),
        name="rms_matmul",
    )(x, g.reshape(1, K), w)


def _matmul_res_kernel(x_ref, a_ref, w_ref, o_ref):
    o_ref[...] = x_ref[...] + _dot(a_ref[...], w_ref[...])


def matmul_res(x, a, w, tn=512):
    M, N = x.shape
    K = a.shape[1]
    tm = _pick(M, (1024, 512, 256, 128, 64, 32, 16, 8))
    tn = _pick(N, (tn, 256, 128))
    return pl.pallas_call(
        _matmul_res_kernel,
        grid=(M // tm, N // tn),
        in_specs=[pl.BlockSpec((tm, tn), lambda i, j: (i, j)),
                  pl.BlockSpec((tm, K), lambda i, j: (i, 0)),
                  pl.BlockSpec((K, tn), lambda i, j: (0, j))],
        out_specs=pl.BlockSpec((tm, tn), lambda i, j: (i, j)),
        out_shape=jax.ShapeDtypeStruct((M, N), F32),
        compiler_params=_cparams("parallel", "arbitrary"),
        name="matmul_res",
    )(x, a, w)


def _merge_mix_kernel(oa_ref, ob_ref, oc_ref, wb_ref, m0_ref, m1_ref, m2_ref, o_ref):
    acc = jax.nn.sigmoid(m0_ref[...]) * _dot(oa_ref[...], wb_ref[0])
    acc = acc + jax.nn.sigmoid(m1_ref[...]) * _dot(ob_ref[...], wb_ref[1])
    acc = acc + jax.nn.sigmoid(m2_ref[...]) * _dot(oc_ref[...], wb_ref[2])
    o_ref[...] = acc.astype(o_ref.dtype)


def merge_mix(oa, ob, oc, wb, merge, tn=512):
    M, W = oa.shape
    D = wb.shape[2]
    tm = _pick(M, (512, 256, 128, 64, 32, 16, 8))
    nb = D // tn
    ospec = pl.BlockSpec((tm, W), lambda i, j: (i, 0))
    return pl.pallas_call(
        _merge_mix_kernel,
        grid=(M // tm, nb),
        in_specs=[ospec, ospec, ospec,
                  pl.BlockSpec((3, W, tn), lambda i, j: (0, 0, j)),
                  pl.BlockSpec((tm, tn), lambda i, j: (i, j)),
                  pl.BlockSpec((tm, tn), lambda i, j: (i, j + nb)),
                  pl.BlockSpec((tm, tn), lambda i, j: (i, j + 2 * nb))],
        out_specs=pl.BlockSpec((tm, tn), lambda i, j: (i, j)),
        out_shape=jax.ShapeDtypeStruct((M, D), BF16),
        compiler_params=_cparams("parallel", "arbitrary"),
        name="merge_mix",
    )(oa, ob, oc, wb, merge, merge, merge)


def _ffn_kernel(x_ref, g_ref, wu_ref, wg_ref, cw_ref, cb_ref, wd_ref, prev_ref,
                o_ref, st_ref, h_ref, acc_ref, gs_ref, carry_ref, *, tm, tps, whole_gate):
    i = pl.program_id(0)
    n = pl.program_id(1)

    @pl.when(n == 0)
    def _():
        h_ref[...] = _rms_rows(x_ref[...], g_ref[...]).astype(BF16)
        acc_ref[...] = jnp.zeros_like(acc_ref)

    h = h_ref[...]
    u = _dot(h, wu_ref[...])
    g = _dot(h, wg_ref[...])
    if whole_gate:
        g = g + prev_ref[...]
        head = jnp.zeros((SUBLANES, g.shape[1]), F32)
    else:
        head = jnp.where(i % tps == 0, prev_ref[0], carry_ref[n])
    gs_ref[pl.ds(0, SUBLANES), :] = head
    gs_ref[pl.ds(SUBLANES, tm), :] = g
    gm1 = gs_ref[pl.ds(SUBLANES - 1, tm), :]
    gm2 = gs_ref[pl.ds(SUBLANES - 2, tm), :]
    gc = cb_ref[...] + gm2 * cw_ref[0:1, :] + gm1 * cw_ref[1:2, :] + g * cw_ref[2:3, :]
    act = (gc * jax.nn.sigmoid(gc) * u).astype(BF16)
    acc_ref[...] += _dot(act, wd_ref[...])
    if whole_gate:
        st_ref[...] = g
    else:
        tail = gs_ref[pl.ds(tm, SUBLANES), :]
        carry_ref[n] = tail
        st_ref[0] = tail

    @pl.when(n == pl.num_programs(1) - 1)
    def _():
        o_ref[...] = x_ref[...] + acc_ref[...]


def conv_ffn(x, g, wu, wg, cw, cb, wd, prev, *, seq_rows, whole_gate, tn=512):
    M, D = x.shape
    Fp = wu.shape[1]
    nb = Fp // tn
    if whole_gate:
        tm, tps = M, 1
        prev_spec = pl.BlockSpec((tm, tn), lambda i, n: (0, n))
        st_spec = pl.BlockSpec((tm, tn), lambda i, n: (0, n))
        st_shape = jax.ShapeDtypeStruct((M, Fp), F32)
    else:
        tm = _pick(seq_rows, (512, 256, 128, 64))
        tps = seq_rows // tm
        prev_spec = pl.BlockSpec((1, SUBLANES, tn), lambda i, n: (i // tps, 0, n))
        st_spec = pl.BlockSpec((1, SUBLANES, tn), lambda i, n: (i, 0, n))
        st_shape = jax.ShapeDtypeStruct((M // tm, SUBLANES, Fp), F32)
    kern = functools.partial(_ffn_kernel, tm=tm, tps=tps, whole_gate=whole_gate)
    y, st = pl.pallas_call(
        kern,
        grid=(M // tm, nb),
        in_specs=[pl.BlockSpec((tm, D), lambda i, n: (i, 0)),
                  pl.BlockSpec((1, D), lambda i, n: (0, 0)),
                  pl.BlockSpec((D, tn), lambda i, n: (0, n)),
                  pl.BlockSpec((D, tn), lambda i, n: (0, n)),
                  pl.BlockSpec((SUBLANES, tn), lambda i, n: (0, n)),
                  pl.BlockSpec((1, tn), lambda i, n: (0, n)),
                  pl.BlockSpec((tn, D), lambda i, n: (n, 0)),
                  prev_spec],
        out_specs=[pl.BlockSpec((tm, D), lambda i, n: (i, 0)), st_spec],
        out_shape=[jax.ShapeDtypeStruct((M, D), F32), st_shape],
        scratch_shapes=[pltpu.VMEM((tm, D), BF16), pltpu.VMEM((tm, D), F32),
                        pltpu.VMEM((tm + SUBLANES, tn), F32), pltpu.VMEM((nb, SUBLANES, tn), F32)],
        compiler_params=_cparams("arbitrary", "arbitrary"),
        name="conv_ffn",
    )(x, g.reshape(1, D), wu, wg, cw, cb, wd, prev)
    return (y, st) if whole_gate else (y, st[tps - 1::tps])


def _sb_rhs():
    j = _iota((LANES, 2 * LANES), 0)
    s = _iota((LANES, 2 * LANES), 1)
    return jnp.where((j > s) | (s >= LANES), 1.0, 0.0).astype(BF16)


def _sb_block(qm, kb, vb, before, c, rhs):
    z = _dot(qm, kb, NT)
    lf = -(jnp.maximum(z, 0.0) + jnp.log(1.0 + jnp.exp(-jnp.abs(z))))
    lf = jnp.where(before, lf, 0.0)
    cs2 = _dot_hl(lf, rhs)
    between = cs2[:, :LANES] + c
    w = jnp.where(before, jnp.exp(lf + z + between), 0.0)
    return _dot(w.astype(BF16), vb), c + cs2[:, LANES:]


def _sb_prompt_kernel(qi_ref, kj_ref, q_ref, k_ref, v_ref, o_ref, acc_ref, c_ref, *, tq, tk):
    s = pl.program_id(2)
    qi = qi_ref[s]
    kj = kj_ref[s]

    @pl.when(kj == qi)
    def _():
        acc_ref[...] = jnp.zeros_like(acc_ref)
        c_ref[...] = jnp.zeros_like(c_ref)

    lo = _lo_mask()
    q = q_ref[...]
    qms = (jnp.where(lo, q, 0.0).astype(BF16), jnp.where(lo, 0.0, q).astype(BF16))
    rhs = _sb_rhs()
    row = _iota((tq, LANES), 0)
    col = _iota((tq, LANES), 1)
    for sub in reversed(range(tk // LANES)):
        kb = k_ref[pl.ds(sub * LANES, LANES), :].astype(BF16)
        vb = v_ref[pl.ds(sub * LANES, LANES), :].astype(BF16)
        before = (kj * tk + sub * LANES + col) < (qi * tq + row)
        for hh in range(2):
            pv, c_new = _sb_block(qms[hh], kb, vb, before, c_ref[hh], rhs)
            acc_ref[hh] += pv
            c_ref[hh] = c_new

    @pl.when(kj == 0)
    def _():
        o_ref[...] = jnp.where(lo, acc_ref[0], acc_ref[1]).astype(o_ref.dtype)


def _tri_steps(n, lookback=None):
    qs, ks, first, last = [], [], [], []
    for q in range(n):
        k_lo = 0 if lookback is None else max(0, q - lookback)
        for k in range(q, k_lo - 1, -1):
            qs.append(q)
            ks.append(k)
            first.append(int(k == q))
            last.append(int(k == k_lo))
    return tuple(jnp.asarray(np.array(a, np.int32)) for a in (qs, ks, first, last))


def sb_prompt(qkv, B, T, n_heads):
    M = B * T
    tq = tk = _pick(T, (256, 128))
    nq = T // tq
    npair = n_heads // 2
    qs, ks, _, _ = _tri_steps(nq)
    kern = functools.partial(_sb_prompt_kernel, tq=tq, tk=tk)
    return pl.pallas_call(
        kern,
        grid_spec=pltpu.PrefetchScalarGridSpec(
            num_scalar_prefetch=2,
            grid=(B, npair, int(qs.shape[0])),
            in_specs=[pl.BlockSpec((tq, LANES), lambda b, p, s, qi, kj: (b * nq + qi[s], p)),
                      pl.BlockSpec((tk, LANES), lambda b, p, s, qi, kj: (b * nq + kj[s], npair + p)),
                      pl.BlockSpec((tk, LANES), lambda b, p, s, qi, kj: (b * nq + kj[s], 2 * npair + p))],
            out_specs=pl.BlockSpec((tq, LANES), lambda b, p, s, qi, kj: (b * nq + qi[s], p)),
            scratch_shapes=[pltpu.VMEM((2, tq, LANES), F32), pltpu.VMEM((2, tq, LANES), F32)]),
        out_shape=jax.ShapeDtypeStruct((M, n_heads * HEAD_DIM), BF16),
        compiler_params=_cparams("parallel", "parallel", "arbitrary"),
        name="sb_prompt",
    )(qs, ks, qkv, qkv, qkv)


def _block_diag_rows(q8, n_heads):
    rows = n_heads * ROW_GROUP
    width = q8.shape[1]
    tiled = jnp.concatenate([q8] * n_heads, axis=0)
    rh = lax.shift_right_logical(_iota((rows, width), 0), 3)
    ch = lax.shift_right_logical(_iota((rows, width), 1), 6)
    return jnp.where(rh == ch, tiled, 0.0), rh == ch


def _sb_decode_kernel(pt_ref, q_ref, kn_ref, vn_ref, kp_ref, vp_ref, o_ref, acc_ref, c_ref, *,
                      n_heads, n_pages, n_new):
    p = pl.program_id(1)
    rows = n_heads * ROW_GROUP
    width = n_heads * HEAD_DIM

    @pl.when(p == 0)
    def _():
        acc_ref[...] = jnp.zeros_like(acc_ref)
        c_ref[...] = jnp.zeros_like(c_ref)

    qbd, own = _block_diag_rows(q_ref[...], n_heads)
    qbd = qbd.astype(BF16)
    rhs = _sb_rhs()
    r_in = jnp.bitwise_and(_iota((rows, LANES), 0), ROW_GROUP - 1)
    col = _iota((rows, LANES), 1)

    def step(kb, vb, before):
        z = _dot(qbd, kb, NT)
        lf = -(jnp.maximum(z, 0.0) + jnp.log(1.0 + jnp.exp(-jnp.abs(z))))
        lf = jnp.where(before, lf, 0.0)
        cs2 = _dot_hl(lf, rhs)
        between = cs2[:, :LANES] + c_ref[...]
        w = jnp.where(before, jnp.exp(lf + z + between), 0.0)
        acc_ref[...] += _dot(w.astype(BF16), vb)
        c_ref[...] += cs2[:, LANES:]

    @pl.when(p == 0)
    def _():
        pad = jnp.zeros((LANES - ROW_GROUP, width), F32)
        kb = jnp.concatenate([kn_ref[...], pad], axis=0).astype(BF16)
        vb = jnp.concatenate([vn_ref[...], pad], axis=0).astype(BF16)
        before = (col >= ROW_GROUP - n_new) & (col < ROW_GROUP) & (col < r_in)
        step(kb, vb, before)

    @pl.when(p > 0)
    def _():
        step(kp_ref[...].astype(BF16), vp_ref[...].astype(BF16), r_in >= ROW_GROUP - n_new)

    @pl.when(p == n_pages)
    def _():
        m = jnp.where(own, acc_ref[...], 0.0).reshape(n_heads, ROW_GROUP, width)
        o_ref[...] = jnp.sum(m, axis=0).astype(o_ref.dtype)


def sb_decode(qkv, cache, page_table, layer, n_new):
    S, n_pages = page_table.shape
    page = cache.shape[2]
    W = cache.shape[3] // 2
    n_heads = W // HEAD_DIM
    assert page == LANES
    rows = n_heads * ROW_GROUP
    pt = page_table.reshape(-1).astype(jnp.int32)

    def kmap(s, p, pt_ref):
        return (pt_ref[s * n_pages + jnp.minimum(n_pages - p, n_pages - 1)], layer, 0, 0)

    def vmap_(s, p, pt_ref):
        return (pt_ref[s * n_pages + jnp.minimum(n_pages - p, n_pages - 1)], layer, 0, 1)

    kern = functools.partial(_sb_decode_kernel, n_heads=n_heads, n_pages=n_pages, n_new=n_new)
    return pl.pallas_call(
        kern,
        grid_spec=pltpu.PrefetchScalarGridSpec(
            num_scalar_prefetch=1,
            grid=(S, n_pages + 1),
            in_specs=[pl.BlockSpec((ROW_GROUP, W), lambda s, p, pt_ref: (s, 0)),
                      pl.BlockSpec((ROW_GROUP, W), lambda s, p, pt_ref: (s, 1)),
                      pl.BlockSpec((ROW_GROUP, W), lambda s, p, pt_ref: (s, 2)),
                      pl.BlockSpec((None, None, page, W), kmap),
                      pl.BlockSpec((None, None, page, W), vmap_)],
            out_specs=pl.BlockSpec((ROW_GROUP, W), lambda s, p, pt_ref: (s, 0)),
            scratch_shapes=[pltpu.VMEM((rows, W), F32), pltpu.VMEM((rows, LANES), F32)]),
        out_shape=jax.ShapeDtypeStruct((S * ROW_GROUP, W), BF16),
        compiler_params=_cparams("parallel", "arbitrary"),
        name="sb_decode",
    )(pt, qkv, qkv, qkv, cache, cache)


def _nsa_prep_kernel(x_ref, gain_ref, nq_ref, cmp_ref, slc_ref, win_ref):
    seg = _seg_mat(LANES, HEAD_DIM, 1.0 / HEAD_DIM)

    def norm(col, gain_row):
        blk = x_ref[:, pl.ds(col, LANES)]
        ms = _dot_hl(blk * blk, seg)
        return blk * lax.rsqrt(ms + RMS_EPS) * gain_ref[gain_row:gain_row + 1, :]

    for j in range(8):
        nq_ref[:, pl.ds(j * LANES, LANES)] = (norm(j * LANES, 0) * HEAD_DIM ** -0.5).astype(BF16)
    cmp_ref[...] = x_ref[:, pl.ds(1024, 512)]
    for j in range(2):
        slc_ref[:, pl.ds(j * LANES, LANES)] = norm(1536 + j * LANES, 2)
        win_ref[:, pl.ds(j * LANES, LANES)] = norm(2048 + j * LANES, 3)
    slc_ref[:, pl.ds(256, 256)] = x_ref[:, pl.ds(1792, 256)]
    win_ref[:, pl.ds(256, 256)] = x_ref[:, pl.ds(2304, 256)]


def nsa_prep(blk_b, gains):
    M = blk_b.shape[0]
    tm = _pick(M, (512, 256, 128, 64, 32, 16, 8))
    return pl.pallas_call(
        _nsa_prep_kernel,
        grid=(M // tm,),
        in_specs=[pl.BlockSpec((tm, 2560), lambda i: (i, 0)),
                  pl.BlockSpec((SUBLANES, LANES), lambda i: (0, 0))],
        out_specs=[pl.BlockSpec((tm, 1024), lambda i: (i, 0)),
                   pl.BlockSpec((tm, 512), lambda i: (i, 0)),
                   pl.BlockSpec((tm, 512), lambda i: (i, 0)),
                   pl.BlockSpec((tm, 512), lambda i: (i, 0))],
        out_shape=[jax.ShapeDtypeStruct((M, 1024), BF16)] + [jax.ShapeDtypeStruct((M, 512), F32)] * 3,
        compiler_params=_cparams("parallel"),
        name="nsa_prep",
    )(blk_b, gains)


def _compress_kernel(x0_ref, x1_ref, x2_ref, x3_ref, w_ref, o_ref, *, G):
    for p, x_ref in enumerate((x0_ref, x1_ref, x2_ref, x3_ref)):
        acc = jnp.zeros((G * SUBLANES, 2 * LANES), F32)
        for s in range(CMP_STRIDE):
            xs = x_ref[:, pl.ds(s, SUBLANES, stride=CMP_STRIDE), :].reshape(G * SUBLANES, LANES).astype(BF16)
            acc = acc + _dot(xs, w_ref[p // 2, s])
        o_ref[:, pl.ds(p * 2 * LANES, 2 * LANES)] = acc


def compress(pages, layer, w1c):
    n_pages = pages.shape[0]
    G = _pick(n_pages, (16, 8, 5, 4, 3, 2, 1))
    kern = functools.partial(_compress_kernel, G=G)
    return pl.pallas_call(
        kern,
        grid=(n_pages // G,),
        in_specs=[pl.BlockSpec((G, None, LANES, LANES), functools.partial(lambda i, p: (i, layer, 0, p), p=p))
                  for p in range(4)]
        + [pl.BlockSpec((2, CMP_STRIDE, LANES, 2 * LANES), lambda i: (0, 0, 0, 0))],
        out_specs=pl.BlockSpec((G * SUBLANES, 1024), lambda i: (i, 0)),
        out_shape=jax.ShapeDtypeStruct((n_pages * SUBLANES, 1024), F32),
        compiler_params=_cparams("parallel"),
        name="nsa_compress",
    )(pages, pages, pages, pages, w1c)


def _gather_rows_kernel(pt_ref, src_ref, o_ref, sem, *, n_pages):
    s = pl.program_id(0)

    def copy(p):
        return pltpu.make_async_copy(src_ref.at[pl.ds(pt_ref[s * n_pages + p] * SUBLANES, SUBLANES)],
                                     o_ref.at[0, pl.ds(p * SUBLANES, SUBLANES)], sem)

    def start(p, carry):
        copy(p).start()
        return carry

    def wait(p, carry):
        copy(p).wait()
        return carry

    lax.fori_loop(0, n_pages, start, 0)
    lax.fori_loop(0, n_pages, wait, 0)


def gather_page_rows(src, page_table):
    S, n_pages = page_table.shape
    W = src.shape[1]
    kern = functools.partial(_gather_rows_kernel, n_pages=n_pages)
    return pl.pallas_call(
        kern,
        grid_spec=pltpu.PrefetchScalarGridSpec(
            num_scalar_prefetch=1,
            grid=(S,),
            in_specs=[pl.BlockSpec(memory_space=pl.ANY)],
            out_specs=pl.BlockSpec((1, n_pages * SUBLANES, W), lambda s, pt_ref: (s, 0, 0)),
            scratch_shapes=[pltpu.SemaphoreType.DMA(())]),
        out_shape=jax.ShapeDtypeStruct((S, n_pages * SUBLANES, W), F32),
        compiler_params=_cparams("arbitrary"),
        name="gather_page_rows",
    )(page_table.reshape(-1).astype(jnp.int32), src)


def _cmp_finish_kernel(fs_ref, posb_ref, w2_ref, b2_ref, gain_ref, kd_ref, vd_ref, *, nch):
    for p in range(4):
        c, pp = p // 2, p % 2
        first = fs_ref[:, pl.ds(p * 2 * LANES, LANES)]
        second = pltpu.roll(fs_ref[:, pl.ds(p * 2 * LANES + LANES, LANES)], nch - 1, 0)
        hid = jax.nn.gelu(first + second + posb_ref[c:c + 1, :]).astype(BF16)
        for e in range(2):
            out = _dot(hid, w2_ref[c, e]) + b2_ref[c:c + 1, :]
            if c == 0:
                ms = jnp.mean(out * out, axis=-1, keepdims=True)
                kd_ref[pp * 2 + e] = (out * lax.rsqrt(ms + RMS_EPS) * gain_ref[...]).astype(BF16)
            else:
                vd_ref[pp * 2 + e] = out.astype(BF16)


def cmp_finish(fs, posb, w2dup, b2, gain):
    S, nch, _ = fs.shape
    kern = functools.partial(_cmp_finish_kernel, nch=nch)
    small = pl.BlockSpec((SUBLANES, LANES), lambda s: (0, 0))
    return pl.pallas_call(
        kern,
        grid=(S,),
        in_specs=[pl.BlockSpec((None, nch, 1024), lambda s: (s, 0, 0)), small,
                  pl.BlockSpec((2, 2, LANES, LANES), lambda s: (0, 0, 0, 0)), small,
                  pl.BlockSpec((1, LANES), lambda s: (0, 0))],
        out_specs=[pl.BlockSpec((None, 4, nch, LANES), lambda s: (s, 0, 0, 0))] * 2,
        out_shape=[jax.ShapeDtypeStruct((S, 4, nch, LANES), BF16)] * 2,
        compiler_params=_cparams("parallel"),
        name="cmp_finish",
    )(fs, posb, w2dup, b2, gain)


def _head_queries(q, lo):
    out = []
    for g in range(4):
        blk = q[:, (g // 2) * LANES:(g // 2 + 1) * LANES]
        keep = lo if g % 2 == 0 else jnp.logical_not(lo)
        out.append(jnp.where(keep, blk, jnp.zeros_like(blk)))
    return out


def _pair_out(o, lo):
    return jnp.concatenate([jnp.where(lo, o[0], o[1]), jnp.where(lo, o[2], o[3])], axis=1)


def _topk_mask(score, n_sel, k_eff, axis):
    j = _iota(score.shape, axis)
    rank = jnp.zeros(score.shape, jnp.int32)
    for i in range(n_sel):
        si = score[i:i + 1, :] if axis == 0 else score[:, i:i + 1]
        ahead = (si > score) | ((si == score) & (j > i))
        rank = rank + jnp.where(ahead, 1, 0)
    return rank < k_eff


def _cmp_attn_kernel(q_ref, kd_ref, vd_ref, bias_ref, ov_ref, o_ref, sel_ref, *,
                     tq, ncp, n_sel, k_eff, pos0, transposed):
    t0 = pos0 + pl.program_id(2) * tq
    t = t0 + _iota((tq, ncp), 0)
    c = _iota((tq, ncp), 1)
    valid = t - (c * CMP_STRIDE + CMP_BLOCK - 1) >= 0
    lo = _lo_mask()
    kd = kd_ref[...]
    vd = vd_ref[...]
    psum = jnp.zeros((tq, ncp), F32)
    outs = []
    for g, qm in enumerate(_head_queries(q_ref[...], lo)):
        sc = jnp.where(valid, _dot(qm, kd, NT) + bias_ref[g], NEG)
        m = jnp.max(sc, axis=1, keepdims=True)
        e = jnp.where(valid, jnp.exp(sc - m), 0.0)
        den = jnp.sum(e, axis=1, keepdims=True)
        p = e / jnp.where(den > 0, den, 1.0)
        outs.append(_dot(p.astype(BF16), vd))
        psum = psum + p
    o_ref[...] = _pair_out(outs, lo)
    hi, lw = _split(psum)
    if transposed:
        imp = _dot(ov_ref[...], hi, NT) + _dot(ov_ref[...], lw, NT)
        j = _iota(imp.shape, 0)
        tt = t0 + _iota(imp.shape, 1)
    else:
        imp = _dot(hi, ov_ref[...]) + _dot(lw, ov_ref[...])
        j = _iota(imp.shape, 1)
        tt = t0 + _iota(imp.shape, 0)
    causal = j * SEL_BLOCK <= tt
    cur = lax.shift_right_logical(tt, 6)
    forced = causal & ((j == 0) | (j == cur) | (j == cur - 1))
    score = jnp.where(forced, -NEG, jnp.where(causal, imp, NEG))
    score = jnp.where(j < n_sel, score, 2 * NEG)
    sel = _topk_mask(score, n_sel, k_eff, 0 if transposed else 1)
    sel_ref[...] = jnp.where(sel, 1.0, 0.0)


def cmp_attn(nq, kd, vd, bias, ov, *, S, tq, n_q, n_sel, pos0, transposed):
    ncp = kd.shape[2]
    nsp = ov.shape[0] if transposed else ov.shape[1]
    k_eff = min(N_SEL, n_sel)
    rows = S * n_q * tq
    if transposed:
        sel_spec = pl.BlockSpec((None, None, nsp, tq), lambda b, k, i: (b, k, 0, i))
        sel_shape = jax.ShapeDtypeStruct((S, 4, nsp, n_q * tq), F32)
    else:
        sel_spec = pl.BlockSpec((None, None, tq, nsp), lambda b, k, i: (b, k, i, 0))
        sel_shape = jax.ShapeDtypeStruct((S, 4, n_q * tq, nsp), F32)
    kern = functools.partial(_cmp_attn_kernel, tq=tq, ncp=ncp, n_sel=n_sel, k_eff=k_eff, pos0=pos0,
                             transposed=transposed)
    return pl.pallas_call(
        kern,
        grid=(S, 4, n_q),
        in_specs=[pl.BlockSpec((tq, 2 * LANES), lambda b, k, i: (b * n_q + i, k)),
                  pl.BlockSpec((None, None, ncp, LANES), lambda b, k, i: (b, k, 0, 0)),
                  pl.BlockSpec((None, None, ncp, LANES), lambda b, k, i: (b, k, 0, 0)),
                  pl.BlockSpec((4, tq, ncp), lambda b, k, i: (k, i, 0)),
                  pl.BlockSpec(ov.shape, lambda b, k, i: (0, 0))],
        out_specs=[pl.BlockSpec((tq, 2 * LANES), lambda b, k, i: (b * n_q + i, k)), sel_spec],
        out_shape=[jax.ShapeDtypeStruct((rows, 1024), F32), sel_shape],
        compiler_params=_cparams("parallel", "parallel", "arbitrary"),
        name="cmp_attn",
    )(nq, kd, vd, bias, ov)


def _dup_half(x, odd):
    lo = _lo_mask()
    take_x = jnp.logical_xor(lo, odd)
    return jnp.where(take_x, x, pltpu.roll(x, HEAD_DIM, 1))


def _flash_kernel(qi_ref, kj_ref, dl_ref, first_ref, last_ref, q_ref, k_ref, v_ref, bias_ref, *rest,
                  tq, tk, mode):
    if mode == "slc":
        sel_ref, o_ref, m_ref, l_ref, acc_ref = rest
    else:
        o_ref, m_ref, l_ref, acc_ref = rest
    kvh = pl.program_id(1)
    s = pl.program_id(2)
    qi = qi_ref[s]
    kj = kj_ref[s]

    @pl.when(first_ref[s] == 1)
    def _():
        m_ref[...] = jnp.full_like(m_ref, NEG)
        l_ref[...] = jnp.zeros_like(l_ref)
        acc_ref[...] = jnp.zeros_like(acc_ref)

    lo = _lo_mask()
    odd = (kvh % 2) == 1
    kd = _dup_half(k_ref[...], odd).astype(BF16)
    vd = _dup_half(v_ref[...], odd).astype(BF16)
    d = (qi * tq + _iota((tq, tk), 0)) - (kj * tk + _iota((tq, tk), 1))
    if mode == "slc":
        nsp = sel_ref.shape[0]
        blk = kj * (tk // SEL_BLOCK) + lax.shift_right_logical(_iota((nsp, tk), 1), 6)
        expand = jnp.where(_iota((nsp, tk), 0) == blk, 1.0, 0.0).astype(BF16)
        chosen = _dot(sel_ref[...].astype(BF16), expand, TN)
        valid = (chosen > 0.5) & (d >= 0)
    else:
        valid = (d >= 0) & (d < WINDOW)
    for g, qm in enumerate(_head_queries(q_ref[...], lo)):
        sc = jnp.where(valid, _dot(qm, kd, NT) + bias_ref[g], NEG)
        m_old = m_ref[g]
        m_new = jnp.maximum(m_old, jnp.max(sc, axis=1, keepdims=True))
        p = jnp.where(valid, jnp.exp(sc - m_new), 0.0)
        alpha = jnp.exp(m_old - m_new)
        l_ref[g] = alpha * l_ref[g] + jnp.sum(p, axis=1, keepdims=True)
        acc_ref[g] = alpha * acc_ref[g] + _dot(p.astype(BF16), vd)
        m_ref[g] = m_new

    @pl.when(last_ref[s] == 1)
    def _():
        outs = []
        for g in range(4):
            den = l_ref[g]
            outs.append(acc_ref[g] / jnp.where(den > 0, den, 1.0))
        o_ref[...] = _pair_out(outs, lo)


def flash_prompt(nq, kv, bias_tiles, sel, *, B, T, mode):
    tq = tk = bias_tiles.shape[2]
    nq_t = T // tq
    nd = bias_tiles.shape[1]
    lookback = None if mode == "slc" else -(-(WINDOW - 1) // tk)
    qs, ks, first, last = _tri_steps(nq_t, lookback)
    dl = jnp.minimum(qs - ks, nd - 1)
    n_steps = int(qs.shape[0])
    in_specs = [pl.BlockSpec((tq, 2 * LANES), lambda b, k, s, qi, kj, dl, f, l: (b * nq_t + qi[s], k)),
                pl.BlockSpec((tk, LANES), lambda b, k, s, qi, kj, dl, f, l: (b * nq_t + kj[s], k // 2)),
                pl.BlockSpec((tk, LANES), lambda b, k, s, qi, kj, dl, f, l: (b * nq_t + kj[s], 2 + k // 2)),
                pl.BlockSpec((4, None, tq, tk), lambda b, k, s, qi, kj, dl, f, l: (k, dl[s], 0, 0))]
    args = [nq, kv, kv, bias_tiles]
    if mode == "slc":
        nsp = sel.shape[2]
        in_specs.append(pl.BlockSpec((None, None, nsp, tq), lambda b, k, s, qi, kj, dl, f, l: (b, k, 0, qi[s])))
        args.append(sel)
    kern = functools.partial(_flash_kernel, tq=tq, tk=tk, mode=mode)
    return pl.pallas_call(
        kern,
        grid_spec=pltpu.PrefetchScalarGridSpec(
            num_scalar_prefetch=5,
            grid=(B, 4, n_steps),
            in_specs=in_specs,
            out_specs=pl.BlockSpec((tq, 2 * LANES), lambda b, k, s, qi, kj, dl, f, l: (b * nq_t + qi[s], k)),
            scratch_shapes=[pltpu.VMEM((4, tq, 1), F32), pltpu.VMEM((4, tq, 1), F32),
                            pltpu.VMEM((4, tq, LANES), F32)]),
        out_shape=jax.ShapeDtypeStruct((B * T, 1024), F32),
        compiler_params=_cparams("parallel", "parallel", "arbitrary"),
        name="nsa_" + mode,
    )(qs, ks, dl, first, last, *args)


def _kv_queries(q8, n_heads, n_kv):
    qbd, _ = _block_diag_rows(q8, n_heads)
    wq, wk = n_heads * HEAD_DIM, n_kv * HEAD_DIM
    gshift = int(math.log2(n_heads // n_kv)) + 6
    r = _iota((wq, wk), 0)
    c = _iota((wq, wk), 1)
    fold = jnp.where((lax.shift_right_logical(r, gshift) == lax.shift_right_logical(c, 6))
                     & (jnp.bitwise_and(r, 63) == jnp.bitwise_and(c, 63)), 1.0, 0.0).astype(BF16)
    return _dot(qbd.astype(BF16), fold).astype(BF16)


def _kv_outputs(o, n_heads, n_kv):
    rows = n_heads * ROW_GROUP
    wq, wk = n_heads * HEAD_DIM, n_kv * HEAD_DIM
    gshift = int(math.log2(n_heads // n_kv))
    own_kv = lax.shift_right_logical(_iota((rows, wk), 0), 3 + gshift) == lax.shift_right_logical(_iota((rows, wk), 1), 6)
    r = _iota((wk, wq), 0)
    c = _iota((wk, wq), 1)
    unfold = jnp.where((lax.shift_right_logical(c, gshift + 6) == lax.shift_right_logical(r, 6))
                       & (jnp.bitwise_and(r, 63) == jnp.bitwise_and(c, 63)), 1.0, 0.0).astype(BF16)
    wide = _dot_hl(jnp.where(own_kv, o, 0.0), unfold)
    own = lax.shift_right_logical(_iota((rows, wq), 0), 3) == lax.shift_right_logical(_iota((rows, wq), 1), 6)
    return jnp.sum(jnp.where(own, wide, 0.0).reshape(n_heads, ROW_GROUP, wq), axis=0)


def _softmax_step(sc, valid, vb, m_ref, l_ref, acc_ref):
    sc = jnp.where(valid, sc, NEG)
    m_old = m_ref[...]
    m_new = jnp.maximum(m_old, jnp.max(sc, axis=1, keepdims=True))
    p = jnp.where(valid, jnp.exp(sc - m_new), 0.0)
    alpha = jnp.exp(m_old - m_new)
    l_ref[...] = alpha * l_ref[...] + jnp.sum(p, axis=1, keepdims=True)
    acc_ref[...] = alpha * acc_ref[...] + _dot(p.astype(BF16), vb)
    m_ref[...] = m_new


def _new_rows(ref, cols):
    blk = ref[:, cols]
    return jnp.concatenate([blk, jnp.zeros((LANES - ROW_GROUP, blk.shape[1]), F32)], axis=0).astype(BF16)


def _slc_decode_kernel(pt_ref, q_ref, kvn_ref, sel_ref, bias_ref, kv_ref, o_ref,
                       m_ref, l_ref, acc_ref, qbd_ref, *, n_heads, n_kv, n_pages, n_new):
    p = pl.program_id(1)
    rows = n_heads * ROW_GROUP
    wk = n_kv * HEAD_DIM
    group = n_heads // n_kv

    @pl.when(p == 0)
    def _():
        m_ref[...] = jnp.full_like(m_ref, NEG)
        l_ref[...] = jnp.zeros_like(l_ref)
        acc_ref[...] = jnp.zeros_like(acc_ref)
        qbd_ref[...] = _kv_queries(q_ref[...].astype(F32), n_heads, n_kv)

    nsp = sel_ref.shape[2]
    sel_rows = jnp.concatenate([sel_ref[k] for k in range(n_kv) for _ in range(group)], axis=0).astype(BF16)
    page = jnp.where(p == 0, n_pages, n_pages - p)
    blk = page * (LANES // SEL_BLOCK) + lax.shift_right_logical(_iota((nsp, LANES), 1), 6)
    expand = jnp.where(_iota((nsp, LANES), 0) == blk, 1.0, 0.0).astype(BF16)
    chosen = _dot(sel_rows, expand) > 0.5
    r_in = jnp.bitwise_and(_iota((rows, LANES), 0), ROW_GROUP - 1)
    col = _iota((rows, LANES), 1)
    qbd = qbd_ref[...]

    @pl.when(p == 0)
    def _():
        kb = _new_rows(kvn_ref, slice(0, wk))
        vb = _new_rows(kvn_ref, slice(wk, 2 * wk))
        valid = chosen & (col >= ROW_GROUP - n_new) & (col < ROW_GROUP) & (col <= r_in)
        _softmax_step(_dot(qbd, kb, NT) + bias_ref[...], valid, vb, m_ref, l_ref, acc_ref)

    @pl.when(p > 0)
    def _():
        kb = kv_ref[:, pl.ds(0, wk)].astype(BF16)
        vb = kv_ref[:, pl.ds(wk, wk)].astype(BF16)
        _softmax_step(_dot(qbd, kb, NT) + bias_ref[...], chosen, vb, m_ref, l_ref, acc_ref)

    @pl.when(p == n_pages)
    def _():
        den = l_ref[...]
        o_ref[...] = _kv_outputs(acc_ref[...] / jnp.where(den > 0, den, 1.0), n_heads, n_kv)


def slc_decode(nq, kv_new, sel, bias, cache, page_table, layer, n_new):
    S, n_pages = page_table.shape
    n_kv = cache.shape[3] // (2 * HEAD_DIM)
    n_heads = nq.shape[1] // HEAD_DIM
    rows = n_heads * ROW_GROUP
    nsp = sel.shape[3]
    pt = page_table.reshape(-1).astype(jnp.int32)
    kern = functools.partial(_slc_decode_kernel, n_heads=n_heads, n_kv=n_kv, n_pages=n_pages, n_new=n_new)
    return pl.pallas_call(
        kern,
        grid_spec=pltpu.PrefetchScalarGridSpec(
            num_scalar_prefetch=1,
            grid=(S, n_pages + 1),
            in_specs=[pl.BlockSpec((ROW_GROUP, nq.shape[1]), lambda s, p, pt_ref: (s, 0)),
                      pl.BlockSpec((ROW_GROUP, kv_new.shape[1]), lambda s, p, pt_ref: (s, 0)),
                      pl.BlockSpec((None, n_kv, ROW_GROUP, nsp), lambda s, p, pt_ref: (s, 0, 0, 0)),
                      pl.BlockSpec((rows, LANES),
                                   lambda s, p, pt_ref: (0, jnp.where(p == 0, n_pages, n_pages - p))),
                      pl.BlockSpec((None, None, LANES, cache.shape[3]),
                                   lambda s, p, pt_ref: (pt_ref[s * n_pages + jnp.minimum(n_pages - p, n_pages - 1)],
                                                         layer, 0, 0))],
            out_specs=pl.BlockSpec((ROW_GROUP, nq.shape[1]), lambda s, p, pt_ref: (s, 0)),
            scratch_shapes=[pltpu.VMEM((rows, 1), F32), pltpu.VMEM((rows, 1), F32),
                            pltpu.VMEM((rows, n_kv * HEAD_DIM), F32),
                            pltpu.VMEM((rows, n_kv * HEAD_DIM), BF16)]),
        out_shape=jax.ShapeDtypeStruct((S * ROW_GROUP, nq.shape[1]), F32),
        compiler_params=_cparams("parallel", "arbitrary"),
        name="slc_decode",
    )(pt, nq, kv_new, sel, bias, cache)


def _win_decode_kernel(q_ref, kvn_ref, bias_ref, kv_ref, o_ref, m_ref, l_ref, acc_ref, *,
                       n_heads, n_kv, n_new, layer_rows):
    rows = n_heads * ROW_GROUP
    wk = n_kv * HEAD_DIM
    m_ref[...] = jnp.full_like(m_ref, NEG)
    l_ref[...] = jnp.zeros_like(l_ref)
    acc_ref[...] = jnp.zeros_like(acc_ref)
    qbd = _kv_queries(q_ref[...].astype(F32), n_heads, n_kv)
    i_q = jnp.bitwise_and(_iota((rows, layer_rows), 0), ROW_GROUP - 1) - (ROW_GROUP - n_new)
    j = _iota((rows, layer_rows), 1)
    dist = layer_rows + i_q - j
    valid = (dist >= 0) & (dist < WINDOW)
    kb = kv_ref[:, pl.ds(0, wk)].astype(BF16)
    vb = kv_ref[:, pl.ds(wk, wk)].astype(BF16)
    _softmax_step(_dot(qbd, kb, NT) + bias_ref[:, pl.ds(0, layer_rows)], valid, vb, m_ref, l_ref, acc_ref)
    r_in = jnp.bitwise_and(_iota((rows, LANES), 0), ROW_GROUP - 1)
    col = _iota((rows, LANES), 1)
    valid = (col >= ROW_GROUP - n_new) & (col < ROW_GROUP) & (col <= r_in)
    kb = _new_rows(kvn_ref, slice(0, wk))
    vb = _new_rows(kvn_ref, slice(wk, 2 * wk))
    _softmax_step(_dot(qbd, kb, NT) + bias_ref[:, pl.ds(layer_rows, LANES)], valid, vb, m_ref, l_ref, acc_ref)
    den = l_ref[...]
    o_ref[...] = _kv_outputs(acc_ref[...] / jnp.where(den > 0, den, 1.0), n_heads, n_kv)


def win_decode(nq, kv_new, bias, cache_win, layer, n_new):
    S, _, wc, width = cache_win.shape
    n_kv = width // (2 * HEAD_DIM)
    n_heads = nq.shape[1] // HEAD_DIM
    rows = n_heads * ROW_GROUP
    kern = functools.partial(_win_decode_kernel, n_heads=n_heads, n_kv=n_kv, n_new=n_new, layer_rows=wc)
    return pl.pallas_call(
        kern,
        grid=(S,),
        in_specs=[pl.BlockSpec((ROW_GROUP, nq.shape[1]), lambda s: (s, 0)),
                  pl.BlockSpec((ROW_GROUP, width), lambda s: (s, 0)),
                  pl.BlockSpec((rows, wc + LANES), lambda s: (0, 0)),
                  pl.BlockSpec((None, None, wc, width), lambda s: (s, layer, 0, 0))],
        out_specs=pl.BlockSpec((ROW_GROUP, nq.shape[1]), lambda s: (s, 0)),
        out_shape=jax.ShapeDtypeStruct((S * ROW_GROUP, nq.shape[1]), F32),
        scratch_shapes=[pltpu.VMEM((rows, 1), F32), pltpu.VMEM((rows, 1), F32),
                        pltpu.VMEM((rows, n_kv * HEAD_DIM), F32)],
        compiler_params=_cparams("parallel"),
        name="win_decode",
    )(nq, kv_new, bias, cache_win)


def _nsa_combine_kernel(oc_ref, os_ref, ow_ref, gate_ref, o_ref, *, lane0, n_heads):
    hi, lw = _split(jax.nn.sigmoid(gate_ref[...]))
    width = n_heads * HEAD_DIM
    acc = jnp.zeros(oc_ref.shape, F32)
    for br, ref in enumerate((oc_ref, os_ref, ow_ref)):
        src = lane0 + br * n_heads + lax.shift_right_logical(_iota((LANES, width), 1), 6)
        expand = jnp.where(_iota((LANES, width), 0) == src, 1.0, 0.0).astype(BF16)
        acc = acc + (_dot(hi, expand) + _dot(lw, expand)) * ref[...]
    o_ref[...] = acc.astype(o_ref.dtype)


def nsa_combine(o_cmp, o_slc, o_win, blk_c, *, gate_block, lane0):
    M, width = o_cmp.shape
    tm = _pick(M, (512, 256, 128, 64, 32, 16, 8))
    kern = functools.partial(_nsa_combine_kernel, lane0=lane0, n_heads=width // HEAD_DIM)
    ospec = pl.BlockSpec((tm, width), lambda i: (i, 0))
    return pl.pallas_call(
        kern,
        grid=(M // tm,),
        in_specs=[ospec, ospec, ospec, pl.BlockSpec((tm, LANES), lambda i: (i, gate_block))],
        out_specs=ospec,
        out_shape=jax.ShapeDtypeStruct((M, width), BF16),
        compiler_params=_cparams("parallel"),
        name="nsa_combine",
    )(o_cmp, o_slc, o_win, blk_c)


RW_COLS = 3584


def _rwkv_prep_kernel(c_ref, head_ref, mu_ref, vec_ref, w2_ref, a2_ref, g2_ref,
                      r_ref, lw_ref, k_ref, v_ref, kk_ref, b_ref, bonus_ref, g_ref,
                      xs_ref, carry_ref, *, tm, tps, whole, n_new):
    i = pl.program_id(0)
    cols = c_ref[...]
    if whole:
        cols = cols + head_ref[...]
        head = jnp.zeros((SUBLANES, RW_COLS), F32)
    else:
        head = jnp.where(i % tps == 0, head_ref[0], carry_ref[...])
    xs_ref[pl.ds(0, SUBLANES), :] = head
    xs_ref[pl.ds(SUBLANES, tm), :] = cols
    prev = xs_ref[pl.ds(SUBLANES - 1, tm), :]
    if not whole:
        carry_ref[...] = xs_ref[pl.ds(tm, SUBLANES), :]
    xs_ref[pl.ds(SUBLANES, tm), :] = cols + (prev - cols) * mu_ref[...]
    small = xs_ref[pl.ds(SUBLANES, tm), pl.ds(3328, 256)]
    th = jnp.tanh(small).astype(BF16)
    sm = small.astype(BF16)
    sg = jax.nn.sigmoid(xs_ref[pl.ds(SUBLANES, tm), pl.ds(3072, 256)]).astype(BF16)
    seg = _seg_mat(LANES, HEAD_DIM, 1.0)
    if whole:
        real = jnp.bitwise_and(_iota((tm, LANES), 0), ROW_GROUP - 1) >= ROW_GROUP - n_new
    for j in range(8):
        cs = pl.ds(j * LANES, LANES)
        r = xs_ref[pl.ds(SUBLANES, tm), pl.ds(j * LANES, LANES)]
        k = xs_ref[pl.ds(SUBLANES, tm), pl.ds(1024 + j * LANES, LANES)]
        v = xs_ref[pl.ds(SUBLANES, tm), pl.ds(2048 + j * LANES, LANES)]
        y = vec_ref[0:1, cs] + _dot(th, w2_ref[:, cs])
        w_log = -(jnp.maximum(-y, 0.0) + jnp.log(1.0 + jnp.exp(-jnp.abs(y)))) - 0.5
        lw = -jnp.exp(w_log)
        a = jax.nn.sigmoid(vec_ref[1:2, cs] + _dot(sm, a2_ref[:, cs]))
        g = _dot(sg, g2_ref[:, cs])
        kk = k * vec_ref[2:3, cs]
        kk = kk * lax.rsqrt(jnp.maximum(_dot_hl(kk * kk, seg), 1e-24))
        k2 = k * (1.0 + (a - 1.0) * vec_ref[3:4, cs])
        b = kk * a
        bonus = _dot_hl(r * k2 * vec_ref[4:5, cs], seg) * v
        if whole:
            r, k2, v, kk, b, lw = [jnp.where(real, t, 0.0) for t in (r, k2, v, kk, b, lw)]
        r_ref[:, cs] = r
        lw_ref[:, cs] = lw
        k_ref[:, cs] = k2
        v_ref[:, cs] = v
        kk_ref[:, cs] = kk
        b_ref[:, cs] = b
        bonus_ref[:, cs] = bonus
        g_ref[:, cs] = g


def rwkv_prep(blk_c, head, mu, vec, w2p, a2p, g2, *, seq_rows, whole, n_new=0):
    M = blk_c.shape[0]
    if whole:
        tm, tps = M, 1
        head_spec = pl.BlockSpec((tm, RW_COLS), lambda i: (0, 0))
    else:
        tm = _pick(seq_rows, (256, 128, 64))
        tps = seq_rows // tm
        head_spec = pl.BlockSpec((1, SUBLANES, RW_COLS), lambda i: (i // tps, 0, 0))
    kern = functools.partial(_rwkv_prep_kernel, tm=tm, tps=tps, whole=whole, n_new=n_new)
    full = lambda shape: pl.BlockSpec(shape, lambda i: (0,) * len(shape))
    return pl.pallas_call(
        kern,
        grid=(M // tm,),
        in_specs=[pl.BlockSpec((tm, RW_COLS), lambda i: (i, 0)), head_spec, full((1, RW_COLS)),
                  full((SUBLANES, 1024)), full((256, 1024)), full((256, 1024)), full((256, 1024))],
        out_specs=[pl.BlockSpec((tm, 1024), lambda i: (i, 0))] * 8,
        out_shape=[jax.ShapeDtypeStruct((M, 1024), F32)] * 8,
        scratch_shapes=[pltpu.VMEM((tm + SUBLANES, RW_COLS), F32), pltpu.VMEM((SUBLANES, RW_COLS), F32)],
        compiler_params=_cparams("arbitrary"),
        name="rwkv_prep",
    )(blk_c, head, mu, vec, w2p, a2p, g2)


def _rwkv_par_kernel(r_ref, lw_ref, k_ref, v_ref, kk_ref, b_ref, r2_ref, y2_ref, m_ref, g_ref, *, C, npair):
    C2 = 2 * C
    cum = jnp.where(_iota((C, C), 0) >= _iota((C, C), 1), 1.0, 0.0).astype(BF16)
    keep = (_iota((C2, LANES), 0) < C) == (_iota((C2, LANES), 1) < HEAD_DIM)
    rb = _iota((C2, C2), 0)
    cb = _iota((C2, C2), 1)
    same = (rb < C) == (cb < C)
    rr = jnp.bitwise_and(rb, C - 1)
    cc = jnp.bitwise_and(cb, C - 1)
    strict = same & (rr > cc)
    incl = same & (rr >= cc)
    eye2 = jnp.where(rb == cb, 1.0, 0.0)
    eye_l = _iota((LANES, LANES), 0) == _iota((LANES, LANES), 1)

    def stack(x):
        return jnp.where(keep, jnp.concatenate([x, x], axis=0), 0.0)

    for p in range(npair):
        cs = pl.ds(p * LANES, LANES)
        lw = lw_ref[:, cs]
        hi, lo = _split(lw)
        log_p = _dot(cum, hi) + _dot(cum, lo)
        log_end = log_p[C - 1:C, :]
        e_pos = jnp.exp(log_p)
        e_neg = jnp.exp(-log_p)
        r, k, v, kk, b = r_ref[:, cs], k_ref[:, cs], v_ref[:, cs], kk_ref[:, cs], b_ref[:, cs]
        rt = stack(r * e_pos)
        kt = stack(k * e_neg)
        bt = stack(b * e_neg)
        at = stack(-kk * jnp.exp(log_p - lw))
        e_end = jnp.exp(log_end - log_p)
        kend = stack(k * e_end)
        bend = stack(b * e_end)
        v2 = stack(v)
        a_ak = jnp.where(strict, _dot3(at, kt, NT), 0.0)
        a_ab = jnp.where(strict, _dot3(at, bt, NT), 0.0)
        a_rk = jnp.where(incl, _dot3(rt, kt, NT), 0.0)
        a_rb = jnp.where(incl, _dot3(rt, bt, NT), 0.0)
        inv = eye2 + a_ab
        powr = a_ab
        for _ in range(int(math.log2(C)) - 1):
            powr = _dot3(powr, powr)
            inv = inv + _dot3(inv, powr)
        a_eff = _dot3(inv, at)
        u0 = _dot3(inv, _dot3(a_ak, v2))
        r2_ref[:, cs] = rt + _dot3(a_rb, a_eff)
        y2_ref[:, cs] = _dot3(a_rk, v2) + _dot3(a_rb, u0)
        decay_end = jnp.where(eye_l, jnp.broadcast_to(jnp.exp(log_end), (LANES, LANES)), 0.0)
        m_ref[:, cs] = decay_end + _dot3(a_eff, bend, TN)
        g_ref[:, cs] = _dot3(v2, kend, TN) + _dot3(u0, bend, TN)


def rwkv_par(r, lw, k, v, kk, b, C):
    M, W = r.shape
    npair = W // LANES
    nch = M // C
    kern = functools.partial(_rwkv_par_kernel, C=C, npair=npair)
    ispec = pl.BlockSpec((C, W), lambda i: (i, 0))
    return pl.pallas_call(
        kern,
        grid=(nch,),
        in_specs=[ispec] * 6,
        out_specs=[pl.BlockSpec((None, 2 * C, W), lambda i: (i, 0, 0))] * 2
        + [pl.BlockSpec((None, LANES, W), lambda i: (i, 0, 0))] * 2,
        out_shape=[jax.ShapeDtypeStruct((nch, 2 * C, W), F32)] * 2
        + [jax.ShapeDtypeStruct((nch, LANES, W), F32)] * 2,
        compiler_params=_cparams("parallel"),
        name="rwkv_par",
    )(r, lw, k, v, kk, b)


def _rwkv_seq_kernel(r2_ref, y2_ref, m_ref, g_ref, s0_ref, bonus_ref, gate_ref, ln_ref,
                     o_ref, sf_ref, s_ref, *, C, npair):
    c = pl.program_id(1)

    @pl.when(c == 0)
    def _():
        s_ref[...] = s0_ref[...]

    seg = _seg_mat(LANES, HEAD_DIM, 1.0 / HEAD_DIM)
    for p in range(npair):
        cs = pl.ds(p * LANES, LANES)
        st = s_ref[:, cs]
        y2 = _dot3(r2_ref[:, cs], st, NT) + y2_ref[:, cs]
        y = y2[:C] + y2[C:]
        mu = _dot_hl(y, seg)
        dev = y - mu
        var = _dot_hl(dev * dev, seg)
        yn = dev * lax.rsqrt(var + GN_EPS) * ln_ref[0:1, cs] + ln_ref[1:2, cs]
        o_ref[:, cs] = ((yn + bonus_ref[:, cs]) * gate_ref[:, cs]).astype(o_ref.dtype)
        s_ref[:, cs] = _dot3(st, m_ref[:, cs]) + g_ref[:, cs]

    @pl.when(c == pl.num_programs(1) - 1)
    def _():
        sf_ref[...] = s_ref[...]


def rwkv_seq(r2, y2, mt, gt, s0, bonus, gate, ln, *, n_seq):
    nch_total, C2, W = r2.shape
    C = C2 // 2
    nch = nch_total // n_seq
    npair = W // LANES
    kern = functools.partial(_rwkv_seq_kernel, C=C, npair=npair)
    cspec = lambda rows: pl.BlockSpec((None, rows, W), lambda s, c: (s * nch + c, 0, 0))
    return pl.pallas_call(
        kern,
        grid=(n_seq, nch),
        in_specs=[cspec(C2), cspec(C2), cspec(LANES), cspec(LANES),
                  pl.BlockSpec((None, LANES, W), lambda s, c: (s, 0, 0)),
                  pl.BlockSpec((C, W), lambda s, c: (s * nch + c, 0)),
                  pl.BlockSpec((C, W), lambda s, c: (s * nch + c, 0)),
                  pl.BlockSpec((SUBLANES, W), lambda s, c: (0, 0))],
        out_specs=[pl.BlockSpec((C, W), lambda s, c: (s * nch + c, 0)),
                   pl.BlockSpec((None, LANES, W), lambda s, c: (s, 0, 0))],
        out_shape=[jax.ShapeDtypeStruct((nch_total * C, W), BF16),
                   jax.ShapeDtypeStruct((n_seq, LANES, W), F32)],
        scratch_shapes=[pltpu.VMEM((LANES, W), F32)],
        compiler_params=_cparams("parallel", "arbitrary"),
        name="rwkv_seq",
    )(r2, y2, mt, gt, s0, bonus, gate, ln)


def _bucket_np(d):
    d = np.maximum(d, 0)
    ratio = np.log(np.maximum(d, 1).astype(np.float32) / np.float32(N_BUCKETS // 2)) / np.float32(
        math.log(MAX_DISTANCE / (N_BUCKETS // 2)))
    large = np.minimum(N_BUCKETS // 2 + (ratio * np.float32(N_BUCKETS - N_BUCKETS // 2)).astype(np.int32),
                       N_BUCKETS - 1)
    return np.where(d < N_BUCKETS // 2, d, large).astype(np.int32)


def _overlap_np(n_cmp, n_sel):
    c0 = np.arange(n_cmp)[:, None] * CMP_STRIDE
    j0 = np.arange(n_sel)[None, :] * SEL_BLOCK
    ov = np.clip(np.minimum(c0 + CMP_BLOCK, j0 + SEL_BLOCK) - np.maximum(c0, j0), 0, None)
    return (ov / CMP_BLOCK).astype(np.float32)


def _const_from(d_bucket, n_far):
    return n_far


def _make_tables(rel_bias, T, past, n_new, n_heads):
    tab_h = rel_bias.astype(F32).T
    take = lambda idx: jnp.take(tab_h, jnp.asarray(idx), axis=1)
    t = {}
    tq = _pick(T, (256, 128))
    nq_t = T // tq
    far = 0
    while _bucket_np(np.array([far]))[0] < N_BUCKETS - 1:
        far += 1
    nd = min(-(-(far + tq - 1) // tq) + 1, nq_t)
    i = np.arange(tq)[:, None]
    j = np.arange(tq)[None, :]
    t["tiles"] = take(np.stack([_bucket_np(dl * tq + i - j) for dl in range(nd)]))
    ncp = T // CMP_STRIDE
    tt = np.arange(T)[:, None]
    cc = np.arange(ncp)[None, :]
    t["cmp_p"] = take(_bucket_np(tt - (cc * CMP_STRIDE + CMP_BLOCK - 1)))
    n_sel = -(-T // SEL_BLOCK)
    nsp = -(-n_sel // SUBLANES) * SUBLANES
    ov = np.zeros((nsp, ncp), np.float32)
    ov[:n_sel, :ncp - 1] = _overlap_np(ncp - 1, n_sel).T
    t["ov_p"] = jnp.asarray(ov, BF16)
    t["n_sel_p"] = n_sel
    t["tq"] = tq
    rows = n_heads * ROW_GROUP
    tpos = past - (ROW_GROUP - n_new) + np.arange(ROW_GROUP)
    kpos = np.concatenate([np.arange(past), past - (ROW_GROUP - n_new) + np.arange(LANES)])
    idx = _bucket_np(tpos[:, None] - kpos[None, :])
    t["slc_d"] = take(idx).reshape(rows, past + LANES)
    wc = min(WINDOW, past)
    kpos = np.concatenate([past - wc + np.arange(wc), past - (ROW_GROUP - n_new) + np.arange(LANES)])
    t["win_d"] = take(_bucket_np(tpos[:, None] - kpos[None, :])).reshape(rows, wc + LANES)
    ncp_d = past // CMP_STRIDE
    cc = np.arange(ncp_d)[None, :]
    t["cmp_d"] = take(_bucket_np(tpos[:, None] - (cc * CMP_STRIDE + CMP_BLOCK - 1)))
    n_sel_d = -(-(past + n_new) // SEL_BLOCK)
    nsp_d = -(-n_sel_d // LANES) * LANES
    n_cmp_d = (past + n_new) // CMP_STRIDE - 1
    ov = np.zeros((ncp_d, nsp_d), np.float32)
    ov[:n_cmp_d, :n_sel_d] = _overlap_np(n_cmp_d, n_sel_d)
    t["ov_d"] = jnp.asarray(ov, BF16)
    t["n_sel_d"] = n_sel_d
    return t


def _prep_layer(l, P):
    W = {}
    w_in = P["w_in"][l]
    sbw = 1024
    W["wA"] = jnp.concatenate([w_in[:, :sbw] * HEAD_DIM ** -0.5, w_in[:, sbw:3 * sbw]], axis=1).astype(BF16)
    W["wB"] = w_in[:, 3072:5632].astype(BF16)
    D = w_in.shape[0]
    W["wC"] = jnp.concatenate([w_in[:, 5680:8752], w_in[:, 8944:9200], w_in[:, 8752:8944],
                               w_in[:, 5632:5680], jnp.zeros((D, 16), F32)], axis=1).astype(BF16)
    W["wD"] = w_in[:, 9200:].astype(BF16)
    W["norm_mix"] = P["norm_mix_g"][l]
    W["norm_ffn"] = P["norm_ffn_g"][l]
    gains = P["nsa_qk_gain"][l]
    W["gains"] = jnp.pad(jnp.tile(gains, (1, 2)), ((0, 4), (0, 0)))
    W["gain1"] = jnp.tile(gains[1:2], (1, 2))
    w1 = P["cmp_w1"][l].reshape(2, 2, CMP_STRIDE, HEAD_DIM, HEAD_DIM)
    first, second = w1[:, 0], w1[:, 1]
    z = jnp.zeros_like(first)
    W["w1c"] = jnp.concatenate([jnp.concatenate([first, z, second, z], axis=-1),
                                jnp.concatenate([z, first, z, second], axis=-1)], axis=-2).astype(BF16)
    pos = jnp.einsum("cld,clde->ce", P["cmp_pe"][l], P["cmp_w1"][l], precision=lax.Precision.HIGHEST)
    W["posb"] = jnp.pad(jnp.tile(pos + P["cmp_b1"][l], (1, 2)), ((0, 6), (0, 0)))
    w2 = P["cmp_w2"][l]
    w2d = jnp.concatenate([w2, w2], axis=-1)
    z2 = jnp.zeros_like(w2d)
    W["w2dup"] = jnp.stack([jnp.concatenate([w2d, z2], axis=1), jnp.concatenate([z2, w2d], axis=1)],
                           axis=1).astype(BF16)
    W["b2"] = jnp.pad(jnp.tile(P["cmp_b2"][l], (1, 2)), ((0, 6), (0, 0)))
    mu = P["rwkv_mu"][l]
    W["mu"] = jnp.concatenate([mu[:3072], mu[3264:3520], mu[3072:3264], jnp.zeros((64,), F32)])[None]
    W["vec"] = jnp.pad(jnp.stack([P["rwkv_w0"][l], P["rwkv_a0"][l], P["rwkv_kk"][l], P["rwkv_ka"][l],
                                  P["rwkv_rk"][l].reshape(-1)]), ((0, 3), (0, 0)))
    W["w2p"] = jnp.pad(P["rwkv_w2"][l], ((0, 160), (0, 0))).astype(BF16)
    W["a2p"] = jnp.pad(P["rwkv_a2"][l], ((96, 64), (0, 0))).astype(BF16)
    W["g2"] = P["rwkv_g2"][l].astype(BF16)
    W["ln"] = jnp.pad(jnp.stack([P["rwkv_lnx_w"][l], P["rwkv_lnx_b"][l]]), ((0, 6), (0, 0)))
    W["wb"] = P["w_branch"][l].astype(BF16)
    W["wout"] = P["w_out"][l].astype(BF16)
    F = P["ffn_conv_w"].shape[-1]
    Fp = -(-F // 512) * 512
    padc = lambda a: jnp.pad(a, ((0, 0), (0, Fp - F)))
    w_up = P["ffn_w_up"][l]
    W["wu"] = padc(w_up[:, :F]).astype(BF16)
    W["wg"] = padc(w_up[:, F:]).astype(BF16)
    W["cw"] = jnp.pad(padc(P["ffn_conv_w"][l]), ((0, 5), (0, 0)))
    W["cb"] = padc(P["ffn_conv_b"][l][None])
    W["wd"] = jnp.pad(P["ffn_w_down"][l], ((0, Fp - F), (0, 0))).astype(BF16)
    W["F"], W["Fp"] = F, Fp
    return W


def _rw_to_mine(a):
    pad = jnp.zeros(a.shape[:-1] + (64,), a.dtype)
    return jnp.concatenate([a[..., :3072], a[..., 3264:3520], a[..., 3072:3264], pad], axis=-1)


def _rw_from_mine(a):
    return jnp.concatenate([a[..., :3072], a[..., 3328:3520], a[..., 3072:3328]], axis=-1)


def _state_to_pairs(s):
    S, H = s.shape[:2]
    sp = s.reshape(S, H // 2, 2, HEAD_DIM, HEAD_DIM)
    z = jnp.zeros_like(sp[:, :, 0])
    top = jnp.concatenate([sp[:, :, 0], z], axis=-1)
    bot = jnp.concatenate([z, sp[:, :, 1]], axis=-1)
    bd = jnp.concatenate([top, bot], axis=-2)
    return bd.transpose(0, 2, 1, 3).reshape(S, LANES, H // 2 * LANES)


def _state_from_pairs(bd, H):
    S = bd.shape[0]
    b4 = bd.reshape(S, LANES, H // 2, LANES).transpose(0, 2, 1, 3)
    return jnp.stack([b4[:, :, :HEAD_DIM, :HEAD_DIM], b4[:, :, HEAD_DIM:, HEAD_DIM:]], axis=2).reshape(
        S, H, HEAD_DIM, HEAD_DIM)


def _layer(x, W, tb, *, prompt, n_seq, seq_rows, n_new, past, dec):
    blk_a = rms_matmul(x, W["norm_mix"], W["wA"])
    blk_b = rms_matmul(x, W["norm_mix"], W["wB"])
    blk_c = rms_matmul(x, W["norm_mix"], W["wC"])
    blk_d = rms_matmul(x, W["norm_mix"], W["wD"])
    nq, cmp_new, slc_new, win_new = nsa_prep(blk_b, W["gains"])
    M = x.shape[0]
    if prompt:
        o_a = sb_prompt(blk_a, n_seq, seq_rows, 16)
        fs = compress(cmp_new.reshape(M // LANES, 1, LANES, 512), 0, W["w1c"])
        kd, vd = cmp_finish(fs.reshape(n_seq, seq_rows // CMP_STRIDE, 1024), W["posb"], W["w2dup"], W["b2"],
                            W["gain1"])
        tq = tb["tq"]
        o_cmp, sel = cmp_attn(nq, kd, vd, tb["cmp_p"], tb["ov_p"], S=n_seq, tq=tq, n_q=seq_rows // tq,
                              n_sel=tb["n_sel_p"], pos0=0, transposed=True)
        o_slc = flash_prompt(nq, slc_new, tb["tiles"], sel, B=n_seq, T=seq_rows, mode="slc")
        o_win = flash_prompt(nq, win_new, tb["tiles"], None, B=n_seq, T=seq_rows, mode="win")
        head = jnp.zeros((n_seq, SUBLANES, RW_COLS), F32)
        rw = rwkv_prep(blk_c, head, W["mu"], W["vec"], W["w2p"], W["a2p"], W["g2"], seq_rows=seq_rows,
                       whole=False)
        C = RWKV_CHUNK
        s0 = jnp.zeros((n_seq, LANES, 1024), F32)
    else:
        lyr = dec["layer"]
        pt = dec["page_table"]
        o_a = sb_decode(blk_a, dec["cache_sb"], pt, lyr, n_new)
        fs_phys = compress(dec["cache_cmp"], lyr, W["w1c"])
        fs = gather_page_rows(fs_phys, pt)
        kd, vd = cmp_finish(fs, W["posb"], W["w2dup"], W["b2"], W["gain1"])
        o_cmp, sel = cmp_attn(nq, kd, vd, tb["cmp_d"], tb["ov_d"], S=n_seq, tq=ROW_GROUP, n_q=1,
                              n_sel=tb["n_sel_d"], pos0=past - (ROW_GROUP - n_new), transposed=False)
        o_slc = slc_decode(nq, slc_new, sel, tb["slc_d"], dec["cache_slc"], pt, lyr, n_new)
        o_win = win_decode(nq, win_new, tb["win_d"], dec["cache_win"], lyr, n_new)
        shift = _rw_to_mine(dec["shift"][:, lyr])
        head = jnp.zeros((n_seq, ROW_GROUP, RW_COLS), F32).at[:, ROW_GROUP - n_new - 1].set(shift)
        rw = rwkv_prep(blk_c, head.reshape(M, RW_COLS), W["mu"], W["vec"], W["w2p"], W["a2p"], W["g2"],
                       seq_rows=ROW_GROUP, whole=True, n_new=n_new)
        C = ROW_GROUP
        s0 = _state_to_pairs(dec["rwkv"][:, lyr].astype(F32))
    o_b = nsa_combine(o_cmp, o_slc, o_win, blk_c, gate_block=27, lane0=64)
    r, lw, k2, v, kk, b, bonus, g = rw
    r2, y2, mt, gt = rwkv_par(r, lw, k2, v, kk, b, C)
    o_c, s_fin = rwkv_seq(r2, y2, mt, gt, s0, bonus, g, W["ln"], n_seq=n_seq)
    mixed = merge_mix(o_a, o_b, o_c, W["wb"], blk_d)
    x1 = matmul_res(x, mixed, W["wout"])
    F, Fp = W["F"], W["Fp"]
    if prompt:
        x2, tail = conv_ffn(x1, W["norm_ffn"], W["wu"], W["wg"], W["cw"], W["cb"], W["wd"],
                            jnp.zeros((n_seq, SUBLANES, Fp), F32), seq_rows=seq_rows, whole_gate=False)
        conv = tail[:, SUBLANES - 2:, :F]
        last = blk_c.reshape(n_seq, seq_rows, RW_COLS)[:, -1]
    else:
        inj = jnp.zeros((n_seq, ROW_GROUP, Fp), F32).at[:, ROW_GROUP - n_new - 2:ROW_GROUP - n_new, :F].set(
            dec["conv"][:, lyr])
        real = (jnp.arange(M) % ROW_GROUP >= ROW_GROUP - n_new)[:, None]
        x2, gate = conv_ffn(jnp.where(real, x1, 0.0), W["norm_ffn"], W["wu"], W["wg"], W["cw"], W["cb"], W["wd"],
                            inj.reshape(M, Fp), seq_rows=ROW_GROUP, whole_gate=True)
        x2 = jnp.where(real, x2, 0.0)
        conv = gate.reshape(n_seq, ROW_GROUP, Fp)[:, ROW_GROUP - 2:, :F]
        last = blk_c.reshape(n_seq, ROW_GROUP, RW_COLS)[:, -1]
    new = {"sb": blk_a[:, 1024:], "cmp": cmp_new, "slc": slc_new, "win": win_new,
           "rwkv": _state_from_pairs(s_fin, 16), "shift": _rw_from_mine(last), "conv": conv}
    return x2, new


def kernel(x_prompt, x_sample, cache_sb_kv, cache_cmp_kv, cache_slc_kv, cache_win_kv, state_rwkv, state_rwkv_shift, state_conv, page_table, rel_bias, norm_mix_g, norm_ffn_g, w_in, nsa_qk_gain, cmp_pe, cmp_w1, cmp_b1, cmp_w2, cmp_b2, rwkv_mu, rwkv_w0, rwkv_w2, rwkv_a0, rwkv_a2, rwkv_g2, rwkv_kk, rwkv_ka, rwkv_rk, rwkv_lnx_w, rwkv_lnx_b, w_branch, w_out, ffn_w_up, ffn_conv_w, ffn_conv_b, ffn_w_down):
    P = dict(norm_mix_g=norm_mix_g, norm_ffn_g=norm_ffn_g, w_in=w_in, nsa_qk_gain=nsa_qk_gain, cmp_pe=cmp_pe,
             cmp_w1=cmp_w1, cmp_b1=cmp_b1, cmp_w2=cmp_w2, cmp_b2=cmp_b2, rwkv_mu=rwkv_mu, rwkv_w0=rwkv_w0,
             rwkv_w2=rwkv_w2, rwkv_a0=rwkv_a0, rwkv_a2=rwkv_a2, rwkv_g2=rwkv_g2, rwkv_kk=rwkv_kk,
             rwkv_ka=rwkv_ka, rwkv_rk=rwkv_rk, rwkv_lnx_w=rwkv_lnx_w, rwkv_lnx_b=rwkv_lnx_b,
             w_branch=w_branch, w_out=w_out, ffn_w_up=ffn_w_up, ffn_conv_w=ffn_conv_w,
             ffn_conv_b=ffn_conv_b, ffn_w_down=ffn_w_down)
    B, T, D = x_prompt.shape
    S, n_new, _ = x_sample.shape
    depth = w_in.shape[0]
    n_phys, _, page = cache_sb_kv.shape[:3]
    n_pages = page_table.shape[1]
    past = n_pages * page
    n_heads = rel_bias.shape[1]
    assert page == LANES and n_new + 2 <= ROW_GROUP and T % 128 == 0 and past % LANES == 0
    assert cache_win_kv.shape[2] == WINDOW <= past
    tb = _make_tables(rel_bias, T, past, n_new, n_heads)
    dec = {"page_table": page_table,
           "cache_sb": cache_sb_kv.reshape(n_phys, depth, page, -1),
           "cache_cmp": cache_cmp_kv.reshape(n_phys, depth, page, -1),
           "cache_slc": cache_slc_kv.reshape(n_phys, depth, page, -1),
           "cache_win": cache_win_kv.reshape(S, depth, WINDOW, -1),
           "rwkv": state_rwkv, "shift": state_rwkv_shift, "conv": state_conv}
    xp = x_prompt.reshape(B * T, D)
    xs = jnp.zeros((S, ROW_GROUP, D), F32).at[:, ROW_GROUP - n_new:].set(x_sample).reshape(S * ROW_GROUP, D)
    new_p, new_s = [], []
    for l in range(depth):
        W = _prep_layer(l, P)
        xp, st = _layer(xp, W, tb, prompt=True, n_seq=B, seq_rows=T, n_new=0, past=0, dec=None)
        new_p.append(st)
        dec["layer"] = l
        xs, st = _layer(xs, W, tb, prompt=False, n_seq=S, seq_rows=ROW_GROUP, n_new=n_new, past=past, dec=dec)
        new_s.append(st)

    n_win = min(WINDOW, T)

    def stack_p(name, shape):
        return jnp.stack([st[name].reshape((B, -1) + shape) for st in new_p], axis=1)

    def rows_s(a):
        return a.reshape(S, ROW_GROUP, -1)[:, ROW_GROUP - n_new:]

    def stack_s(name, shape):
        return jnp.stack([rows_s(st[name]).reshape((S, n_new) + shape) for st in new_s], axis=1)

    kvh = cache_cmp_kv.shape[4]
    sbh = cache_sb_kv.shape[4]
    p_win = jnp.stack([st["win"].reshape(B, T, 2, kvh, HEAD_DIM)[:, T - n_win:] for st in new_p], axis=1)
    s_win = jnp.stack([jnp.concatenate([cache_win_kv[:, l], rows_s(st["win"]).reshape(S, n_new, 2, kvh, HEAD_DIM)],
                                       axis=1)[:, n_new:] for l, st in enumerate(new_s)], axis=1)
    outs = (xp.reshape(B, T, D), rows_s(xs),
            stack_p("sb", (2, sbh, HEAD_DIM)), stack_p("cmp", (2, kvh, HEAD_DIM)),
            stack_p("slc", (2, kvh, HEAD_DIM)), p_win,
            jnp.stack([st["rwkv"] for st in new_p], axis=1), jnp.stack([st["shift"] for st in new_p], axis=1),
            jnp.stack([st["conv"] for st in new_p], axis=1),
            stack_s("sb", (2, sbh, HEAD_DIM)), stack_s("cmp", (2, kvh, HEAD_DIM)),
            stack_s("slc", (2, kvh, HEAD_DIM)), s_win,
            jnp.stack([st["rwkv"] for st in new_s], axis=1), jnp.stack([st["shift"] for st in new_s], axis=1),
            jnp.stack([st["conv"] for st in new_s], axis=1))
    return outs
```

```python
import functools
import math

import numpy as np
import jax
import jax.numpy as jnp
from jax import lax
from jax.experimental import pallas as pl
from jax.experimental.pallas import tpu as pltpu

F32 = jnp.float32
BF16 = jnp.bfloat16

HEAD_DIM = 64
LANES = 128
SUBLANES = 8
ROW_GROUP = 8
VMEM_LIMIT = 56 * 2 ** 20

RMS_EPS = 1e-6
GN_EPS = 64e-5
CMP_STRIDE = 16
CMP_BLOCK = 32
SEL_BLOCK = 64
N_SEL = 16
WINDOW = 512
N_BUCKETS = 32
MAX_DISTANCE = 1024
NEG = -1e30
M_FLOOR = -1e20
RWKV_CHUNK = 64

NN = (((1,), (0,)), ((), ()))
NT = (((1,), (1,)), ((), ()))
TN = (((0,), (0,)), ((), ()))


def _dot(a, b, dims=NN):
    return lax.dot_general(a, b, dims, preferred_element_type=F32)


def _split(x):
    hi = x.astype(BF16)
    lo = (x - hi.astype(F32)).astype(BF16)
    return hi, lo


def _dot_hl(a, b_exact, dims=NN):
    hi, lo = _split(a)
    return _dot(hi, b_exact, dims) + _dot(lo, b_exact, dims)


def _dot_hl_rhs(a_exact, b):
    hi, lo = _split(b)
    return _dot(a_exact, hi) + _dot(a_exact, lo)


def _dot3(a, b, dims=NN):
    ah, al = _split(a)
    bh, bl = _split(b)
    return _dot(ah, bh, dims) + (_dot(ah, bl, dims) + _dot(al, bh, dims))


def _iota(shape, dim):
    return lax.broadcasted_iota(jnp.int32, shape, dim)


def _pick(n, cands):
    for c in cands:
        if n % c == 0:
            return c
    raise ValueError(f"no tile for {n} in {cands}")


def _cparams(*sem):
    return pltpu.CompilerParams(dimension_semantics=sem, vmem_limit_bytes=VMEM_LIMIT)


def _seg_mat(n, seg, scale):
    r = lax.shift_right_logical(_iota((n, n), 0), int(math.log2(seg)))
    c = lax.shift_right_logical(_iota((n, n), 1), int(math.log2(seg)))
    return jnp.where(r == c, scale, 0.0).astype(BF16)


def _lo_mask():
    return _iota((1, LANES), 1) < HEAD_DIM


def _rms_rows(x, g):
    ms = jnp.mean(x * x, axis=-1, keepdims=True)
    return x * lax.rsqrt(ms + RMS_EPS) * g


def _rms_matmul_kernel(x_ref, g_ref, w_ref, o_ref, h_ref):
    @pl.when(pl.program_id(1) == 0)
    def _():
        h_ref[...] = _rms_rows(x_ref[...], g_ref[...]).astype(BF16)

    o_ref[...] = _dot(h_ref[...], w_ref[...])


def rms_matmul(x, g, w, tn=512):
    M, K = x.shape
    N = w.shape[1]
    tm = _pick(M, (1024, 512, 256, 128, 64, 32, 16, 8))
    tn = _pick(N, (tn, 256, 128))
    return pl.pallas_call(
        _rms_matmul_kernel,
        grid=(M // tm, N // tn),
        in_specs=[pl.BlockSpec((tm, K), lambda i, j: (i, 0)),
                  pl.BlockSpec((1, K), lambda i, j: (0, 0)),
                  pl.BlockSpec((K, tn), lambda i, j: (0, j))],
        out_specs=pl.BlockSpec((tm, tn), lambda i, j: (i, j)),
        out_shape=jax.ShapeDtypeStruct((M, N), F32),
        scratch_shapes=[pltpu.VMEM((tm, K), BF16)],
        compiler_params=_cparams("parallel", "arbitrary"),
        name="rms_matmul",
    )(x, g.reshape(1, K), w)


def _matmul_res_kernel(x_ref, a_ref, w_ref, o_ref):
    o_ref[...] = x_ref[...] + _dot(a_ref[...], w_ref[...])


def matmul_res(x, a, w, tn=512):
    M, N = x.shape
    K = a.shape[1]
    tm = _pick(M, (1024, 512, 256, 128, 64, 32, 16, 8))
    tn = _pick(N, (tn, 256, 128))
    return pl.pallas_call(
        _matmul_res_kernel,
        grid=(M // tm, N // tn),
        in_specs=[pl.BlockSpec((tm, tn), lambda i, j: (i, j)),
                  pl.BlockSpec((tm, K), lambda i, j: (i, 0)),
                  pl.BlockSpec((K, tn), lambda i, j: (0, j))],
        out_specs=pl.BlockSpec((tm, tn), lambda i, j: (i, j)),
        out_shape=jax.ShapeDtypeStruct((M, N), F32),
        compiler_params=_cparams("parallel", "arbitrary"),
        name="matmul_res",
    )(x, a, w)


def _merge_mix_kernel(oa_ref, ob_ref, oc_ref, wb_ref, m0_ref, m1_ref, m2_ref, o_ref):
    acc = jax.nn.sigmoid(m0_ref[...]) * _dot(oa_ref[...], wb_ref[0])
    acc = acc + jax.nn.sigmoid(m1_ref[...]) * _dot(ob_ref[...], wb_ref[1])
    acc = acc + jax.nn.sigmoid(m2_ref[...]) * _dot(oc_ref[...], wb_ref[2])
    o_ref[...] = acc.astype(o_ref.dtype)


def merge_mix(oa, ob, oc, wb, merge, tn=512):
    M, W = oa.shape
    D = wb.shape[2]
    tm = _pick(M, (512, 256, 128, 64, 32, 16, 8))
    nb = D // tn
    ospec = pl.BlockSpec((tm, W), lambda i, j: (i, 0))
    return pl.pallas_call(
        _merge_mix_kernel,
        grid=(M // tm, nb),
        in_specs=[ospec, ospec, ospec,
                  pl.BlockSpec((3, W, tn), lambda i, j: (0, 0, j)),
                  pl.BlockSpec((tm, tn), lambda i, j: (i, j)),
                  pl.BlockSpec((tm, tn), lambda i, j: (i, j + nb)),
                  pl.BlockSpec((tm, tn), lambda i, j: (i, j + 2 * nb))],
        out_specs=pl.BlockSpec((tm, tn), lambda i, j: (i, j)),
        out_shape=jax.ShapeDtypeStruct((M, D), BF16),
        compiler_params=_cparams("parallel", "arbitrary"),
        name="merge_mix",
    )(oa, ob, oc, wb, merge, merge, merge)


def _ffn_kernel(x_ref, g_ref, wu_ref, wg_ref, cw_ref, cb_ref, wd_ref, prev_ref,
                o_ref, st_ref, h_ref, acc_ref, gs_ref, carry_ref, *, tm, tps, whole_gate):
    i = pl.program_id(0)
    n = pl.program_id(1)

    @pl.when(n == 0)
    def _():
        h_ref[...] = _rms_rows(x_ref[...], g_ref[...]).astype(BF16)
        acc_ref[...] = jnp.zeros_like(acc_ref)

    h = h_ref[...]
    u = _dot(h, wu_ref[...])
    g = _dot(h, wg_ref[...])
    if whole_gate:
        g = g + prev_ref[...]
        head = jnp.zeros((SUBLANES, g.shape[1]), F32)
    else:
        head = jnp.where(i % tps == 0, prev_ref[0], carry_ref[n])
    gs_ref[pl.ds(0, SUBLANES), :] = head
    gs_ref[pl.ds(SUBLANES, tm), :] = g
    gm1 = gs_ref[pl.ds(SUBLANES - 1, tm), :]
    gm2 = gs_ref[pl.ds(SUBLANES - 2, tm), :]
    gc = cb_ref[...] + gm2 * cw_ref[0:1, :] + gm1 * cw_ref[1:2, :] + g * cw_ref[2:3, :]
    act = (gc * jax.nn.sigmoid(gc) * u).astype(BF16)
    acc_ref[...] += _dot(act, wd_ref[...])
    if whole_gate:
        st_ref[...] = g
    else:
        tail = gs_ref[pl.ds(tm, SUBLANES), :]
        carry_ref[n] = tail
        st_ref[0] = tail

    @pl.when(n == pl.num_programs(1) - 1)
    def _():
        o_ref[...] = x_ref[...] + acc_ref[...]


def conv_ffn(x, g, wu, wg, cw, cb, wd, prev, *, seq_rows, whole_gate, tn=512):
    M, D = x.shape
    Fp = wu.shape[1]
    nb = Fp // tn
    if whole_gate:
        tm, tps = M, 1
        prev_spec = pl.BlockSpec((tm, tn), lambda i, n: (0, n))
        st_spec = pl.BlockSpec((tm, tn), lambda i, n: (0, n))
        st_shape = jax.ShapeDtypeStruct((M, Fp), F32)
    else:
        tm = _pick(seq_rows, (512, 256, 128, 64))
        tps = seq_rows // tm
        prev_spec = pl.BlockSpec((1, SUBLANES, tn), lambda i, n: (i // tps, 0, n))
        st_spec = pl.BlockSpec((1, SUBLANES, tn), lambda i, n: (i, 0, n))
        st_shape = jax.ShapeDtypeStruct((M // tm, SUBLANES, Fp), F32)
    kern = functools.partial(_ffn_kernel, tm=tm, tps=tps, whole_gate=whole_gate)
    y, st = pl.pallas_call(
        kern,
        grid=(M // tm, nb),
        in_specs=[pl.BlockSpec((tm, D), lambda i, n: (i, 0)),
                  pl.BlockSpec((1, D), lambda i, n: (0, 0)),
                  pl.BlockSpec((D, tn), lambda i, n: (0, n)),
                  pl.BlockSpec((D, tn), lambda i, n: (0, n)),
                  pl.BlockSpec((SUBLANES, tn), lambda i, n: (0, n)),
                  pl.BlockSpec((1, tn), lambda i, n: (0, n)),
                  pl.BlockSpec((tn, D), lambda i, n: (n, 0)),
                  prev_spec],
        out_specs=[pl.BlockSpec((tm, D), lambda i, n: (i, 0)), st_spec],
        out_shape=[jax.ShapeDtypeStruct((M, D), F32), st_shape],
        scratch_shapes=[pltpu.VMEM((tm, D), BF16), pltpu.VMEM((tm, D), F32),
                        pltpu.VMEM((tm + SUBLANES, tn), F32), pltpu.VMEM((nb, SUBLANES, tn), F32)],
        compiler_params=_cparams("arbitrary", "arbitrary"),
        name="conv_ffn",
    )(x, g.reshape(1, D), wu, wg, cw, cb, wd, prev)
    return (y, st) if whole_gate else (y, st[tps - 1::tps])


def _sb_rhs():
    j = jnp.bitwise_and(_iota((2 * LANES, 2 * LANES), 0), LANES - 1)
    s = _iota((2 * LANES, 2 * LANES), 1)
    return jnp.where((j > s) | (s >= LANES), -1.0, 0.0).astype(BF16)


def _sb_sums(sp, nrhs):
    hi, lw = _split(sp)
    return _dot(jnp.concatenate([hi, lw], axis=1), nrhs)


LOG2E = math.log2(math.e)


def _softplus2(z):
    return jnp.maximum(z, 0.0) + jnp.log2(1.0 + jnp.exp2(-jnp.abs(z)))


def _sb_block(qm, kb, vb, before, c, nrhs):
    z = _dot(qm, kb, NT)
    sp = _softplus2(z)
    if before is not None:
        sp = jnp.where(before, sp, 0.0)
    cs2 = _sb_sums(sp, nrhs)
    w = jnp.exp2((z - sp) + (cs2[:, :LANES] + c))
    if before is not None:
        w = jnp.where(before, w, 0.0)
    return _dot(w.astype(BF16), vb), c + cs2[:, LANES:]


def _sb_pair_blocks(q, kbs, vbs, befores, c, nrhs, lo):
    def per_head(x):
        zero = jnp.zeros_like(x)
        return jnp.concatenate([jnp.where(lo, x, zero), jnp.where(lo, zero, x)], axis=0)

    kbd = [per_head(x) for x in kbs]
    vbd = [per_head(x) for x in vbs]
    z = [_dot(q, x, NT) for x in kbd]
    sp = [_softplus2(x) for x in z]
    sp = [x if m is None else jnp.where(m, x, 0.0) for x, m in zip(sp, befores)]
    cs = [[_sb_sums(x[:, h * LANES:(h + 1) * LANES], nrhs) for h in range(2)] for x in sp]
    suffix = [jnp.concatenate([y[0][:, :LANES], y[1][:, :LANES]], axis=1) for y in cs]
    total = [jnp.concatenate([y[0][:, LANES:], y[1][:, LANES:]], axis=1) for y in cs]
    carry = [c]
    for t in total:
        carry.append(carry[-1] + t)
    w = [jnp.exp2((a - b) + (s + cc)) for a, b, s, cc in zip(z, sp, suffix, carry)]
    w = [x if m is None else jnp.where(m, x, 0.0) for x, m in zip(w, befores)]
    pv = [_dot(x.astype(BF16), y) for x, y in zip(w, vbd)]
    acc = pv[0]
    for x in pv[1:]:
        acc = acc + x
    return acc, carry[-1]


def _sb_prompt_kernel(qi_ref, kj_ref, q_ref, k_ref, v_ref, o_ref, acc_ref, c_ref, *, tq, tk):
    s = pl.program_id(2)
    qi = qi_ref[s]
    kj = kj_ref[s]
    lo = _lo_mask()
    first = (kj + 1) * tk == (qi + 1) * tq
    overlap = (kj + 1) * tk > qi * tq

    def tile(masked):
        q = (q_ref[...] * LOG2E).astype(BF16)
        nrhs = _sb_rhs()
        c = jnp.where(first, 0.0, c_ref[...])
        subs = list(reversed(range(tk // LANES)))
        kbs = [k_ref[pl.ds(sub * LANES, LANES), :].astype(BF16) for sub in subs]
        vbs = [v_ref[pl.ds(sub * LANES, LANES), :].astype(BF16) for sub in subs]
        befores = [None] * len(subs)
        if masked:
            lane_key = jnp.bitwise_and(_iota((tq, 2 * LANES), 1), LANES - 1)
            row = qi * tq + _iota((tq, 2 * LANES), 0)
            befores = [kj * tk + sub * LANES + lane_key < row for sub in subs]
        acc, c = _sb_pair_blocks(q, kbs, vbs, befores, c, nrhs, lo)
        c_ref[...] = c
        acc_ref[...] = jnp.where(first, 0.0, acc_ref[...]) + acc

    @pl.when(overlap)
    def _():
        tile(True)

    @pl.when(jnp.logical_not(overlap))
    def _():
        tile(False)

    @pl.when(kj == 0)
    def _():
        o_ref[...] = acc_ref[...].astype(o_ref.dtype)


def _tri_steps(n, lookback=None):
    qs, ks, first, last = [], [], [], []
    for q in range(n):
        k_lo = 0 if lookback is None else max(0, q - lookback)
        for k in range(q, k_lo - 1, -1):
            qs.append(q)
            ks.append(k)
            first.append(int(k == q))
            last.append(int(k == k_lo))
    return tuple(jnp.asarray(np.array(a, np.int32)) for a in (qs, ks, first, last))


def _sb_steps(nq, ratio):
    qs, ks = [], []
    for q in range(nq):
        for k in range((q + 1) * ratio - 1, -1, -1):
            qs.append(q)
            ks.append(k)
    return jnp.asarray(np.array(qs, np.int32)), jnp.asarray(np.array(ks, np.int32))


def sb_prompt(qkv, B, T, n_heads, tq=None, tk=None):
    M = B * T
    tk = tk or _pick(T, (512, 256, 128))
    tq = tq or _pick(T, (512, 256, 128))
    assert tq % tk == 0
    nq, nk = T // tq, T // tk
    npair = n_heads // 2
    qs, ks = _sb_steps(nq, tq // tk)
    kern = functools.partial(_sb_prompt_kernel, tq=tq, tk=tk)
    return pl.pallas_call(
        kern,
        grid_spec=pltpu.PrefetchScalarGridSpec(
            num_scalar_prefetch=2,
            grid=(B, npair, int(qs.shape[0])),
            in_specs=[pl.BlockSpec((tq, LANES), lambda b, p, s, qi, kj: (b * nq + qi[s], p)),
                      pl.BlockSpec((tk, LANES), lambda b, p, s, qi, kj: (b * nk + kj[s], npair + p)),
                      pl.BlockSpec((tk, LANES), lambda b, p, s, qi, kj: (b * nk + kj[s], 2 * npair + p))],
            out_specs=pl.BlockSpec((tq, LANES), lambda b, p, s, qi, kj: (b * nq + qi[s], p)),
            scratch_shapes=[pltpu.VMEM((tq, LANES), F32), pltpu.VMEM((tq, 2 * LANES), F32)]),
        out_shape=jax.ShapeDtypeStruct((M, n_heads * HEAD_DIM), BF16),
        compiler_params=_cparams("parallel", "parallel", "arbitrary"),
        name="sb_prompt",
    )(qs, ks, qkv, qkv, qkv)


def _block_diag_rows(q8, n_heads):
    rows = n_heads * ROW_GROUP
    width = q8.shape[1]
    tiled = jnp.concatenate([q8] * n_heads, axis=0)
    rh = lax.shift_right_logical(_iota((rows, width), 0), 3)
    ch = lax.shift_right_logical(_iota((rows, width), 1), 6)
    return jnp.where(rh == ch, tiled, 0.0), rh == ch


def _sb_blocks(qm, kbs, vbs, c, nrhs):
    z = [_dot(qm, x, NT) for x in kbs]
    sp = [_softplus2(x) for x in z]
    cs = [_sb_sums(x, nrhs) for x in sp]
    carry = [c]
    for x in cs:
        carry.append(carry[-1] + x[:, LANES:])
    w = [jnp.exp2((a - b) + (x[:, :LANES] + cc)).astype(BF16) for a, b, x, cc in zip(z, sp, cs, carry)]
    pv = [_dot(x, y) for x, y in zip(w, vbs)]
    acc = pv[0]
    for x in pv[1:]:
        acc = acc + x
    return acc, carry[-1]


def _sb_decode_kernel(pt_ref, q_ref, kn_ref, vn_ref, *rest, n_heads, n_steps, n_new, ppb):
    page_refs = rest[:ppb]
    o_ref, acc_ref, c_ref, qbd_ref = rest[ppb:]
    p = pl.program_id(1)
    rows = n_heads * ROW_GROUP
    width = n_heads * HEAD_DIM
    rhs = _sb_rhs()

    @pl.when(p == 0)
    def _():
        qbd, _ = _block_diag_rows(q_ref[...] * LOG2E, n_heads)
        qbd = qbd.astype(BF16)
        qbd_ref[...] = qbd
        pad = jnp.zeros((LANES - ROW_GROUP, width), F32)
        kb = jnp.concatenate([kn_ref[...], pad], axis=0).astype(BF16)
        vb = jnp.concatenate([vn_ref[...], pad], axis=0).astype(BF16)
        r_in = jnp.bitwise_and(_iota((rows, LANES), 0), ROW_GROUP - 1)
        col = _iota((rows, LANES), 1)
        before = (col >= ROW_GROUP - n_new) & (col < ROW_GROUP) & (col < r_in)
        pv, c_new = _sb_block(qbd, kb, vb, before, jnp.zeros((rows, LANES), F32), rhs)
        acc_ref[...] = pv
        c_ref[...] = c_new

    @pl.when(p > 0)
    def _():
        kbs = [ref[:, pl.ds(0, width)] for ref in page_refs]
        vbs = [ref[:, pl.ds(width, width)] for ref in page_refs]
        pv, c_new = _sb_blocks(qbd_ref[...], kbs, vbs, c_ref[...], rhs)
        acc_ref[...] += pv
        c_ref[...] = c_new

    @pl.when(p == n_steps - 1)
    def _():
        own = (lax.shift_right_logical(_iota((rows, width), 0), 3)
               == lax.shift_right_logical(_iota((rows, width), 1), 6))
        m = jnp.where(own, acc_ref[...], 0.0).reshape(n_heads, ROW_GROUP, width)
        o_ref[...] = jnp.sum(m, axis=0).astype(o_ref.dtype)


def sb_decode(qkv, cache, page_table, layer, n_new):
    S, n_pages = page_table.shape
    page = cache.shape[2]
    W = cache.shape[3] // 2
    n_heads = W // HEAD_DIM
    assert page == LANES
    rows = n_heads * ROW_GROUP
    pt = page_table.reshape(-1).astype(jnp.int32)
    ppb = _pick(n_pages, (4, 2, 1))
    n_steps = 1 + n_pages // ppb

    def page_map(i):
        def index(s, p, pt_ref):
            logical = n_pages - 1 - (jnp.maximum(p, 1) - 1) * ppb - i
            return (pt_ref[s * n_pages + logical], layer, 0, 0)
        return index

    kern = functools.partial(_sb_decode_kernel, n_heads=n_heads, n_steps=n_steps, n_new=n_new, ppb=ppb)
    return pl.pallas_call(
        kern,
        grid_spec=pltpu.PrefetchScalarGridSpec(
            num_scalar_prefetch=1,
            grid=(S, n_steps),
            in_specs=[pl.BlockSpec((ROW_GROUP, W), lambda s, p, pt_ref: (s, 0)),
                      pl.BlockSpec((ROW_GROUP, W), lambda s, p, pt_ref: (s, 1)),
                      pl.BlockSpec((ROW_GROUP, W), lambda s, p, pt_ref: (s, 2))]
            + [pl.BlockSpec((None, None, page, 2 * W), page_map(i)) for i in range(ppb)],
            out_specs=pl.BlockSpec((ROW_GROUP, W), lambda s, p, pt_ref: (s, 0)),
            scratch_shapes=[pltpu.VMEM((rows, W), F32), pltpu.VMEM((rows, LANES), F32),
                            pltpu.VMEM((rows, W), BF16)]),
        out_shape=jax.ShapeDtypeStruct((S * ROW_GROUP, W), BF16),
        compiler_params=_cparams("parallel", "arbitrary"),
        name="sb_decode",
    )(pt, qkv, qkv, qkv, *([cache] * ppb))


def _nsa_prep_kernel(x_ref, gain_ref, nq_ref, cmp_ref, slc_ref, win_ref):
    seg = _seg_mat(LANES, HEAD_DIM, 1.0 / HEAD_DIM)

    def norm(col, gain_row):
        blk = x_ref[:, pl.ds(col, LANES)]
        ms = _dot_hl(blk * blk, seg)
        return blk * lax.rsqrt(ms + RMS_EPS) * gain_ref[gain_row:gain_row + 1, :]

    for j in range(8):
        nq_ref[:, pl.ds(j * LANES, LANES)] = (norm(j * LANES, 0) * HEAD_DIM ** -0.5).astype(BF16)
    cmp_ref[...] = x_ref[:, pl.ds(1024, 512)]
    for j in range(2):
        slc_ref[:, pl.ds(j * LANES, LANES)] = norm(1536 + j * LANES, 2)
        win_ref[:, pl.ds(j * LANES, LANES)] = norm(2048 + j * LANES, 3)
    slc_ref[:, pl.ds(256, 256)] = x_ref[:, pl.ds(1792, 256)]
    win_ref[:, pl.ds(256, 256)] = x_ref[:, pl.ds(2304, 256)]


def nsa_prep(blk_b, gains):
    M = blk_b.shape[0]
    tm = _pick(M, (512, 256, 128, 64, 32, 16, 8))
    return pl.pallas_call(
        _nsa_prep_kernel,
        grid=(M // tm,),
        in_specs=[pl.BlockSpec((tm, 2560), lambda i: (i, 0)),
                  pl.BlockSpec((SUBLANES, LANES), lambda i: (0, 0))],
        out_specs=[pl.BlockSpec((tm, 1024), lambda i: (i, 0)),
                   pl.BlockSpec((tm, 512), lambda i: (i, 0)),
                   pl.BlockSpec((tm, 512), lambda i: (i, 0)),
                   pl.BlockSpec((tm, 512), lambda i: (i, 0))],
        out_shape=[jax.ShapeDtypeStruct((M, 1024), BF16)] + [jax.ShapeDtypeStruct((M, 512), F32)] * 3,
        compiler_params=_cparams("parallel"),
        name="nsa_prep",
    )(blk_b, gains)


def _compress_kernel(x0_ref, x1_ref, x2_ref, x3_ref, w_ref, o_ref, *, G):
    for p, x_ref in enumerate((x0_ref, x1_ref, x2_ref, x3_ref)):
        acc = jnp.zeros((G * SUBLANES, 2 * LANES), F32)
        for s in range(CMP_STRIDE):
            xs = x_ref[:, pl.ds(s, SUBLANES, stride=CMP_STRIDE), :].reshape(G * SUBLANES, LANES).astype(BF16)
            acc = acc + _dot(xs, w_ref[p // 2, s])
        o_ref[:, pl.ds(p * 2 * LANES, 2 * LANES)] = acc


def compress(pages, layer, w1c):
    n_pages = pages.shape[0]
    G = _pick(n_pages, (16, 8, 5, 4, 3, 2, 1))
    kern = functools.partial(_compress_kernel, G=G)
    return pl.pallas_call(
        kern,
        grid=(n_pages // G,),
        in_specs=[pl.BlockSpec((G, None, LANES, LANES), functools.partial(lambda i, p: (i, layer, 0, p), p=p))
                  for p in range(4)]
        + [pl.BlockSpec((2, CMP_STRIDE, LANES, 2 * LANES), lambda i: (0, 0, 0, 0))],
        out_specs=pl.BlockSpec((G * SUBLANES, 1024), lambda i: (i, 0)),
        out_shape=jax.ShapeDtypeStruct((n_pages * SUBLANES, 1024), F32),
        compiler_params=_cparams("parallel"),
        name="nsa_compress",
    )(pages, pages, pages, pages, w1c)


def _gather_rows_kernel(pt_ref, src_ref, o_ref, sem, *, n_pages):
    s = pl.program_id(0)

    def copy(p):
        return pltpu.make_async_copy(src_ref.at[pl.ds(pt_ref[s * n_pages + p] * SUBLANES, SUBLANES)],
                                     o_ref.at[0, pl.ds(p * SUBLANES, SUBLANES)], sem)

    def start(p, carry):
        copy(p).start()
        return carry

    def wait(p, carry):
        copy(p).wait()
        return carry

    lax.fori_loop(0, n_pages, start, 0)
    lax.fori_loop(0, n_pages, wait, 0)


def gather_page_rows(src, page_table):
    S, n_pages = page_table.shape
    W = src.shape[1]
    kern = functools.partial(_gather_rows_kernel, n_pages=n_pages)
    return pl.pallas_call(
        kern,
        grid_spec=pltpu.PrefetchScalarGridSpec(
            num_scalar_prefetch=1,
            grid=(S,),
            in_specs=[pl.BlockSpec(memory_space=pl.ANY)],
            out_specs=pl.BlockSpec((1, n_pages * SUBLANES, W), lambda s, pt_ref: (s, 0, 0)),
            scratch_shapes=[pltpu.SemaphoreType.DMA(())]),
        out_shape=jax.ShapeDtypeStruct((S, n_pages * SUBLANES, W), F32),
        compiler_params=_cparams("arbitrary"),
        name="gather_page_rows",
    )(page_table.reshape(-1).astype(jnp.int32), src)


def _cmp_finish_kernel(fs_ref, posb_ref, w2_ref, b2_ref, gain_ref, kd_ref, vd_ref, *, nch):
    for p in range(4):
        c, pp = p // 2, p % 2
        first = fs_ref[:, pl.ds(p * 2 * LANES, LANES)]
        second = pltpu.roll(fs_ref[:, pl.ds(p * 2 * LANES + LANES, LANES)], nch - 1, 0)
        hid = jax.nn.gelu(first + second + posb_ref[c:c + 1, :]).astype(BF16)
        for e in range(2):
            out = _dot(hid, w2_ref[c, e]) + b2_ref[c:c + 1, :]
            if c == 0:
                ms = jnp.mean(out * out, axis=-1, keepdims=True)
                kd_ref[pp * 2 + e] = (out * lax.rsqrt(ms + RMS_EPS) * gain_ref[...]).astype(BF16)
            else:
                vd_ref[pp * 2 + e] = out.astype(BF16)


def cmp_finish(fs, posb, w2dup, b2, gain):
    S, nch, _ = fs.shape
    kern = functools.partial(_cmp_finish_kernel, nch=nch)
    small = pl.BlockSpec((SUBLANES, LANES), lambda s: (0, 0))
    return pl.pallas_call(
        kern,
        grid=(S,),
        in_specs=[pl.BlockSpec((None, nch, 1024), lambda s: (s, 0, 0)), small,
                  pl.BlockSpec((2, 2, LANES, LANES), lambda s: (0, 0, 0, 0)), small,
                  pl.BlockSpec((1, LANES), lambda s: (0, 0))],
        out_specs=[pl.BlockSpec((None, 4, nch, LANES), lambda s: (s, 0, 0, 0))] * 2,
        out_shape=[jax.ShapeDtypeStruct((S, 4, nch, LANES), BF16)] * 2,
        compiler_params=_cparams("parallel"),
        name="cmp_finish",
    )(fs, posb, w2dup, b2, gain)


def _head_queries(q, lo):
    out = []
    for g in range(4):
        blk = q[:, (g // 2) * LANES:(g // 2 + 1) * LANES]
        keep = lo if g % 2 == 0 else jnp.logical_not(lo)
        out.append(jnp.where(keep, blk, jnp.zeros_like(blk)))
    return out


def _pair_out(o, lo):
    return jnp.concatenate([jnp.where(lo, o[0], o[1]), jnp.where(lo, o[2], o[3])], axis=1)


def _topk_mask(score, n_sel, k_eff, axis):
    j = _iota(score.shape, axis)
    rank = jnp.zeros(score.shape, jnp.int32)
    for i in range(n_sel):
        si = score[i:i + 1, :] if axis == 0 else score[:, i:i + 1]
        ahead = (si > score) | ((si == score) & (j > i))
        rank = rank + jnp.where(ahead, 1, 0)
    return rank < k_eff


def _cmp_attn_kernel(q_ref, kd_ref, vd_ref, bias_ref, ov_ref, o_ref, sel_ref, *,
                     tq, ncp, n_sel, k_eff, pos0, transposed):
    t0 = pos0 + pl.program_id(2) * tq
    t = t0 + _iota((tq, ncp), 0)
    c = _iota((tq, ncp), 1)
    valid = t - (c * CMP_STRIDE + CMP_BLOCK - 1) >= 0
    lo = _lo_mask()
    kd = kd_ref[...]
    vd = vd_ref[...]
    psum = jnp.zeros((tq, ncp), F32)
    outs = []
    for g, qm in enumerate(_head_queries(q_ref[...], lo)):
        sc = jnp.where(valid, _dot(qm, kd, NT) + bias_ref[g], NEG)
        m = jnp.max(sc, axis=1, keepdims=True)
        e = jnp.where(valid, jnp.exp(sc - m), 0.0)
        den = jnp.sum(e, axis=1, keepdims=True)
        p = e / jnp.where(den > 0, den, 1.0)
        outs.append(_dot(p.astype(BF16), vd))
        psum = psum + p
    o_ref[...] = _pair_out(outs, lo)
    hi, lw = _split(psum)
    if transposed:
        imp = _dot(ov_ref[...], hi, NT) + _dot(ov_ref[...], lw, NT)
        j = _iota(imp.shape, 0)
        tt = t0 + _iota(imp.shape, 1)
    else:
        imp = _dot(hi, ov_ref[...]) + _dot(lw, ov_ref[...])
        j = _iota(imp.shape, 1)
        tt = t0 + _iota(imp.shape, 0)
    causal = j * SEL_BLOCK <= tt
    cur = lax.shift_right_logical(tt, 6)
    forced = causal & ((j == 0) | (j == cur) | (j == cur - 1))
    score = jnp.where(forced, -NEG, jnp.where(causal, imp, NEG))
    score = jnp.where(j < n_sel, score, 2 * NEG)
    sel = _topk_mask(score, n_sel, k_eff, 0 if transposed else 1)
    sel_ref[...] = jnp.where(sel, 1.0, 0.0)


def cmp_attn(nq, kd, vd, bias, ov, *, S, tq, n_q, n_sel, pos0, transposed):
    ncp = kd.shape[2]
    nsp = ov.shape[0] if transposed else ov.shape[1]
    k_eff = min(N_SEL, n_sel)
    rows = S * n_q * tq
    if transposed:
        sel_spec = pl.BlockSpec((None, None, nsp, tq), lambda b, k, i: (b, k, 0, i))
        sel_shape = jax.ShapeDtypeStruct((S, 4, nsp, n_q * tq), F32)
    else:
        sel_spec = pl.BlockSpec((None, None, tq, nsp), lambda b, k, i: (b, k, i, 0))
        sel_shape = jax.ShapeDtypeStruct((S, 4, n_q * tq, nsp), F32)
    kern = functools.partial(_cmp_attn_kernel, tq=tq, ncp=ncp, n_sel=n_sel, k_eff=k_eff, pos0=pos0,
                             transposed=transposed)
    return pl.pallas_call(
        kern,
        grid=(S, 4, n_q),
        in_specs=[pl.BlockSpec((tq, 2 * LANES), lambda b, k, i: (b * n_q + i, k)),
                  pl.BlockSpec((None, None, ncp, LANES), lambda b, k, i: (b, k, 0, 0)),
                  pl.BlockSpec((None, None, ncp, LANES), lambda b, k, i: (b, k, 0, 0)),
                  pl.BlockSpec((4, tq, ncp), lambda b, k, i: (k, i, 0)),
                  pl.BlockSpec(ov.shape, lambda b, k, i: (0, 0))],
        out_specs=[pl.BlockSpec((tq, 2 * LANES), lambda b, k, i: (b * n_q + i, k)), sel_spec],
        out_shape=[jax.ShapeDtypeStruct((rows, 1024), F32), sel_shape],
        compiler_params=_cparams("parallel", "parallel", "arbitrary"),
        name="cmp_attn",
    )(nq, kd, vd, bias, ov)


def _dup_half(x, odd):
    lo = _lo_mask()
    take_x = jnp.logical_xor(lo, odd)
    return jnp.where(take_x, x, pltpu.roll(x, HEAD_DIM, 1))


def _flash_kernel(qi_ref, kj_ref, dl_ref, first_ref, last_ref, q_ref, k_ref, v_ref, bias_ref, *rest,
                  tq, tk, mode):
    if mode == "slc":
        sel_ref, o_ref, m_ref, l_ref, acc_ref = rest
    else:
        o_ref, m_ref, l_ref, acc_ref = rest
    kvh = pl.program_id(1)
    s = pl.program_id(2)
    qi = qi_ref[s]
    kj = kj_ref[s]

    @pl.when(first_ref[s] == 1)
    def _():
        m_ref[...] = jnp.full_like(m_ref, M_FLOOR)
        l_ref[...] = jnp.zeros_like(l_ref)
        acc_ref[...] = jnp.zeros_like(acc_ref)

    lo = _lo_mask()
    odd = (kvh % 2) == 1
    kd = _dup_half(k_ref[...], odd).astype(BF16)
    vd = _dup_half(v_ref[...], odd).astype(BF16)
    ones = jnp.ones((tk, LANES), BF16)
    if mode == "slc":
        nsp = sel_ref.shape[0]
        blk = kj * (tk // SEL_BLOCK) + lax.shift_right_logical(_iota((nsp, tk), 1), 6)
        expand = jnp.where(_iota((nsp, tk), 0) == blk, 1.0, 0.0).astype(BF16)
    rc = min(tq, LANES)
    units, valids = [], []
    for r0 in range(0, tq, rc):
        rows = pl.ds(r0, rc)
        d = (qi * tq + r0 + _iota((rc, tk), 0)) - (kj * tk + _iota((rc, tk), 1))
        if mode == "slc":
            chosen = _dot(sel_ref[:, rows].astype(BF16), expand, TN)
            valids.append((chosen > 0.5) & (d >= 0))
        else:
            valids.append((d >= 0) & (d < WINDOW))
        for g, qm in enumerate(_head_queries(q_ref[rows, :], lo)):
            units.append((g, rows, qm, len(valids) - 1))
    sc = [_dot(qm, kd, NT) + jnp.where(valids[v], bias_ref[g, rows, :], NEG) for g, rows, qm, v in units]
    m_old = [m_ref[g, rows, :] for g, rows, _, _ in units]
    m_new = [jnp.maximum(mo, jnp.max(x, axis=1, keepdims=True)) for mo, x in zip(m_old, sc)]
    p = [jnp.exp(x - mn).astype(BF16) for x, mn in zip(sc, m_new)]
    alpha = [jnp.exp(mo - mn) for mo, mn in zip(m_old, m_new)]
    rowsum = [_dot(x, ones) for x in p]
    pv = [_dot(x, vd) for x in p]
    for (g, rows, _, _), a, rs, o, mn in zip(units, alpha, rowsum, pv, m_new):
        l_ref[g, rows, :] = a * l_ref[g, rows, :] + rs
        acc_ref[g, rows, :] = a * acc_ref[g, rows, :] + o
        m_ref[g, rows, :] = mn

    @pl.when(last_ref[s] == 1)
    def _():
        outs = []
        for g in range(4):
            den = l_ref[g]
            outs.append(acc_ref[g] / jnp.where(den > 0, den, 1.0))
        o_ref[...] = _pair_out(outs, lo)


def flash_prompt(nq, kv, bias_tiles, sel, *, B, T, mode):
    tq = tk = bias_tiles.shape[2]
    nq_t = T // tq
    nd = bias_tiles.shape[1]
    lookback = None if mode == "slc" else -(-(WINDOW - 1) // tk)
    qs, ks, first, last = _tri_steps(nq_t, lookback)
    dl = jnp.minimum(qs - ks, nd - 1)
    n_steps = int(qs.shape[0])
    in_specs = [pl.BlockSpec((tq, 2 * LANES), lambda b, k, s, qi, kj, dl, f, l: (b * nq_t + qi[s], k)),
                pl.BlockSpec((tk, LANES), lambda b, k, s, qi, kj, dl, f, l: (b * nq_t + kj[s], k // 2)),
                pl.BlockSpec((tk, LANES), lambda b, k, s, qi, kj, dl, f, l: (b * nq_t + kj[s], 2 + k // 2)),
                pl.BlockSpec((4, None, tq, tk), lambda b, k, s, qi, kj, dl, f, l: (k, dl[s], 0, 0))]
    args = [nq, kv, kv, bias_tiles]
    if mode == "slc":
        nsp = sel.shape[2]
        in_specs.append(pl.BlockSpec((None, None, nsp, tq), lambda b, k, s, qi, kj, dl, f, l: (b, k, 0, qi[s])))
        args.append(sel)
    kern = functools.partial(_flash_kernel, tq=tq, tk=tk, mode=mode)
    return pl.pallas_call(
        kern,
        grid_spec=pltpu.PrefetchScalarGridSpec(
            num_scalar_prefetch=5,
            grid=(B, 4, n_steps),
            in_specs=in_specs,
            out_specs=pl.BlockSpec((tq, 2 * LANES), lambda b, k, s, qi, kj, dl, f, l: (b * nq_t + qi[s], k)),
            scratch_shapes=[pltpu.VMEM((4, tq, 1), F32), pltpu.VMEM((4, tq, LANES), F32),
                            pltpu.VMEM((4, tq, LANES), F32)]),
        out_shape=jax.ShapeDtypeStruct((B * T, 1024), F32),
        compiler_params=_cparams("parallel", "parallel", "arbitrary"),
        name="nsa_" + mode,
    )(qs, ks, dl, first, last, *args)


def _kv_queries(q8, n_heads, n_kv):
    qbd, _ = _block_diag_rows(q8, n_heads)
    wq, wk = n_heads * HEAD_DIM, n_kv * HEAD_DIM
    gshift = int(math.log2(n_heads // n_kv)) + 6
    r = _iota((wq, wk), 0)
    c = _iota((wq, wk), 1)
    fold = jnp.where((lax.shift_right_logical(r, gshift) == lax.shift_right_logical(c, 6))
                     & (jnp.bitwise_and(r, 63) == jnp.bitwise_and(c, 63)), 1.0, 0.0).astype(BF16)
    return _dot(qbd.astype(BF16), fold).astype(BF16)


def _kv_outputs(o, n_heads, n_kv):
    rows = n_heads * ROW_GROUP
    wq, wk = n_heads * HEAD_DIM, n_kv * HEAD_DIM
    gshift = int(math.log2(n_heads // n_kv))
    own_kv = lax.shift_right_logical(_iota((rows, wk), 0), 3 + gshift) == lax.shift_right_logical(_iota((rows, wk), 1), 6)
    r = _iota((wk, wq), 0)
    c = _iota((wk, wq), 1)
    unfold = jnp.where((lax.shift_right_logical(c, gshift + 6) == lax.shift_right_logical(r, 6))
                       & (jnp.bitwise_and(r, 63) == jnp.bitwise_and(c, 63)), 1.0, 0.0).astype(BF16)
    wide = _dot_hl(jnp.where(own_kv, o, 0.0), unfold)
    own = lax.shift_right_logical(_iota((rows, wq), 0), 3) == lax.shift_right_logical(_iota((rows, wq), 1), 6)
    return jnp.sum(jnp.where(own, wide, 0.0).reshape(n_heads, ROW_GROUP, wq), axis=0)


def _softmax_step(sc, vb, m_ref, l_ref, acc_ref):
    m_old = m_ref[...]
    m_new = jnp.maximum(m_old, jnp.max(sc, axis=1, keepdims=True))
    p = jnp.exp(sc - m_new).astype(BF16)
    alpha = jnp.exp(m_old - m_new)
    l_ref[...] = alpha * l_ref[...] + _dot(p, jnp.ones((sc.shape[1], LANES), BF16))
    acc_ref[...] = alpha * acc_ref[...] + _dot(p, vb)
    m_ref[...] = m_new


def _new_rows(ref, cols):
    blk = ref[:, cols]
    return jnp.concatenate([blk, jnp.zeros((LANES - ROW_GROUP, blk.shape[1]), F32)], axis=0).astype(BF16)


def _slc_decode_kernel(pt_ref, q_ref, kvn_ref, sel_ref, bias_ref, *rest, n_heads, n_kv, n_pages, n_new, ppb):
    page_refs = rest[:ppb]
    o_ref, m_ref, l_ref, acc_ref, qbd_ref = rest[ppb:]
    p = pl.program_id(1)
    rows = n_heads * ROW_GROUP
    wk = n_kv * HEAD_DIM
    group = n_heads // n_kv
    nsp = sel_ref.shape[2]
    sel_rows = jnp.concatenate([sel_ref[k] for k in range(n_kv) for _ in range(group)], axis=0).astype(BF16)

    def chosen(first_blk, n_keys):
        blk = first_blk + lax.shift_right_logical(_iota((nsp, n_keys), 1), 6)
        expand = jnp.where(_iota((nsp, n_keys), 0) == blk, 1.0, 0.0).astype(BF16)
        return _dot(sel_rows, expand) > 0.5

    @pl.when(p == 0)
    def _():
        m_ref[...] = jnp.full_like(m_ref, M_FLOOR)
        l_ref[...] = jnp.zeros_like(l_ref)
        acc_ref[...] = jnp.zeros_like(acc_ref)
        qbd = _kv_queries(q_ref[...].astype(F32), n_heads, n_kv)
        qbd_ref[...] = qbd
        r_in = jnp.bitwise_and(_iota((rows, LANES), 0), ROW_GROUP - 1)
        col = _iota((rows, LANES), 1)
        kb = _new_rows(kvn_ref, slice(0, wk))
        vb = _new_rows(kvn_ref, slice(wk, 2 * wk))
        valid = (chosen(n_pages * (LANES // SEL_BLOCK), LANES)
                 & (col >= ROW_GROUP - n_new) & (col < ROW_GROUP) & (col <= r_in))
        sc = _dot(qbd, kb, NT) + jnp.where(valid, bias_ref[:, pl.ds(0, LANES)], NEG)
        _softmax_step(sc, vb, m_ref, l_ref, acc_ref)

    @pl.when(p > 0)
    def _():
        kb = jnp.concatenate([ref[:, pl.ds(0, wk)] for ref in page_refs], axis=0)
        vb = jnp.concatenate([ref[:, pl.ds(wk, wk)] for ref in page_refs], axis=0)
        base = n_pages - p * ppb
        valid = chosen(base * (LANES // SEL_BLOCK), ppb * LANES)
        sc = _dot(qbd_ref[...], kb, NT) + jnp.where(valid, bias_ref[...], NEG)
        _softmax_step(sc, vb, m_ref, l_ref, acc_ref)

    @pl.when(p == n_pages // ppb)
    def _():
        den = l_ref[:, pl.ds(0, 1)]
        o_ref[...] = _kv_outputs(acc_ref[...] / jnp.where(den > 0, den, 1.0), n_heads, n_kv)


def slc_decode(nq, kv_new, sel, bias, cache, page_table, layer, n_new):
    S, n_pages = page_table.shape
    n_kv = cache.shape[3] // (2 * HEAD_DIM)
    n_heads = nq.shape[1] // HEAD_DIM
    rows = n_heads * ROW_GROUP
    nsp = sel.shape[3]
    pt = page_table.reshape(-1).astype(jnp.int32)
    ppb = (bias.shape[1] - n_pages * LANES) // LANES
    assert n_pages % ppb == 0
    n_steps = 1 + n_pages // ppb

    def page_map(i):
        def index(s, p, pt_ref):
            logical = n_pages - jnp.maximum(p, 1) * ppb + i
            return (pt_ref[s * n_pages + logical], layer, 0, 0)
        return index

    kern = functools.partial(_slc_decode_kernel, n_heads=n_heads, n_kv=n_kv, n_pages=n_pages, n_new=n_new,
                             ppb=ppb)
    return pl.pallas_call(
        kern,
        grid_spec=pltpu.PrefetchScalarGridSpec(
            num_scalar_prefetch=1,
            grid=(S, n_steps),
            in_specs=[pl.BlockSpec((ROW_GROUP, nq.shape[1]), lambda s, p, pt_ref: (s, 0)),
                      pl.BlockSpec((ROW_GROUP, kv_new.shape[1]), lambda s, p, pt_ref: (s, 0)),
                      pl.BlockSpec((None, n_kv, ROW_GROUP, nsp), lambda s, p, pt_ref: (s, 0, 0, 0)),
                      pl.BlockSpec((rows, ppb * LANES),
                                   lambda s, p, pt_ref: (0, jnp.where(p == 0, n_pages // ppb, n_pages // ppb - p)))]
            + [pl.BlockSpec((None, None, LANES, cache.shape[3]), page_map(i)) for i in range(ppb)],
            out_specs=pl.BlockSpec((ROW_GROUP, nq.shape[1]), lambda s, p, pt_ref: (s, 0)),
            scratch_shapes=[pltpu.VMEM((rows, 1), F32), pltpu.VMEM((rows, LANES), F32),
                            pltpu.VMEM((rows, n_kv * HEAD_DIM), F32),
                            pltpu.VMEM((rows, n_kv * HEAD_DIM), BF16)]),
        out_shape=jax.ShapeDtypeStruct((S * ROW_GROUP, nq.shape[1]), F32),
        compiler_params=_cparams("parallel", "arbitrary"),
        name="slc_decode",
    )(pt, nq, kv_new, sel, bias, *([cache] * ppb))


def _win_decode_kernel(q_ref, kvn_ref, bias_ref, kv_ref, o_ref, m_ref, l_ref, acc_ref, *,
                       n_heads, n_kv, n_new, layer_rows):
    rows = n_heads * ROW_GROUP
    wk = n_kv * HEAD_DIM
    m_ref[...] = jnp.full_like(m_ref, M_FLOOR)
    l_ref[...] = jnp.zeros_like(l_ref)
    acc_ref[...] = jnp.zeros_like(acc_ref)
    qbd = _kv_queries(q_ref[...].astype(F32), n_heads, n_kv)
    i_q = jnp.bitwise_and(_iota((rows, layer_rows), 0), ROW_GROUP - 1) - (ROW_GROUP - n_new)
    j = _iota((rows, layer_rows), 1)
    dist = layer_rows + i_q - j
    valid = (dist >= 0) & (dist < WINDOW)
    kb = kv_ref[:, pl.ds(0, wk)].astype(BF16)
    vb = kv_ref[:, pl.ds(wk, wk)].astype(BF16)
    sc = _dot(qbd, kb, NT) + jnp.where(valid, bias_ref[:, pl.ds(0, layer_rows)], NEG)
    _softmax_step(sc, vb, m_ref, l_ref, acc_ref)
    r_in = jnp.bitwise_and(_iota((rows, LANES), 0), ROW_GROUP - 1)
    col = _iota((rows, LANES), 1)
    valid = (col >= ROW_GROUP - n_new) & (col < ROW_GROUP) & (col <= r_in)
    kb = _new_rows(kvn_ref, slice(0, wk))
    vb = _new_rows(kvn_ref, slice(wk, 2 * wk))
    sc = _dot(qbd, kb, NT) + jnp.where(valid, bias_ref[:, pl.ds(layer_rows, LANES)], NEG)
    _softmax_step(sc, vb, m_ref, l_ref, acc_ref)
    den = l_ref[:, pl.ds(0, 1)]
    o_ref[...] = _kv_outputs(acc_ref[...] / jnp.where(den > 0, den, 1.0), n_heads, n_kv)


def win_decode(nq, kv_new, bias, cache_win, layer, n_new):
    S, _, wc, width = cache_win.shape
    n_kv = width // (2 * HEAD_DIM)
    n_heads = nq.shape[1] // HEAD_DIM
    rows = n_heads * ROW_GROUP
    kern = functools.partial(_win_decode_kernel, n_heads=n_heads, n_kv=n_kv, n_new=n_new, layer_rows=wc)
    return pl.pallas_call(
        kern,
        grid=(S,),
        in_specs=[pl.BlockSpec((ROW_GROUP, nq.shape[1]), lambda s: (s, 0)),
                  pl.BlockSpec((ROW_GROUP, width), lambda s: (s, 0)),
                  pl.BlockSpec((rows, wc + LANES), lambda s: (0, 0)),
                  pl.BlockSpec((None, None, wc, width), lambda s: (s, layer, 0, 0))],
        out_specs=pl.BlockSpec((ROW_GROUP, nq.shape[1]), lambda s: (s, 0)),
        out_shape=jax.ShapeDtypeStruct((S * ROW_GROUP, nq.shape[1]), F32),
        scratch_shapes=[pltpu.VMEM((rows, 1), F32), pltpu.VMEM((rows, LANES), F32),
                        pltpu.VMEM((rows, n_kv * HEAD_DIM), F32)],
        compiler_params=_cparams("parallel"),
        name="win_decode",
    )(nq, kv_new, bias, cache_win)


def _nsa_combine_kernel(oc_ref, os_ref, ow_ref, gate_ref, o_ref, *, lane0, n_heads):
    hi, lw = _split(jax.nn.sigmoid(gate_ref[...]))
    width = n_heads * HEAD_DIM
    acc = jnp.zeros(oc_ref.shape, F32)
    for br, ref in enumerate((oc_ref, os_ref, ow_ref)):
        src = lane0 + br * n_heads + lax.shift_right_logical(_iota((LANES, width), 1), 6)
        expand = jnp.where(_iota((LANES, width), 0) == src, 1.0, 0.0).astype(BF16)
        acc = acc + (_dot(hi, expand) + _dot(lw, expand)) * ref[...]
    o_ref[...] = acc.astype(o_ref.dtype)


def nsa_combine(o_cmp, o_slc, o_win, blk_c, *, gate_block, lane0):
    M, width = o_cmp.shape
    tm = _pick(M, (512, 256, 128, 64, 32, 16, 8))
    kern = functools.partial(_nsa_combine_kernel, lane0=lane0, n_heads=width // HEAD_DIM)
    ospec = pl.BlockSpec((tm, width), lambda i: (i, 0))
    return pl.pallas_call(
        kern,
        grid=(M // tm,),
        in_specs=[ospec, ospec, ospec, pl.BlockSpec((tm, LANES), lambda i: (i, gate_block))],
        out_specs=ospec,
        out_shape=jax.ShapeDtypeStruct((M, width), BF16),
        compiler_params=_cparams("parallel"),
        name="nsa_combine",
    )(o_cmp, o_slc, o_win, blk_c)


RW_COLS = 3584


def _rwkv_prep_kernel(c_ref, head_ref, mu_ref, vec_ref, w2_ref, a2_ref, g2_ref,
                      r_ref, lw_ref, k_ref, v_ref, kk_ref, b_ref, bonus_ref, g_ref,
                      xs_ref, carry_ref, *, tm, tps, whole, n_new):
    i = pl.program_id(0)
    cols = c_ref[...]
    if whole:
        cols = cols + head_ref[...]
        head = jnp.zeros((SUBLANES, RW_COLS), F32)
    else:
        head = jnp.where(i % tps == 0, head_ref[0], carry_ref[...])
    xs_ref[pl.ds(0, SUBLANES), :] = head
    xs_ref[pl.ds(SUBLANES, tm), :] = cols
    prev = xs_ref[pl.ds(SUBLANES - 1, tm), :]
    if not whole:
        carry_ref[...] = xs_ref[pl.ds(tm, SUBLANES), :]
    xs_ref[pl.ds(SUBLANES, tm), :] = cols + (prev - cols) * mu_ref[...]
    small = xs_ref[pl.ds(SUBLANES, tm), pl.ds(3328, 256)]
    th = jnp.tanh(small).astype(BF16)
    sm = small.astype(BF16)
    sg = jax.nn.sigmoid(xs_ref[pl.ds(SUBLANES, tm), pl.ds(3072, 256)]).astype(BF16)
    seg = _seg_mat(LANES, HEAD_DIM, 1.0)
    if whole:
        real = jnp.bitwise_and(_iota((tm, LANES), 0), ROW_GROUP - 1) >= ROW_GROUP - n_new
    for j in range(8):
        cs = pl.ds(j * LANES, LANES)
        r = xs_ref[pl.ds(SUBLANES, tm), pl.ds(j * LANES, LANES)]
        k = xs_ref[pl.ds(SUBLANES, tm), pl.ds(1024 + j * LANES, LANES)]
        v = xs_ref[pl.ds(SUBLANES, tm), pl.ds(2048 + j * LANES, LANES)]
        y = vec_ref[0:1, cs] + _dot(th, w2_ref[:, cs])
        w_log = -(jnp.maximum(-y, 0.0) + jnp.log(1.0 + jnp.exp(-jnp.abs(y)))) - 0.5
        lw = -jnp.exp(w_log)
        a = jax.nn.sigmoid(vec_ref[1:2, cs] + _dot(sm, a2_ref[:, cs]))
        g = _dot(sg, g2_ref[:, cs])
        kk = k * vec_ref[2:3, cs]
        kk = kk * lax.rsqrt(jnp.maximum(_dot_hl(kk * kk, seg), 1e-24))
        k2 = k * (1.0 + (a - 1.0) * vec_ref[3:4, cs])
        b = kk * a
        bonus = _dot_hl(r * k2 * vec_ref[4:5, cs], seg) * v
        if whole:
            r, k2, v, kk, b, lw = [jnp.where(real, t, 0.0) for t in (r, k2, v, kk, b, lw)]
        r_ref[:, cs] = r
        lw_ref[:, cs] = lw
        k_ref[:, cs] = k2
        v_ref[:, cs] = v
        kk_ref[:, cs] = kk
        b_ref[:, cs] = b
        bonus_ref[:, cs] = bonus
        g_ref[:, cs] = g


def rwkv_prep(blk_c, head, mu, vec, w2p, a2p, g2, *, seq_rows, whole, n_new=0):
    M = blk_c.shape[0]
    if whole:
        tm, tps = M, 1
        head_spec = pl.BlockSpec((tm, RW_COLS), lambda i: (0, 0))
    else:
        tm = _pick(seq_rows, (256, 128, 64))
        tps = seq_rows // tm
        head_spec = pl.BlockSpec((1, SUBLANES, RW_COLS), lambda i: (i // tps, 0, 0))
    kern = functools.partial(_rwkv_prep_kernel, tm=tm, tps=tps, whole=whole, n_new=n_new)
    full = lambda shape: pl.BlockSpec(shape, lambda i: (0,) * len(shape))
    return pl.pallas_call(
        kern,
        grid=(M // tm,),
        in_specs=[pl.BlockSpec((tm, RW_COLS), lambda i: (i, 0)), head_spec, full((1, RW_COLS)),
                  full((SUBLANES, 1024)), full((256, 1024)), full((256, 1024)), full((256, 1024))],
        out_specs=[pl.BlockSpec((tm, 1024), lambda i: (i, 0))] * 8,
        out_shape=[jax.ShapeDtypeStruct((M, 1024), F32)] * 8,
        scratch_shapes=[pltpu.VMEM((tm + SUBLANES, RW_COLS), F32), pltpu.VMEM((SUBLANES, RW_COLS), F32)],
        compiler_params=_cparams("arbitrary"),
        name="rwkv_prep",
    )(blk_c, head, mu, vec, w2p, a2p, g2)


def _rwkv_par_kernel(r_ref, lw_ref, k_ref, v_ref, kk_ref, b_ref, r2_ref, y2_ref, m_ref, g_ref, *, C, npair):
    C2 = 2 * C
    cum = jnp.where(_iota((C, C), 0) >= _iota((C, C), 1), 1.0, 0.0).astype(BF16)
    keep = (_iota((C2, LANES), 0) < C) == (_iota((C2, LANES), 1) < HEAD_DIM)
    rb = _iota((C2, C2), 0)
    cb = _iota((C2, C2), 1)
    same = (rb < C) == (cb < C)
    rr = jnp.bitwise_and(rb, C - 1)
    cc = jnp.bitwise_and(cb, C - 1)
    strict = same & (rr > cc)
    incl = same & (rr >= cc)
    eye2 = jnp.where(rb == cb, 1.0, 0.0)
    eye_l = _iota((LANES, LANES), 0) == _iota((LANES, LANES), 1)

    def stack(x):
        return jnp.where(keep, jnp.concatenate([x, x], axis=0), 0.0)

    group = 8
    for p0 in range(0, npair, group):
        ps = range(p0, min(p0 + group, npair))
        cols = [pl.ds(p * LANES, LANES) for p in ps]
        n = len(cols)
        lw = [lw_ref[:, cs] for cs in cols]
        log_p = [_dot_hl_rhs(cum, x) for x in lw]
        log_end = [x[C - 1:C, :] for x in log_p]
        e_pos = [jnp.exp(x) for x in log_p]
        e_neg = [jnp.exp(-x) for x in log_p]
        e_end = [jnp.exp(le - x) for le, x in zip(log_end, log_p)]
        k = [k_ref[:, cs] for cs in cols]
        b = [b_ref[:, cs] for cs in cols]
        rt = [stack(r_ref[:, cs] * e) for cs, e in zip(cols, e_pos)]
        kt = [stack(x * e) for x, e in zip(k, e_neg)]
        bt = [stack(x * e) for x, e in zip(b, e_neg)]
        at = [stack(-kk_ref[:, cs] * jnp.exp(x - y)) for cs, x, y in zip(cols, log_p, lw)]
        kendb = [stack(x * e).astype(BF16) for x, e in zip(k, e_end)]
        bendb = [stack(x * e).astype(BF16) for x, e in zip(b, e_end)]
        v2b = [stack(v_ref[:, cs]).astype(BF16) for cs in cols]
        atb = [x.astype(BF16) for x in at]
        cross = [_dot(jnp.concatenate([atb[i], rt[i].astype(BF16)], axis=0),
                      jnp.concatenate([kt[i], bt[i]], axis=0).astype(BF16), NT) for i in range(n)]
        a_ak = [jnp.where(strict, x[:C2, :C2], 0.0).astype(BF16) for x in cross]
        a_ab = [jnp.where(strict, x[:C2, C2:], 0.0) for x in cross]
        a_rk = [jnp.where(incl, x[C2:, :C2], 0.0).astype(BF16) for x in cross]
        a_rb = [jnp.where(incl, x[C2:, C2:], 0.0).astype(BF16) for x in cross]
        akv = [_dot(x, y).astype(BF16) for x, y in zip(a_ak, v2b)]
        inv = [eye2 + x for x in a_ab]
        powr = [x.astype(BF16) for x in a_ab]
        for _ in range(int(math.log2(C)) - 1):
            powr = [_dot(x, x).astype(BF16) for x in powr]
            inv = [x + _dot(x.astype(BF16), y) for x, y in zip(inv, powr)]
        aub = [_dot(x.astype(BF16), jnp.concatenate([y, z], axis=1)).astype(BF16)
               for x, y, z in zip(inv, atb, akv)]
        ry = [_dot(x, y) for x, y in zip(a_rb, aub)]
        rkv = [_dot(x, y) for x, y in zip(a_rk, v2b)]
        mg = [_dot(x, y, TN) for x, y in zip(aub, bendb)]
        vk = [_dot(x, y, TN) for x, y in zip(v2b, kendb)]
        for i, cs in enumerate(cols):
            r2_ref[:, cs] = rt[i] + ry[i][:, :LANES]
            y2_ref[:, cs] = rkv[i] + ry[i][:, LANES:]
            decay_end = jnp.where(eye_l, jnp.broadcast_to(jnp.exp(log_end[i]), (LANES, LANES)), 0.0)
            m_ref[:, cs] = decay_end + mg[i][:LANES]
            g_ref[:, cs] = vk[i] + mg[i][LANES:]


def rwkv_par(r, lw, k, v, kk, b, C):
    M, W = r.shape
    npair = W // LANES
    nch = M // C
    kern = functools.partial(_rwkv_par_kernel, C=C, npair=npair)
    ispec = pl.BlockSpec((C, W), lambda i: (i, 0))
    return pl.pallas_call(
        kern,
        grid=(nch,),
        in_specs=[ispec] * 6,
        out_specs=[pl.BlockSpec((None, 2 * C, W), lambda i: (i, 0, 0))] * 2
        + [pl.BlockSpec((None, LANES, W), lambda i: (i, 0, 0))] * 2,
        out_shape=[jax.ShapeDtypeStruct((nch, 2 * C, W), F32)] * 2
        + [jax.ShapeDtypeStruct((nch, LANES, W), F32)] * 2,
        compiler_params=_cparams("parallel"),
        name="rwkv_par",
    )(r, lw, k, v, kk, b)


def _rwkv_seq_kernel(r2_ref, y2_ref, m_ref, g_ref, s0_ref, bonus_ref, gate_ref, ln_ref,
                     o_ref, sf_ref, s_ref, *, C, npair):
    c = pl.program_id(1)

    @pl.when(c == 0)
    def _():
        s_ref[...] = s0_ref[...]

    seg = _seg_mat(LANES, HEAD_DIM, 1.0 / HEAD_DIM)
    cols = [pl.ds(p * LANES, LANES) for p in range(npair)]
    st = [s_ref[:, cs] for cs in cols]
    y2 = [_dot3(r2_ref[:, cs], x, NT) + y2_ref[:, cs] for cs, x in zip(cols, st)]
    s_new = [_dot3(x, m_ref[:, cs]) + g_ref[:, cs] for cs, x in zip(cols, st)]
    for cs, x in zip(cols, s_new):
        s_ref[:, cs] = x
    y = [x[:C] + x[C:] for x in y2]
    mu = [_dot_hl(x, seg) for x in y]
    dev = [x - m for x, m in zip(y, mu)]
    var = [_dot_hl(x * x, seg) for x in dev]
    for cs, d, v in zip(cols, dev, var):
        yn = d * lax.rsqrt(v + GN_EPS) * ln_ref[0:1, cs] + ln_ref[1:2, cs]
        o_ref[:, cs] = ((yn + bonus_ref[:, cs]) * gate_ref[:, cs]).astype(o_ref.dtype)

    @pl.when(c == pl.num_programs(1) - 1)
    def _():
        sf_ref[...] = s_ref[...]


def rwkv_seq(r2, y2, mt, gt, s0, bonus, gate, ln, *, n_seq):
    nch_total, C2, W = r2.shape
    C = C2 // 2
    nch = nch_total // n_seq
    npair = W // LANES
    kern = functools.partial(_rwkv_seq_kernel, C=C, npair=npair)
    cspec = lambda rows: pl.BlockSpec((None, rows, W), lambda s, c: (s * nch + c, 0, 0))
    return pl.pallas_call(
        kern,
        grid=(n_seq, nch),
        in_specs=[cspec(C2), cspec(C2), cspec(LANES), cspec(LANES),
                  pl.BlockSpec((None, LANES, W), lambda s, c: (s, 0, 0)),
                  pl.BlockSpec((C, W), lambda s, c: (s * nch + c, 0)),
                  pl.BlockSpec((C, W), lambda s, c: (s * nch + c, 0)),
                  pl.BlockSpec((SUBLANES, W), lambda s, c: (0, 0))],
        out_specs=[pl.BlockSpec((C, W), lambda s, c: (s * nch + c, 0)),
                   pl.BlockSpec((None, LANES, W), lambda s, c: (s, 0, 0))],
        out_shape=[jax.ShapeDtypeStruct((nch_total * C, W), BF16),
                   jax.ShapeDtypeStruct((n_seq, LANES, W), F32)],
        scratch_shapes=[pltpu.VMEM((LANES, W), F32)],
        compiler_params=_cparams("parallel", "arbitrary"),
        name="rwkv_seq",
    )(r2, y2, mt, gt, s0, bonus, gate, ln)


def _bucket_np(d):
    d = np.maximum(d, 0)
    ratio = np.log(np.maximum(d, 1).astype(np.float32) / np.float32(N_BUCKETS // 2)) / np.float32(
        math.log(MAX_DISTANCE / (N_BUCKETS // 2)))
    large = np.minimum(N_BUCKETS // 2 + (ratio * np.float32(N_BUCKETS - N_BUCKETS // 2)).astype(np.int32),
                       N_BUCKETS - 1)
    return np.where(d < N_BUCKETS // 2, d, large).astype(np.int32)


def _overlap_np(n_cmp, n_sel):
    c0 = np.arange(n_cmp)[:, None] * CMP_STRIDE
    j0 = np.arange(n_sel)[None, :] * SEL_BLOCK
    ov = np.clip(np.minimum(c0 + CMP_BLOCK, j0 + SEL_BLOCK) - np.maximum(c0, j0), 0, None)
    return (ov / CMP_BLOCK).astype(np.float32)


def _bias_table_kernel(tab_ref, idx_ref, o_ref):
    h = pl.program_id(0)
    idx = idx_ref[...]
    acc = jnp.zeros(idx.shape, F32)
    for b in range(N_BUCKETS):
        acc = jnp.where(idx == b, tab_ref[h, b], acc)
    o_ref[...] = acc


def bias_table(tab_h, idx):
    H = tab_h.shape[0]
    R, C = idx.shape
    tr = _pick(R, (256, 128, 64, 32, 16, 8))
    return pl.pallas_call(
        _bias_table_kernel,
        grid=(H, R // tr),
        in_specs=[pl.BlockSpec(memory_space=pltpu.SMEM),
                  pl.BlockSpec((tr, C), lambda h, i: (i, 0))],
        out_specs=pl.BlockSpec((None, tr, C), lambda h, i: (h, i, 0)),
        out_shape=jax.ShapeDtypeStruct((H, R, C), F32),
        compiler_params=_cparams("parallel", "parallel"),
        name="bias_table",
    )(tab_h, idx)


def _make_tables(rel_bias, T, past, n_new, n_heads):
    tab_h = rel_bias.astype(F32).T

    def take(idx):
        idx = np.asarray(idx, np.int32)
        out = bias_table(tab_h, jnp.asarray(idx.reshape(-1, idx.shape[-1])))
        return out.reshape((tab_h.shape[0],) + idx.shape)

    t = {}
    tq = _pick(T, (256, 128))
    nq_t = T // tq
    far = 0
    while _bucket_np(np.array([far]))[0] < N_BUCKETS - 1:
        far += 1
    nd = min(-(-(far + tq - 1) // tq) + 1, nq_t)
    i = np.arange(tq)[:, None]
    j = np.arange(tq)[None, :]
    t["tiles"] = take(np.stack([_bucket_np(dl * tq + i - j) for dl in range(nd)]))
    ncp = T // CMP_STRIDE
    tt = np.arange(T)[:, None]
    cc = np.arange(ncp)[None, :]
    t["cmp_p"] = take(_bucket_np(tt - (cc * CMP_STRIDE + CMP_BLOCK - 1)))
    n_sel = -(-T // SEL_BLOCK)
    nsp = -(-n_sel // SUBLANES) * SUBLANES
    ov = np.zeros((nsp, ncp), np.float32)
    ov[:n_sel, :ncp - 1] = _overlap_np(ncp - 1, n_sel).T
    t["ov_p"] = jnp.asarray(ov, BF16)
    t["n_sel_p"] = n_sel
    t["tq"] = tq
    rows = n_heads * ROW_GROUP
    tpos = past - (ROW_GROUP - n_new) + np.arange(ROW_GROUP)
    ppb = _pick(past // LANES, (4, 2, 1))
    kpos = np.concatenate([np.arange(past), past - (ROW_GROUP - n_new) + np.arange(LANES),
                           np.zeros((ppb - 1) * LANES, np.int64)])
    idx = _bucket_np(tpos[:, None] - kpos[None, :])
    t["slc_d"] = take(idx).reshape(rows, past + ppb * LANES)
    wc = min(WINDOW, past)
    kpos = np.concatenate([past - wc + np.arange(wc), past - (ROW_GROUP - n_new) + np.arange(LANES)])
    t["win_d"] = take(_bucket_np(tpos[:, None] - kpos[None, :])).reshape(rows, wc + LANES)
    ncp_d = past // CMP_STRIDE
    cc = np.arange(ncp_d)[None, :]
    t["cmp_d"] = take(_bucket_np(tpos[:, None] - (cc * CMP_STRIDE + CMP_BLOCK - 1)))
    n_sel_d = -(-(past + n_new) // SEL_BLOCK)
    nsp_d = -(-n_sel_d // LANES) * LANES
    n_cmp_d = (past + n_new) // CMP_STRIDE - 1
    ov = np.zeros((ncp_d, nsp_d), np.float32)
    ov[:n_cmp_d, :n_sel_d] = _overlap_np(n_cmp_d, n_sel_d)
    t["ov_d"] = jnp.asarray(ov, BF16)
    t["n_sel_d"] = n_sel_d
    return t


def _prep_layer(l, P):
    W = {}
    w_in = P["w_in"][l]
    sbw = 1024
    W["wA"] = jnp.concatenate([w_in[:, :sbw] * HEAD_DIM ** -0.5, w_in[:, sbw:3 * sbw]], axis=1).astype(BF16)
    W["wB"] = w_in[:, 3072:5632].astype(BF16)
    D = w_in.shape[0]
    W["wC"] = jnp.concatenate([w_in[:, 5680:8752], w_in[:, 8944:9200], w_in[:, 8752:8944],
                               w_in[:, 5632:5680], jnp.zeros((D, 16), F32)], axis=1).astype(BF16)
    W["wD"] = w_in[:, 9200:].astype(BF16)
    W["norm_mix"] = P["norm_mix_g"][l]
    W["norm_ffn"] = P["norm_ffn_g"][l]
    gains = P["nsa_qk_gain"][l]
    W["gains"] = jnp.pad(jnp.tile(gains, (1, 2)), ((0, 4), (0, 0)))
    W["gain1"] = jnp.tile(gains[1:2], (1, 2))
    w1 = P["cmp_w1"][l].reshape(2, 2, CMP_STRIDE, HEAD_DIM, HEAD_DIM)
    first, second = w1[:, 0], w1[:, 1]
    z = jnp.zeros_like(first)
    W["w1c"] = jnp.concatenate([jnp.concatenate([first, z, second, z], axis=-1),
                                jnp.concatenate([z, first, z, second], axis=-1)], axis=-2).astype(BF16)
    pos = jnp.einsum("cld,clde->ce", P["cmp_pe"][l], P["cmp_w1"][l], precision=lax.Precision.HIGHEST)
    W["posb"] = jnp.pad(jnp.tile(pos + P["cmp_b1"][l], (1, 2)), ((0, 6), (0, 0)))
    w2 = P["cmp_w2"][l]
    w2d = jnp.concatenate([w2, w2], axis=-1)
    z2 = jnp.zeros_like(w2d)
    W["w2dup"] = jnp.stack([jnp.concatenate([w2d, z2], axis=1), jnp.concatenate([z2, w2d], axis=1)],
                           axis=1).astype(BF16)
    W["b2"] = jnp.pad(jnp.tile(P["cmp_b2"][l], (1, 2)), ((0, 6), (0, 0)))
    mu = P["rwkv_mu"][l]
    W["mu"] = jnp.concatenate([mu[:3072], mu[3264:3520], mu[3072:3264], jnp.zeros((64,), F32)])[None]
    W["vec"] = jnp.pad(jnp.stack([P["rwkv_w0"][l], P["rwkv_a0"][l], P["rwkv_kk"][l], P["rwkv_ka"][l],
                                  P["rwkv_rk"][l].reshape(-1)]), ((0, 3), (0, 0)))
    W["w2p"] = jnp.pad(P["rwkv_w2"][l], ((0, 160), (0, 0))).astype(BF16)
    W["a2p"] = jnp.pad(P["rwkv_a2"][l], ((96, 64), (0, 0))).astype(BF16)
    W["g2"] = P["rwkv_g2"][l].astype(BF16)
    W["ln"] = jnp.pad(jnp.stack([P["rwkv_lnx_w"][l], P["rwkv_lnx_b"][l]]), ((0, 6), (0, 0)))
    W["wb"] = P["w_branch"][l].astype(BF16)
    W["wout"] = P["w_out"][l].astype(BF16)
    F = P["ffn_conv_w"].shape[-1]
    Fp = -(-F // 512) * 512
    padc = lambda a: jnp.pad(a, ((0, 0), (0, Fp - F)))
    w_up = P["ffn_w_up"][l]
    W["wu"] = padc(w_up[:, :F]).astype(BF16)
    W["wg"] = padc(w_up[:, F:]).astype(BF16)
    W["cw"] = jnp.pad(padc(P["ffn_conv_w"][l]), ((0, 5), (0, 0)))
    W["cb"] = padc(P["ffn_conv_b"][l][None])
    W["wd"] = jnp.pad(P["ffn_w_down"][l], ((0, Fp - F), (0, 0))).astype(BF16)
    W["F"], W["Fp"] = F, Fp
    return W


def _rw_to_mine(a):
    pad = jnp.zeros(a.shape[:-1] + (64,), a.dtype)
    return jnp.concatenate([a[..., :3072], a[..., 3264:3520], a[..., 3072:3264], pad], axis=-1)


def _rw_from_mine(a):
    return jnp.concatenate([a[..., :3072], a[..., 3328:3520], a[..., 3072:3328]], axis=-1)


def _state_to_pairs(s):
    S, H = s.shape[:2]
    sp = s.reshape(S, H // 2, 2, HEAD_DIM, HEAD_DIM)
    z = jnp.zeros_like(sp[:, :, 0])
    top = jnp.concatenate([sp[:, :, 0], z], axis=-1)
    bot = jnp.concatenate([z, sp[:, :, 1]], axis=-1)
    bd = jnp.concatenate([top, bot], axis=-2)
    return bd.transpose(0, 2, 1, 3).reshape(S, LANES, H // 2 * LANES)


def _state_from_pairs(bd, H):
    S = bd.shape[0]
    b4 = bd.reshape(S, LANES, H // 2, LANES).transpose(0, 2, 1, 3)
    return jnp.stack([b4[:, :, :HEAD_DIM, :HEAD_DIM], b4[:, :, HEAD_DIM:, HEAD_DIM:]], axis=2).reshape(
        S, H, HEAD_DIM, HEAD_DIM)


def _layer(x, W, tb, *, prompt, n_seq, seq_rows, n_new, past, dec):
    blk_a = rms_matmul(x, W["norm_mix"], W["wA"])
    blk_b = rms_matmul(x, W["norm_mix"], W["wB"])
    blk_c = rms_matmul(x, W["norm_mix"], W["wC"])
    blk_d = rms_matmul(x, W["norm_mix"], W["wD"])
    nq, cmp_new, slc_new, win_new = nsa_prep(blk_b, W["gains"])
    M = x.shape[0]
    if prompt:
        o_a = sb_prompt(blk_a, n_seq, seq_rows, 16)
        fs = compress(cmp_new.reshape(M // LANES, 1, LANES, 512), 0, W["w1c"])
        kd, vd = cmp_finish(fs.reshape(n_seq, seq_rows // CMP_STRIDE, 1024), W["posb"], W["w2dup"], W["b2"],
                            W["gain1"])
        tq = tb["tq"]
        o_cmp, sel = cmp_attn(nq, kd, vd, tb["cmp_p"], tb["ov_p"], S=n_seq, tq=tq, n_q=seq_rows // tq,
                              n_sel=tb["n_sel_p"], pos0=0, transposed=True)
        o_slc = flash_prompt(nq, slc_new, tb["tiles"], sel, B=n_seq, T=seq_rows, mode="slc")
        o_win = flash_prompt(nq, win_new, tb["tiles"], None, B=n_seq, T=seq_rows, mode="win")
        head = jnp.zeros((n_seq, SUBLANES, RW_COLS), F32)
        rw = rwkv_prep(blk_c, head, W["mu"], W["vec"], W["w2p"], W["a2p"], W["g2"], seq_rows=seq_rows,
                       whole=False)
        C = RWKV_CHUNK
        s0 = jnp.zeros((n_seq, LANES, 1024), F32)
    else:
        lyr = dec["layer"]
        pt = dec["page_table"]
        o_a = sb_decode(blk_a, dec["cache_sb"], pt, lyr, n_new)
        fs_phys = compress(dec["cache_cmp"], lyr, W["w1c"])
        fs = gather_page_rows(fs_phys, pt)
        kd, vd = cmp_finish(fs, W["posb"], W["w2dup"], W["b2"], W["gain1"])
        o_cmp, sel = cmp_attn(nq, kd, vd, tb["cmp_d"], tb["ov_d"], S=n_seq, tq=ROW_GROUP, n_q=1,
                              n_sel=tb["n_sel_d"], pos0=past - (ROW_GROUP - n_new), transposed=False)
        o_slc = slc_decode(nq, slc_new, sel, tb["slc_d"], dec["cache_slc"], pt, lyr, n_new)
        o_win = win_decode(nq, win_new, tb["win_d"], dec["cache_win"], lyr, n_new)
        shift = _rw_to_mine(dec["shift"][:, lyr])
        head = jnp.zeros((n_seq, ROW_GROUP, RW_COLS), F32).at[:, ROW_GROUP - n_new - 1].set(shift)
        rw = rwkv_prep(blk_c, head.reshape(M, RW_COLS), W["mu"], W["vec"], W["w2p"], W["a2p"], W["g2"],
                       seq_rows=ROW_GROUP, whole=True, n_new=n_new)
        C = ROW_GROUP
        s0 = _state_to_pairs(dec["rwkv"][:, lyr].astype(F32))
    o_b = nsa_combine(o_cmp, o_slc, o_win, blk_c, gate_block=27, lane0=64)
    r, lw, k2, v, kk, b, bonus, g = rw
    r2, y2, mt, gt = rwkv_par(r, lw, k2, v, kk, b, C)
    o_c, s_fin = rwkv_seq(r2, y2, mt, gt, s0, bonus, g, W["ln"], n_seq=n_seq)
    mixed = merge_mix(o_a, o_b, o_c, W["wb"], blk_d)
    x1 = matmul_res(x, mixed, W["wout"])
    F, Fp = W["F"], W["Fp"]
    if prompt:
        x2, tail = conv_ffn(x1, W["norm_ffn"], W["wu"], W["wg"], W["cw"], W["cb"], W["wd"],
                            jnp.zeros((n_seq, SUBLANES, Fp), F32), seq_rows=seq_rows, whole_gate=False)
        conv = tail[:, SUBLANES - 2:, :F]
        last = blk_c.reshape(n_seq, seq_rows, RW_COLS)[:, -1]
    else:
        inj = jnp.zeros((n_seq, ROW_GROUP, Fp), F32).at[:, ROW_GROUP - n_new - 2:ROW_GROUP - n_new, :F].set(
            dec["conv"][:, lyr])
        real = (jnp.arange(M) % ROW_GROUP >= ROW_GROUP - n_new)[:, None]
        x2, gate = conv_ffn(jnp.where(real, x1, 0.0), W["norm_ffn"], W["wu"], W["wg"], W["cw"], W["cb"], W["wd"],
                            inj.reshape(M, Fp), seq_rows=ROW_GROUP, whole_gate=True)
        x2 = jnp.where(real, x2, 0.0)
        conv = gate.reshape(n_seq, ROW_GROUP, Fp)[:, ROW_GROUP - 2:, :F]
        last = blk_c.reshape(n_seq, ROW_GROUP, RW_COLS)[:, -1]
    new = {"sb": blk_a[:, 1024:], "cmp": cmp_new, "slc": slc_new, "win": win_new,
           "rwkv": _state_from_pairs(s_fin, 16), "shift": _rw_from_mine(last), "conv": conv}
    return x2, new


def kernel(x_prompt, x_sample, cache_sb_kv, cache_cmp_kv, cache_slc_kv, cache_win_kv, state_rwkv, state_rwkv_shift, state_conv, page_table, rel_bias, norm_mix_g, norm_ffn_g, w_in, nsa_qk_gain, cmp_pe, cmp_w1, cmp_b1, cmp_w2, cmp_b2, rwkv_mu, rwkv_w0, rwkv_w2, rwkv_a0, rwkv_a2, rwkv_g2, rwkv_kk, rwkv_ka, rwkv_rk, rwkv_lnx_w, rwkv_lnx_b, w_branch, w_out, ffn_w_up, ffn_conv_w, ffn_conv_b, ffn_w_down):
    P = dict(norm_mix_g=norm_mix_g, norm_ffn_g=norm_ffn_g, w_in=w_in, nsa_qk_gain=nsa_qk_gain, cmp_pe=cmp_pe,
             cmp_w1=cmp_w1, cmp_b1=cmp_b1, cmp_w2=cmp_w2, cmp_b2=cmp_b2, rwkv_mu=rwkv_mu, rwkv_w0=rwkv_w0,
             rwkv_w2=rwkv_w2, rwkv_a0=rwkv_a0, rwkv_a2=rwkv_a2, rwkv_g2=rwkv_g2, rwkv_kk=rwkv_kk,
             rwkv_ka=rwkv_ka, rwkv_rk=rwkv_rk, rwkv_lnx_w=rwkv_lnx_w, rwkv_lnx_b=rwkv_lnx_b,
             w_branch=w_branch, w_out=w_out, ffn_w_up=ffn_w_up, ffn_conv_w=ffn_conv_w,
             ffn_conv_b=ffn_conv_b, ffn_w_down=ffn_w_down)
    B, T, D = x_prompt.shape
    S, n_new, _ = x_sample.shape
    depth = w_in.shape[0]
    n_phys, _, page = cache_sb_kv.shape[:3]
    n_pages = page_table.shape[1]
    past = n_pages * page
    n_heads = rel_bias.shape[1]
    assert page == LANES and n_new + 2 <= ROW_GROUP and T % 128 == 0 and past % LANES == 0
    assert cache_win_kv.shape[2] == WINDOW <= past
    tb = _make_tables(rel_bias, T, past, n_new, n_heads)
    dec = {"page_table": page_table,
           "cache_sb": cache_sb_kv.astype(BF16).reshape(n_phys, depth, page, -1),
           "cache_cmp": cache_cmp_kv.reshape(n_phys, depth, page, -1),
           "cache_slc": cache_slc_kv.astype(BF16).reshape(n_phys, depth, page, -1),
           "cache_win": cache_win_kv.reshape(S, depth, WINDOW, -1),
           "rwkv": state_rwkv, "shift": state_rwkv_shift, "conv": state_conv}
    xp = x_prompt.reshape(B * T, D)
    xs = jnp.zeros((S, ROW_GROUP, D), F32).at[:, ROW_GROUP - n_new:].set(x_sample).reshape(S * ROW_GROUP, D)
    new_p, new_s = [], []
    for l in range(depth):
        W = _prep_layer(l, P)
        xp, st = _layer(xp, W, tb, prompt=True, n_seq=B, seq_rows=T, n_new=0, past=0, dec=None)
        new_p.append(st)
        dec["layer"] = l
        xs, st = _layer(xs, W, tb, prompt=False, n_seq=S, seq_rows=ROW_GROUP, n_new=n_new, past=past, dec=dec)
        new_s.append(st)

    n_win = min(WINDOW, T)

    def stack_p(name, shape):
        return jnp.stack([st[name].reshape((B, -1) + shape) for st in new_p], axis=1)

    def rows_s(a):
        return a.reshape(S, ROW_GROUP, -1)[:, ROW_GROUP - n_new:]

    def stack_s(name, shape):
        return jnp.stack([rows_s(st[name]).reshape((S, n_new) + shape) for st in new_s], axis=1)

    kvh = cache_cmp_kv.shape[4]
    sbh = cache_sb_kv.shape[4]
    p_win = jnp.stack([st["win"].reshape(B, T, 2, kvh, HEAD_DIM)[:, T - n_win:] for st in new_p], axis=1)
    s_win = jnp.stack([jnp.concatenate([cache_win_kv[:, l], rows_s(st["win"]).reshape(S, n_new, 2, kvh, HEAD_DIM)],
                                       axis=1)[:, n_new:] for l, st in enumerate(new_s)], axis=1)
    outs = (xp.reshape(B, T, D), rows_s(xs),
            stack_p("sb", (2, sbh, HEAD_DIM)), stack_p("cmp", (2, kvh, HEAD_DIM)),
            stack_p("slc", (2, kvh, HEAD_DIM)), p_win,
            jnp.stack([st["rwkv"] for st in new_p], axis=1), jnp.stack([st["shift"] for st in new_p], axis=1),
            jnp.stack([st["conv"] for st in new_p], axis=1),
            stack_s("sb", (2, sbh, HEAD_DIM)), stack_s("cmp", (2, kvh, HEAD_DIM)),
            stack_s("slc", (2, kvh, HEAD_DIM)), s_win,
            jnp.stack([st["rwkv"] for st in new_s], axis=1), jnp.stack([st["shift"] for st in new_s], axis=1),
            jnp.stack([st["conv"] for st in new_s], axis=1))
    return outs
```

```python
import functools
import math

import numpy as np
import jax
import jax.numpy as jnp
from jax import lax
from jax.experimental import pallas as pl
from jax.experimental.pallas import tpu as pltpu

F32 = jnp.float32
BF16 = jnp.bfloat16

HEAD_DIM = 64
LANES = 128
SUBLANES = 8
ROW_GROUP = 8
VMEM_LIMIT = 56 * 2 ** 20

RMS_EPS = 1e-6
GN_EPS = 64e-5
CMP_STRIDE = 16
CMP_BLOCK = 32
SEL_BLOCK = 64
N_SEL = 16
WINDOW = 512
N_BUCKETS = 32
MAX_DISTANCE = 1024
NEG = -1e30
M_FLOOR = -1e20
RWKV_CHUNK = 64

NN = (((1,), (0,)), ((), ()))
NT = (((1,), (1,)), ((), ()))
TN = (((0,), (0,)), ((), ()))


def _dot(a, b, dims=NN):
    return lax.dot_general(a, b, dims, preferred_element_type=F32)


def _split(x):
    hi = x.astype(BF16)
    lo = (x - hi.astype(F32)).astype(BF16)
    return hi, lo


def _dot_hl(a, b_exact, dims=NN):
    hi, lo = _split(a)
    return _dot(hi, b_exact, dims) + _dot(lo, b_exact, dims)


def _dot_hl_rhs(a_exact, b):
    hi, lo = _split(b)
    return _dot(a_exact, hi) + _dot(a_exact, lo)


def _dot3(a, b, dims=NN):
    ah, al = _split(a)
    bh, bl = _split(b)
    return _dot(ah, bh, dims) + (_dot(ah, bl, dims) + _dot(al, bh, dims))


def _iota(shape, dim):
    return lax.broadcasted_iota(jnp.int32, shape, dim)


def _pick(n, cands):
    for c in cands:
        if n % c == 0:
            return c
    raise ValueError(f"no tile for {n} in {cands}")


def _cparams(*sem):
    return pltpu.CompilerParams(dimension_semantics=sem, vmem_limit_bytes=VMEM_LIMIT)


def _seg_mat(n, seg, scale):
    r = lax.shift_right_logical(_iota((n, n), 0), int(math.log2(seg)))
    c = lax.shift_right_logical(_iota((n, n), 1), int(math.log2(seg)))
    return jnp.where(r == c, scale, 0.0).astype(BF16)


def _lo_mask():
    return _iota((1, LANES), 1) < HEAD_DIM


def _rms_rows(x, g):
    ms = jnp.mean(x * x, axis=-1, keepdims=True)
    return x * lax.rsqrt(ms + RMS_EPS) * g


def _rms_matmul_kernel(x_ref, g_ref, w_ref, o_ref, h_ref):
    @pl.when(pl.program_id(1) == 0)
    def _():
        h_ref[...] = _rms_rows(x_ref[...], g_ref[...]).astype(BF16)

    o_ref[...] = _dot(h_ref[...], w_ref[...])


def rms_matmul(x, g, w, tn=512):
    M, K = x.shape
    N = w.shape[1]
    tm = _pick(M, (1024, 512, 256, 128, 64, 32, 16, 8))
    tn = _pick(N, (tn, 256, 128))
    return pl.pallas_call(
        _rms_matmul_kernel,
        grid=(M // tm, N // tn),
        in_specs=[pl.BlockSpec((tm, K), lambda i, j: (i, 0)),
                  pl.BlockSpec((1, K), lambda i, j: (0, 0)),
                  pl.BlockSpec((K, tn), lambda i, j: (0, j))],
        out_specs=pl.BlockSpec((tm, tn), lambda i, j: (i, j)),
        out_shape=jax.ShapeDtypeStruct((M, N), F32),
        scratch_shapes=[pltpu.VMEM((tm, K), BF16)],
        compiler_params=_cparams("parallel", "arbitrary"),
        name="rms_matmul",
    )(x, g.reshape(1, K), w)


def _matmul_res_kernel(x_ref, a_ref, w_ref, o_ref):
    o_ref[...] = x_ref[...] + _dot(a_ref[...], w_ref[...])


def matmul_res(x, a, w, tn=512):
    M, N = x.shape
    K = a.shape[1]
    tm = _pick(M, (1024, 512, 256, 128, 64, 32, 16, 8))
    tn = _pick(N, (tn, 256, 128))
    return pl.pallas_call(
        _matmul_res_kernel,
        grid=(M // tm, N // tn),
        in_specs=[pl.BlockSpec((tm, tn), lambda i, j: (i, j)),
                  pl.BlockSpec((tm, K), lambda i, j: (i, 0)),
                  pl.BlockSpec((K, tn), lambda i, j: (0, j))],
        out_specs=pl.BlockSpec((tm, tn), lambda i, j: (i, j)),
        out_shape=jax.ShapeDtypeStruct((M, N), F32),
        compiler_params=_cparams("parallel", "arbitrary"),
        name="matmul_res",
    )(x, a, w)


def _merge_mix_kernel(oa_ref, ob_ref, oc_ref, wb_ref, m0_ref, m1_ref, m2_ref, o_ref):
    acc = jax.nn.sigmoid(m0_ref[...]) * _dot(oa_ref[...], wb_ref[0])
    acc = acc + jax.nn.sigmoid(m1_ref[...]) * _dot(ob_ref[...], wb_ref[1])
    acc = acc + jax.nn.sigmoid(m2_ref[...]) * _dot(oc_ref[...], wb_ref[2])
    o_ref[...] = acc.astype(o_ref.dtype)


def merge_mix(oa, ob, oc, wb, merge, tn=512):
    M, W = oa.shape
    D = wb.shape[2]
    tm = _pick(M, (512, 256, 128, 64, 32, 16, 8))
    nb = D // tn
    ospec = pl.BlockSpec((tm, W), lambda i, j: (i, 0))
    return pl.pallas_call(
        _merge_mix_kernel,
        grid=(M // tm, nb),
        in_specs=[ospec, ospec, ospec,
                  pl.BlockSpec((3, W, tn), lambda i, j: (0, 0, j)),
                  pl.BlockSpec((tm, tn), lambda i, j: (i, j)),
                  pl.BlockSpec((tm, tn), lambda i, j: (i, j + nb)),
                  pl.BlockSpec((tm, tn), lambda i, j: (i, j + 2 * nb))],
        out_specs=pl.BlockSpec((tm, tn), lambda i, j: (i, j)),
        out_shape=jax.ShapeDtypeStruct((M, D), BF16),
        compiler_params=_cparams("parallel", "arbitrary"),
        name="merge_mix",
    )(oa, ob, oc, wb, merge, merge, merge)


def _ffn_kernel(x_ref, g_ref, wu_ref, wg_ref, cw_ref, cb_ref, wd_ref, prev_ref,
                o_ref, st_ref, h_ref, acc_ref, gs_ref, carry_ref, *, tm, tps, whole_gate):
    i = pl.program_id(0)
    n = pl.program_id(1)

    @pl.when(n == 0)
    def _():
        h_ref[...] = _rms_rows(x_ref[...], g_ref[...]).astype(BF16)
        acc_ref[...] = jnp.zeros_like(acc_ref)

    h = h_ref[...]
    u = _dot(h, wu_ref[...])
    g = _dot(h, wg_ref[...])
    if whole_gate:
        g = g + prev_ref[...]
        head = jnp.zeros((SUBLANES, g.shape[1]), F32)
    else:
        head = jnp.where(i % tps == 0, prev_ref[0], carry_ref[n])
    gs_ref[pl.ds(0, SUBLANES), :] = head
    gs_ref[pl.ds(SUBLANES, tm), :] = g
    gm1 = gs_ref[pl.ds(SUBLANES - 1, tm), :]
    gm2 = gs_ref[pl.ds(SUBLANES - 2, tm), :]
    gc = cb_ref[...] + gm2 * cw_ref[0:1, :] + gm1 * cw_ref[1:2, :] + g * cw_ref[2:3, :]
    act = (gc * jax.nn.sigmoid(gc) * u).astype(BF16)
    acc_ref[...] += _dot(act, wd_ref[...])
    if whole_gate:
        st_ref[...] = g
    else:
        tail = gs_ref[pl.ds(tm, SUBLANES), :]
        carry_ref[n] = tail
        st_ref[0] = tail

    @pl.when(n == pl.num_programs(1) - 1)
    def _():
        o_ref[...] = x_ref[...] + acc_ref[...]


def conv_ffn(x, g, wu, wg, cw, cb, wd, prev, *, seq_rows, whole_gate, tn=512):
    M, D = x.shape
    Fp = wu.shape[1]
    nb = Fp // tn
    if whole_gate:
        tm, tps = M, 1
        prev_spec = pl.BlockSpec((tm, tn), lambda i, n: (0, n))
        st_spec = pl.BlockSpec((tm, tn), lambda i, n: (0, n))
        st_shape = jax.ShapeDtypeStruct((M, Fp), F32)
    else:
        tm = _pick(seq_rows, (512, 256, 128, 64))
        tps = seq_rows // tm
        prev_spec = pl.BlockSpec((1, SUBLANES, tn), lambda i, n: (i // tps, 0, n))
        st_spec = pl.BlockSpec((1, SUBLANES, tn), lambda i, n: (i, 0, n))
        st_shape = jax.ShapeDtypeStruct((M // tm, SUBLANES, Fp), F32)
    kern = functools.partial(_ffn_kernel, tm=tm, tps=tps, whole_gate=whole_gate)
    y, st = pl.pallas_call(
        kern,
        grid=(M // tm, nb),
        in_specs=[pl.BlockSpec((tm, D), lambda i, n: (i, 0)),
                  pl.BlockSpec((1, D), lambda i, n: (0, 0)),
                  pl.BlockSpec((D, tn), lambda i, n: (0, n)),
                  pl.BlockSpec((D, tn), lambda i, n: (0, n)),
                  pl.BlockSpec((SUBLANES, tn), lambda i, n: (0, n)),
                  pl.BlockSpec((1, tn), lambda i, n: (0, n)),
                  pl.BlockSpec((tn, D), lambda i, n: (n, 0)),
                  prev_spec],
        out_specs=[pl.BlockSpec((tm, D), lambda i, n: (i, 0)), st_spec],
        out_shape=[jax.ShapeDtypeStruct((M, D), F32), st_shape],
        scratch_shapes=[pltpu.VMEM((tm, D), BF16), pltpu.VMEM((tm, D), F32),
                        pltpu.VMEM((tm + SUBLANES, tn), F32), pltpu.VMEM((nb, SUBLANES, tn), F32)],
        compiler_params=_cparams("arbitrary", "arbitrary"),
        name="conv_ffn",
    )(x, g.reshape(1, D), wu, wg, cw, cb, wd, prev)
    return (y, st) if whole_gate else (y, st[tps - 1::tps])


def _sb_rhs():
    j = jnp.bitwise_and(_iota((2 * LANES, 2 * LANES), 0), LANES - 1)
    s = _iota((2 * LANES, 2 * LANES), 1)
    return jnp.where((j > s) | (s >= LANES), -1.0, 0.0).astype(BF16)


def _sb_sums(sp, nrhs):
    hi, lw = _split(sp)
    return _dot(jnp.concatenate([hi, lw], axis=1), nrhs)


LOG2E = math.log2(math.e)


def _softplus2(z):
    return jnp.maximum(z, 0.0) + jnp.log2(1.0 + jnp.exp2(-jnp.abs(z)))


def _sb_block(qm, kb, vb, before, c, nrhs):
    z = _dot(qm, kb, NT)
    sp = _softplus2(z)
    if before is not None:
        sp = jnp.where(before, sp, 0.0)
    cs2 = _sb_sums(sp, nrhs)
    w = jnp.exp2((z - sp) + (cs2[:, :LANES] + c))
    if before is not None:
        w = jnp.where(before, w, 0.0)
    return _dot(w.astype(BF16), vb), c + cs2[:, LANES:]


def _sb_pair_blocks(q, kbs, vbs, befores, c, nrhs, lo):
    def per_head(x):
        zero = jnp.zeros_like(x)
        return jnp.concatenate([jnp.where(lo, x, zero), jnp.where(lo, zero, x)], axis=0)

    kbd = [per_head(x) for x in kbs]
    vbd = [per_head(x) for x in vbs]
    z = [_dot(q, x, NT) for x in kbd]
    sp = [_softplus2(x) for x in z]
    sp = [x if m is None else jnp.where(m, x, 0.0) for x, m in zip(sp, befores)]
    cs = [[_sb_sums(x[:, h * LANES:(h + 1) * LANES], nrhs) for h in range(2)] for x in sp]
    suffix = [jnp.concatenate([y[0][:, :LANES], y[1][:, :LANES]], axis=1) for y in cs]
    total = [jnp.concatenate([y[0][:, LANES:], y[1][:, LANES:]], axis=1) for y in cs]
    carry = [c]
    for t in total:
        carry.append(carry[-1] + t)
    w = [jnp.exp2((a - b) + (s + cc)) for a, b, s, cc in zip(z, sp, suffix, carry)]
    w = [x if m is None else jnp.where(m, x, 0.0) for x, m in zip(w, befores)]
    pv = [_dot(x.astype(BF16), y) for x, y in zip(w, vbd)]
    acc = pv[0]
    for x in pv[1:]:
        acc = acc + x
    return acc, carry[-1]


def _sb_prompt_kernel(qi_ref, kj_ref, q_ref, k_ref, v_ref, o_ref, acc_ref, c_ref, *, tq, tk):
    s = pl.program_id(2)
    qi = qi_ref[s]
    kj = kj_ref[s]
    lo = _lo_mask()
    first = (kj + 1) * tk == (qi + 1) * tq
    overlap = (kj + 1) * tk > qi * tq

    def tile(masked):
        q = (q_ref[...] * LOG2E).astype(BF16)
        nrhs = _sb_rhs()
        c = jnp.where(first, 0.0, c_ref[...])
        subs = list(reversed(range(tk // LANES)))
        kbs = [k_ref[pl.ds(sub * LANES, LANES), :].astype(BF16) for sub in subs]
        vbs = [v_ref[pl.ds(sub * LANES, LANES), :].astype(BF16) for sub in subs]
        befores = [None] * len(subs)
        if masked:
            lane_key = jnp.bitwise_and(_iota((tq, 2 * LANES), 1), LANES - 1)
            row = qi * tq + _iota((tq, 2 * LANES), 0)
            befores = [kj * tk + sub * LANES + lane_key < row for sub in subs]
        acc, c = _sb_pair_blocks(q, kbs, vbs, befores, c, nrhs, lo)
        c_ref[...] = c
        acc_ref[...] = jnp.where(first, 0.0, acc_ref[...]) + acc

    @pl.when(overlap)
    def _():
        tile(True)

    @pl.when(jnp.logical_not(overlap))
    def _():
        tile(False)

    @pl.when(kj == 0)
    def _():
        o_ref[...] = acc_ref[...].astype(o_ref.dtype)


def _tri_steps(n, lookback=None):
    qs, ks, first, last = [], [], [], []
    for q in range(n):
        k_lo = 0 if lookback is None else max(0, q - lookback)
        for k in range(q, k_lo - 1, -1):
            qs.append(q)
            ks.append(k)
            first.append(int(k == q))
            last.append(int(k == k_lo))
    return tuple(jnp.asarray(np.array(a, np.int32)) for a in (qs, ks, first, last))


def _sb_steps(nq, ratio):
    qs, ks = [], []
    for q in range(nq):
        for k in range((q + 1) * ratio - 1, -1, -1):
            qs.append(q)
            ks.append(k)
    return jnp.asarray(np.array(qs, np.int32)), jnp.asarray(np.array(ks, np.int32))


def sb_prompt(qkv, B, T, n_heads, tq=None, tk=None):
    M = B * T
    tk = tk or _pick(T, (512, 256, 128))
    tq = tq or _pick(T, (512, 256, 128))
    assert tq % tk == 0
    nq, nk = T // tq, T // tk
    npair = n_heads // 2
    qs, ks = _sb_steps(nq, tq // tk)
    kern = functools.partial(_sb_prompt_kernel, tq=tq, tk=tk)
    return pl.pallas_call(
        kern,
        grid_spec=pltpu.PrefetchScalarGridSpec(
            num_scalar_prefetch=2,
            grid=(B, npair, int(qs.shape[0])),
            in_specs=[pl.BlockSpec((tq, LANES), lambda b, p, s, qi, kj: (b * nq + qi[s], p)),
                      pl.BlockSpec((tk, LANES), lambda b, p, s, qi, kj: (b * nk + kj[s], npair + p)),
                      pl.BlockSpec((tk, LANES), lambda b, p, s, qi, kj: (b * nk + kj[s], 2 * npair + p))],
            out_specs=pl.BlockSpec((tq, LANES), lambda b, p, s, qi, kj: (b * nq + qi[s], p)),
            scratch_shapes=[pltpu.VMEM((tq, LANES), F32), pltpu.VMEM((tq, 2 * LANES), F32)]),
        out_shape=jax.ShapeDtypeStruct((M, n_heads * HEAD_DIM), BF16),
        compiler_params=_cparams("parallel", "parallel", "arbitrary"),
        name="sb_prompt",
    )(qs, ks, qkv, qkv, qkv)


def _block_diag_rows(q8, n_heads):
    rows = n_heads * ROW_GROUP
    width = q8.shape[1]
    tiled = jnp.concatenate([q8] * n_heads, axis=0)
    rh = lax.shift_right_logical(_iota((rows, width), 0), 3)
    ch = lax.shift_right_logical(_iota((rows, width), 1), 6)
    return jnp.where(rh == ch, tiled, 0.0), rh == ch


def _sb_blocks(qm, kts, vts, c, nrhs):
    z = [_dot(qm, x) for x in kts]
    sp = [_softplus2(x) for x in z]
    cs = [_sb_sums(x, nrhs) for x in sp]
    carry = [c]
    for x in cs:
        carry.append(carry[-1] + x[:, LANES:])
    w = [jnp.exp2((a - b) + (x[:, :LANES] + cc)).astype(BF16) for a, b, x, cc in zip(z, sp, cs, carry)]
    pv = [_dot(x, y, NT) for x, y in zip(w, vts)]
    acc = pv[0]
    for x in pv[1:]:
        acc = acc + x
    return acc, carry[-1]


def _sb_decode_kernel(pt_ref, q_ref, kn_ref, vn_ref, *rest, n_heads, n_steps, n_new, ppb):
    page_refs = rest[:ppb]
    o_ref, acc_ref, c_ref, qbd_ref = rest[ppb:]
    p = pl.program_id(1)
    rows = n_heads * ROW_GROUP
    width = n_heads * HEAD_DIM
    rhs = _sb_rhs()

    @pl.when(p == 0)
    def _():
        qbd, _ = _block_diag_rows(q_ref[...] * LOG2E, n_heads)
        qbd = qbd.astype(BF16)
        qbd_ref[...] = qbd
        pad = jnp.zeros((LANES - ROW_GROUP, width), F32)
        kb = jnp.concatenate([kn_ref[...], pad], axis=0).astype(BF16)
        vb = jnp.concatenate([vn_ref[...], pad], axis=0).astype(BF16)
        r_in = jnp.bitwise_and(_iota((rows, LANES), 0), ROW_GROUP - 1)
        col = _iota((rows, LANES), 1)
        before = (col >= ROW_GROUP - n_new) & (col < ROW_GROUP) & (col < r_in)
        pv, c_new = _sb_block(qbd, kb, vb, before, jnp.zeros((rows, LANES), F32), rhs)
        acc_ref[...] = pv
        c_ref[...] = c_new

    @pl.when(p > 0)
    def _():
        kts = [ref[pl.ds(0, width), :].astype(BF16) for ref in page_refs]
        vts = [ref[pl.ds(width, width), :].astype(BF16) for ref in page_refs]
        pv, c_new = _sb_blocks(qbd_ref[...], kts, vts, c_ref[...], rhs)
        acc_ref[...] += pv
        c_ref[...] = c_new

    @pl.when(p == n_steps - 1)
    def _():
        own = (lax.shift_right_logical(_iota((rows, width), 0), 3)
               == lax.shift_right_logical(_iota((rows, width), 1), 6))
        m = jnp.where(own, acc_ref[...], 0.0).reshape(n_heads, ROW_GROUP, width)
        o_ref[...] = jnp.sum(m, axis=0).astype(o_ref.dtype)


def sb_decode(qkv, cache, page_table, layer, n_new):
    S, n_pages = page_table.shape
    page = cache.shape[3]
    W = cache.shape[2] // 2
    n_heads = W // HEAD_DIM
    assert page == LANES
    rows = n_heads * ROW_GROUP
    pt = page_table.reshape(-1).astype(jnp.int32)
    ppb = _pick(n_pages, (4, 2, 1))
    n_steps = 1 + n_pages // ppb

    def page_map(i):
        def index(s, p, pt_ref):
            logical = n_pages - 1 - (jnp.maximum(p, 1) - 1) * ppb - i
            return (pt_ref[s * n_pages + logical], layer, 0, 0)
        return index

    kern = functools.partial(_sb_decode_kernel, n_heads=n_heads, n_steps=n_steps, n_new=n_new, ppb=ppb)
    return pl.pallas_call(
        kern,
        grid_spec=pltpu.PrefetchScalarGridSpec(
            num_scalar_prefetch=1,
            grid=(S, n_steps),
            in_specs=[pl.BlockSpec((ROW_GROUP, W), lambda s, p, pt_ref: (s, 0)),
                      pl.BlockSpec((ROW_GROUP, W), lambda s, p, pt_ref: (s, 1)),
                      pl.BlockSpec((ROW_GROUP, W), lambda s, p, pt_ref: (s, 2))]
            + [pl.BlockSpec((None, None, 2 * W, page), page_map(i)) for i in range(ppb)],
            out_specs=pl.BlockSpec((ROW_GROUP, W), lambda s, p, pt_ref: (s, 0)),
            scratch_shapes=[pltpu.VMEM((rows, W), F32), pltpu.VMEM((rows, LANES), F32),
                            pltpu.VMEM((rows, W), BF16)]),
        out_shape=jax.ShapeDtypeStruct((S * ROW_GROUP, W), BF16),
        compiler_params=_cparams("parallel", "arbitrary"),
        name="sb_decode",
    )(pt, qkv, qkv, qkv, *([cache] * ppb))


def _nsa_prep_kernel(x_ref, gain_ref, nq_ref, cmp_ref, slc_ref, win_ref):
    seg = _seg_mat(LANES, HEAD_DIM, 1.0 / HEAD_DIM)

    def norm(col, gain_row):
        blk = x_ref[:, pl.ds(col, LANES)]
        ms = _dot_hl(blk * blk, seg)
        return blk * lax.rsqrt(ms + RMS_EPS) * gain_ref[gain_row:gain_row + 1, :]

    for j in range(8):
        nq_ref[:, pl.ds(j * LANES, LANES)] = (norm(j * LANES, 0) * HEAD_DIM ** -0.5).astype(BF16)
    cmp_ref[...] = x_ref[:, pl.ds(1024, 512)]
    for j in range(2):
        slc_ref[:, pl.ds(j * LANES, LANES)] = norm(1536 + j * LANES, 2)
        win_ref[:, pl.ds(j * LANES, LANES)] = norm(2048 + j * LANES, 3)
    slc_ref[:, pl.ds(256, 256)] = x_ref[:, pl.ds(1792, 256)]
    win_ref[:, pl.ds(256, 256)] = x_ref[:, pl.ds(2304, 256)]


def nsa_prep(blk_b, gains):
    M = blk_b.shape[0]
    tm = _pick(M, (512, 256, 128, 64, 32, 16, 8))
    return pl.pallas_call(
        _nsa_prep_kernel,
        grid=(M // tm,),
        in_specs=[pl.BlockSpec((tm, 2560), lambda i: (i, 0)),
                  pl.BlockSpec((SUBLANES, LANES), lambda i: (0, 0))],
        out_specs=[pl.BlockSpec((tm, 1024), lambda i: (i, 0)),
                   pl.BlockSpec((tm, 512), lambda i: (i, 0)),
                   pl.BlockSpec((tm, 512), lambda i: (i, 0)),
                   pl.BlockSpec((tm, 512), lambda i: (i, 0))],
        out_shape=[jax.ShapeDtypeStruct((M, 1024), BF16)] + [jax.ShapeDtypeStruct((M, 512), F32)] * 3,
        compiler_params=_cparams("parallel"),
        name="nsa_prep",
    )(blk_b, gains)


def _compress_kernel(x0_ref, x1_ref, x2_ref, x3_ref, w_ref, o_ref, *, G):
    for p, x_ref in enumerate((x0_ref, x1_ref, x2_ref, x3_ref)):
        acc = jnp.zeros((G * SUBLANES, 2 * LANES), F32)
        for s in range(CMP_STRIDE):
            xs = x_ref[:, pl.ds(s, SUBLANES, stride=CMP_STRIDE), :].reshape(G * SUBLANES, LANES).astype(BF16)
            acc = acc + _dot(xs, w_ref[p // 2, s])
        o_ref[:, pl.ds(p * 2 * LANES, 2 * LANES)] = acc


def compress(pages, layer, w1c):
    n_pages = pages.shape[0]
    G = _pick(n_pages, (16, 8, 5, 4, 3, 2, 1))
    kern = functools.partial(_compress_kernel, G=G)
    return pl.pallas_call(
        kern,
        grid=(n_pages // G,),
        in_specs=[pl.BlockSpec((G, None, LANES, LANES), functools.partial(lambda i, p: (i, layer, 0, p), p=p))
                  for p in range(4)]
        + [pl.BlockSpec((2, CMP_STRIDE, LANES, 2 * LANES), lambda i: (0, 0, 0, 0))],
        out_specs=pl.BlockSpec((G * SUBLANES, 1024), lambda i: (i, 0)),
        out_shape=jax.ShapeDtypeStruct((n_pages * SUBLANES, 1024), F32),
        compiler_params=_cparams("parallel"),
        name="nsa_compress",
    )(pages, pages, pages, pages, w1c)


def _gather_rows_kernel(pt_ref, src_ref, o_ref, sem, *, n_pages):
    s = pl.program_id(0)

    def copy(p):
        return pltpu.make_async_copy(src_ref.at[pl.ds(pt_ref[s * n_pages + p] * SUBLANES, SUBLANES)],
                                     o_ref.at[0, pl.ds(p * SUBLANES, SUBLANES)], sem)

    def start(p, carry):
        copy(p).start()
        return carry

    def wait(p, carry):
        copy(p).wait()
        return carry

    lax.fori_loop(0, n_pages, start, 0)
    lax.fori_loop(0, n_pages, wait, 0)


def gather_page_rows(src, page_table):
    S, n_pages = page_table.shape
    W = src.shape[1]
    kern = functools.partial(_gather_rows_kernel, n_pages=n_pages)
    return pl.pallas_call(
        kern,
        grid_spec=pltpu.PrefetchScalarGridSpec(
            num_scalar_prefetch=1,
            grid=(S,),
            in_specs=[pl.BlockSpec(memory_space=pl.ANY)],
            out_specs=pl.BlockSpec((1, n_pages * SUBLANES, W), lambda s, pt_ref: (s, 0, 0)),
            scratch_shapes=[pltpu.SemaphoreType.DMA(())]),
        out_shape=jax.ShapeDtypeStruct((S, n_pages * SUBLANES, W), F32),
        compiler_params=_cparams("arbitrary"),
        name="gather_page_rows",
    )(page_table.reshape(-1).astype(jnp.int32), src)


def _cmp_finish_kernel(fs_ref, posb_ref, w2_ref, b2_ref, gain_ref, kd_ref, vd_ref, *, nch):
    for p in range(4):
        c, pp = p // 2, p % 2
        first = fs_ref[:, pl.ds(p * 2 * LANES, LANES)]
        second = pltpu.roll(fs_ref[:, pl.ds(p * 2 * LANES + LANES, LANES)], nch - 1, 0)
        hid = jax.nn.gelu(first + second + posb_ref[c:c + 1, :]).astype(BF16)
        for e in range(2):
            out = _dot(hid, w2_ref[c, e]) + b2_ref[c:c + 1, :]
            if c == 0:
                ms = jnp.mean(out * out, axis=-1, keepdims=True)
                kd_ref[pp * 2 + e] = (out * lax.rsqrt(ms + RMS_EPS) * gain_ref[...]).astype(BF16)
            else:
                vd_ref[pp * 2 + e] = out.astype(BF16)


def cmp_finish(fs, posb, w2dup, b2, gain):
    S, nch, _ = fs.shape
    kern = functools.partial(_cmp_finish_kernel, nch=nch)
    small = pl.BlockSpec((SUBLANES, LANES), lambda s: (0, 0))
    return pl.pallas_call(
        kern,
        grid=(S,),
        in_specs=[pl.BlockSpec((None, nch, 1024), lambda s: (s, 0, 0)), small,
                  pl.BlockSpec((2, 2, LANES, LANES), lambda s: (0, 0, 0, 0)), small,
                  pl.BlockSpec((1, LANES), lambda s: (0, 0))],
        out_specs=[pl.BlockSpec((None, 4, nch, LANES), lambda s: (s, 0, 0, 0))] * 2,
        out_shape=[jax.ShapeDtypeStruct((S, 4, nch, LANES), BF16)] * 2,
        compiler_params=_cparams("parallel"),
        name="cmp_finish",
    )(fs, posb, w2dup, b2, gain)


def _head_queries(q, lo):
    out = []
    for g in range(4):
        blk = q[:, (g // 2) * LANES:(g // 2 + 1) * LANES]
        keep = lo if g % 2 == 0 else jnp.logical_not(lo)
        out.append(jnp.where(keep, blk, jnp.zeros_like(blk)))
    return out


def _pair_out(o, lo):
    return jnp.concatenate([jnp.where(lo, o[0], o[1]), jnp.where(lo, o[2], o[3])], axis=1)


def _topk_mask(score, n_sel, k_eff, axis):
    j = _iota(score.shape, axis)
    rank = jnp.zeros(score.shape, jnp.int32)
    for i in range(n_sel):
        si = score[i:i + 1, :] if axis == 0 else score[:, i:i + 1]
        ahead = (si > score) | ((si == score) & (j > i))
        rank = rank + jnp.where(ahead, 1, 0)
    return rank < k_eff


def _cmp_attn_kernel(q_ref, kd_ref, vd_ref, bias_ref, ov_ref, o_ref, sel_ref, *,
                     tq, ncp, n_sel, k_eff, pos0, transposed):
    t0 = pos0 + pl.program_id(2) * tq
    t = t0 + _iota((tq, ncp), 0)
    c = _iota((tq, ncp), 1)
    valid = t - (c * CMP_STRIDE + CMP_BLOCK - 1) >= 0
    lo = _lo_mask()
    kd = kd_ref[...]
    vd = vd_ref[...]
    qms = _head_queries(q_ref[...], lo)
    sc = [jnp.where(valid, _dot(qm, kd, NT) + bias_ref[g], NEG) for g, qm in enumerate(qms)]
    m = [jnp.max(x, axis=1, keepdims=True) for x in sc]
    e = [jnp.where(valid, jnp.exp(x - y), 0.0) for x, y in zip(sc, m)]
    den = [jnp.sum(x, axis=1, keepdims=True) for x in e]
    p = [x / jnp.where(y > 0, y, 1.0) for x, y in zip(e, den)]
    outs = [_dot(x.astype(BF16), vd) for x in p]
    psum = (p[0] + p[1]) + (p[2] + p[3])
    o_ref[...] = _pair_out(outs, lo)
    hi, lw = _split(psum)
    if transposed:
        imp = _dot(ov_ref[...], hi, NT) + _dot(ov_ref[...], lw, NT)
        j = _iota(imp.shape, 0)
        tt = t0 + _iota(imp.shape, 1)
    else:
        imp = _dot(hi, ov_ref[...]) + _dot(lw, ov_ref[...])
        j = _iota(imp.shape, 1)
        tt = t0 + _iota(imp.shape, 0)
    causal = j * SEL_BLOCK <= tt
    cur = lax.shift_right_logical(tt, 6)
    forced = causal & ((j == 0) | (j == cur) | (j == cur - 1))
    score = jnp.where(forced, -NEG, jnp.where(causal, imp, NEG))
    score = jnp.where(j < n_sel, score, 2 * NEG)
    sel = _topk_mask(score, n_sel, k_eff, 0 if transposed else 1)
    sel_ref[...] = jnp.where(sel, 1.0, 0.0)


def cmp_attn(nq, kd, vd, bias, ov, *, S, tq, n_q, n_sel, pos0, transposed):
    ncp = kd.shape[2]
    nsp = ov.shape[0] if transposed else ov.shape[1]
    k_eff = min(N_SEL, n_sel)
    rows = S * n_q * tq
    if transposed:
        sel_spec = pl.BlockSpec((None, None, nsp, tq), lambda b, k, i: (b, k, 0, i))
        sel_shape = jax.ShapeDtypeStruct((S, 4, nsp, n_q * tq), F32)
    else:
        sel_spec = pl.BlockSpec((None, None, tq, nsp), lambda b, k, i: (b, k, i, 0))
        sel_shape = jax.ShapeDtypeStruct((S, 4, n_q * tq, nsp), F32)
    kern = functools.partial(_cmp_attn_kernel, tq=tq, ncp=ncp, n_sel=n_sel, k_eff=k_eff, pos0=pos0,
                             transposed=transposed)
    return pl.pallas_call(
        kern,
        grid=(S, 4, n_q),
        in_specs=[pl.BlockSpec((tq, 2 * LANES), lambda b, k, i: (b * n_q + i, k)),
                  pl.BlockSpec((None, None, ncp, LANES), lambda b, k, i: (b, k, 0, 0)),
                  pl.BlockSpec((None, None, ncp, LANES), lambda b, k, i: (b, k, 0, 0)),
                  pl.BlockSpec((4, tq, ncp), lambda b, k, i: (k, i, 0)),
                  pl.BlockSpec(ov.shape, lambda b, k, i: (0, 0))],
        out_specs=[pl.BlockSpec((tq, 2 * LANES), lambda b, k, i: (b * n_q + i, k)), sel_spec],
        out_shape=[jax.ShapeDtypeStruct((rows, 1024), F32), sel_shape],
        compiler_params=_cparams("parallel", "parallel", "arbitrary"),
        name="cmp_attn",
    )(nq, kd, vd, bias, ov)


def _dup_half(x, odd):
    lo = _lo_mask()
    take_x = jnp.logical_xor(lo, odd)
    return jnp.where(take_x, x, pltpu.roll(x, HEAD_DIM, 1))


def _flash_kernel(qi_ref, kj_ref, dl_ref, first_ref, last_ref, q_ref, k_ref, v_ref, bias_ref, *rest,
                  tq, tk, mode):
    if mode == "slc":
        sel_ref, o_ref, m_ref, l_ref, acc_ref = rest
    else:
        o_ref, m_ref, l_ref, acc_ref = rest
    kvh = pl.program_id(1)
    s = pl.program_id(2)
    qi = qi_ref[s]
    kj = kj_ref[s]

    @pl.when(first_ref[s] == 1)
    def _():
        m_ref[...] = jnp.full_like(m_ref, M_FLOOR)
        l_ref[...] = jnp.zeros_like(l_ref)
        acc_ref[...] = jnp.zeros_like(acc_ref)

    lo = _lo_mask()
    odd = (kvh % 2) == 1
    kd = _dup_half(k_ref[...], odd).astype(BF16)
    vd = _dup_half(v_ref[...], odd).astype(BF16)
    ones = jnp.ones((tk, LANES), BF16)
    if mode == "slc":
        nsp = sel_ref.shape[0]
        blk = kj * (tk // SEL_BLOCK) + lax.shift_right_logical(_iota((nsp, tk), 1), 6)
        expand = jnp.where(_iota((nsp, tk), 0) == blk, 1.0, 0.0).astype(BF16)
    rc = min(tq, LANES)
    units, valids = [], []
    for r0 in range(0, tq, rc):
        rows = pl.ds(r0, rc)
        d = (qi * tq + r0 + _iota((rc, tk), 0)) - (kj * tk + _iota((rc, tk), 1))
        if mode == "slc":
            chosen = _dot(sel_ref[:, rows].astype(BF16), expand, TN)
            valids.append((chosen > 0.5) & (d >= 0))
        else:
            valids.append((d >= 0) & (d < WINDOW))
        for g, qm in enumerate(_head_queries(q_ref[rows, :], lo)):
            units.append((g, rows, qm, len(valids) - 1))
    sc = [_dot(qm, kd, NT) + jnp.where(valids[v], bias_ref[g, rows, :], NEG) for g, rows, qm, v in units]
    m_old = [m_ref[g, rows, :] for g, rows, _, _ in units]
    m_new = [jnp.maximum(mo, jnp.max(x, axis=1, keepdims=True)) for mo, x in zip(m_old, sc)]
    p = [jnp.exp(x - mn).astype(BF16) for x, mn in zip(sc, m_new)]
    alpha = [jnp.exp(mo - mn) for mo, mn in zip(m_old, m_new)]
    rowsum = [_dot(x, ones) for x in p]
    pv = [_dot(x, vd) for x in p]
    for (g, rows, _, _), a, rs, o, mn in zip(units, alpha, rowsum, pv, m_new):
        l_ref[g, rows, :] = a * l_ref[g, rows, :] + rs
        acc_ref[g, rows, :] = a * acc_ref[g, rows, :] + o
        m_ref[g, rows, :] = mn

    @pl.when(last_ref[s] == 1)
    def _():
        outs = []
        for g in range(4):
            den = l_ref[g]
            outs.append(acc_ref[g] / jnp.where(den > 0, den, 1.0))
        o_ref[...] = _pair_out(outs, lo)


def flash_prompt(nq, kv, bias_tiles, sel, *, B, T, mode):
    tq = tk = bias_tiles.shape[2]
    nq_t = T // tq
    nd = bias_tiles.shape[1]
    lookback = None if mode == "slc" else -(-(WINDOW - 1) // tk)
    qs, ks, first, last = _tri_steps(nq_t, lookback)
    dl = jnp.minimum(qs - ks, nd - 1)
    n_steps = int(qs.shape[0])
    in_specs = [pl.BlockSpec((tq, 2 * LANES), lambda b, k, s, qi, kj, dl, f, l: (b * nq_t + qi[s], k)),
                pl.BlockSpec((tk, LANES), lambda b, k, s, qi, kj, dl, f, l: (b * nq_t + kj[s], k // 2)),
                pl.BlockSpec((tk, LANES), lambda b, k, s, qi, kj, dl, f, l: (b * nq_t + kj[s], 2 + k // 2)),
                pl.BlockSpec((4, None, tq, tk), lambda b, k, s, qi, kj, dl, f, l: (k, dl[s], 0, 0))]
    args = [nq, kv, kv, bias_tiles]
    if mode == "slc":
        nsp = sel.shape[2]
        in_specs.append(pl.BlockSpec((None, None, nsp, tq), lambda b, k, s, qi, kj, dl, f, l: (b, k, 0, qi[s])))
        args.append(sel)
    kern = functools.partial(_flash_kernel, tq=tq, tk=tk, mode=mode)
    return pl.pallas_call(
        kern,
        grid_spec=pltpu.PrefetchScalarGridSpec(
            num_scalar_prefetch=5,
            grid=(B, 4, n_steps),
            in_specs=in_specs,
            out_specs=pl.BlockSpec((tq, 2 * LANES), lambda b, k, s, qi, kj, dl, f, l: (b * nq_t + qi[s], k)),
            scratch_shapes=[pltpu.VMEM((4, tq, 1), F32), pltpu.VMEM((4, tq, LANES), F32),
                            pltpu.VMEM((4, tq, LANES), F32)]),
        out_shape=jax.ShapeDtypeStruct((B * T, 1024), F32),
        compiler_params=_cparams("parallel", "parallel", "arbitrary"),
        name="nsa_" + mode,
    )(qs, ks, dl, first, last, *args)


def _kv_queries(q8, n_heads, n_kv):
    qbd, _ = _block_diag_rows(q8, n_heads)
    wq, wk = n_heads * HEAD_DIM, n_kv * HEAD_DIM
    gshift = int(math.log2(n_heads // n_kv)) + 6
    r = _iota((wq, wk), 0)
    c = _iota((wq, wk), 1)
    fold = jnp.where((lax.shift_right_logical(r, gshift) == lax.shift_right_logical(c, 6))
                     & (jnp.bitwise_and(r, 63) == jnp.bitwise_and(c, 63)), 1.0, 0.0).astype(BF16)
    return _dot(qbd.astype(BF16), fold).astype(BF16)


def _kv_outputs(o, n_heads, n_kv):
    rows = n_heads * ROW_GROUP
    wq, wk = n_heads * HEAD_DIM, n_kv * HEAD_DIM
    gshift = int(math.log2(n_heads // n_kv))
    own_kv = lax.shift_right_logical(_iota((rows, wk), 0), 3 + gshift) == lax.shift_right_logical(_iota((rows, wk), 1), 6)
    r = _iota((wk, wq), 0)
    c = _iota((wk, wq), 1)
    unfold = jnp.where((lax.shift_right_logical(c, gshift + 6) == lax.shift_right_logical(r, 6))
                       & (jnp.bitwise_and(r, 63) == jnp.bitwise_and(c, 63)), 1.0, 0.0).astype(BF16)
    wide = _dot_hl(jnp.where(own_kv, o, 0.0), unfold)
    own = lax.shift_right_logical(_iota((rows, wq), 0), 3) == lax.shift_right_logical(_iota((rows, wq), 1), 6)
    return jnp.sum(jnp.where(own, wide, 0.0).reshape(n_heads, ROW_GROUP, wq), axis=0)


def _softmax_step(sc, vb, m_ref, l_ref, acc_ref, v_dims=NN):
    m_old = m_ref[...]
    m_new = jnp.maximum(m_old, jnp.max(sc, axis=1, keepdims=True))
    p = jnp.exp(sc - m_new).astype(BF16)
    alpha = jnp.exp(m_old - m_new)
    l_ref[...] = alpha * l_ref[...] + _dot(p, jnp.ones((sc.shape[1], LANES), BF16))
    acc_ref[...] = alpha * acc_ref[...] + _dot(p, vb, v_dims)
    m_ref[...] = m_new


def _new_rows(ref, cols):
    blk = ref[:, cols]
    return jnp.concatenate([blk, jnp.zeros((LANES - ROW_GROUP, blk.shape[1]), F32)], axis=0).astype(BF16)


def _slc_decode_kernel(pt_ref, q_ref, kvn_ref, sel_ref, bias_ref, *rest, n_heads, n_kv, n_pages, n_new, ppb):
    page_refs = rest[:ppb]
    o_ref, m_ref, l_ref, acc_ref, qbd_ref = rest[ppb:]
    p = pl.program_id(1)
    rows = n_heads * ROW_GROUP
    wk = n_kv * HEAD_DIM
    group = n_heads // n_kv
    nsp = sel_ref.shape[2]
    sel_rows = jnp.concatenate([sel_ref[k] for k in range(n_kv) for _ in range(group)], axis=0).astype(BF16)

    def chosen(first_blk, n_keys):
        blk = first_blk + lax.shift_right_logical(_iota((nsp, n_keys), 1), 6)
        expand = jnp.where(_iota((nsp, n_keys), 0) == blk, 1.0, 0.0).astype(BF16)
        return _dot(sel_rows, expand) > 0.5

    @pl.when(p == 0)
    def _():
        m_ref[...] = jnp.full_like(m_ref, M_FLOOR)
        l_ref[...] = jnp.zeros_like(l_ref)
        acc_ref[...] = jnp.zeros_like(acc_ref)
        qbd = _kv_queries(q_ref[...].astype(F32), n_heads, n_kv)
        qbd_ref[...] = qbd
        r_in = jnp.bitwise_and(_iota((rows, LANES), 0), ROW_GROUP - 1)
        col = _iota((rows, LANES), 1)
        kb = _new_rows(kvn_ref, slice(0, wk))
        vb = _new_rows(kvn_ref, slice(wk, 2 * wk))
        valid = (chosen(n_pages * (LANES // SEL_BLOCK), LANES)
                 & (col >= ROW_GROUP - n_new) & (col < ROW_GROUP) & (col <= r_in))
        sc = _dot(qbd, kb, NT) + jnp.where(valid, bias_ref[:, pl.ds(0, LANES)], NEG)
        _softmax_step(sc, vb, m_ref, l_ref, acc_ref)

    @pl.when(p > 0)
    def _():
        kt = jnp.concatenate([ref[pl.ds(0, wk), :] for ref in page_refs], axis=1).astype(BF16)
        vt = jnp.concatenate([ref[pl.ds(wk, wk), :] for ref in page_refs], axis=1).astype(BF16)
        base = n_pages - p * ppb
        valid = chosen(base * (LANES // SEL_BLOCK), ppb * LANES)
        sc = _dot(qbd_ref[...], kt) + jnp.where(valid, bias_ref[...], NEG)
        _softmax_step(sc, vt, m_ref, l_ref, acc_ref, NT)

    @pl.when(p == n_pages // ppb)
    def _():
        den = l_ref[:, pl.ds(0, 1)]
        o_ref[...] = _kv_outputs(acc_ref[...] / jnp.where(den > 0, den, 1.0), n_heads, n_kv)


def slc_decode(nq, kv_new, sel, bias, cache, page_table, layer, n_new):
    S, n_pages = page_table.shape
    n_kv = cache.shape[2] // (2 * HEAD_DIM)
    n_heads = nq.shape[1] // HEAD_DIM
    rows = n_heads * ROW_GROUP
    nsp = sel.shape[3]
    pt = page_table.reshape(-1).astype(jnp.int32)
    ppb = (bias.shape[1] - n_pages * LANES) // LANES
    assert n_pages % ppb == 0
    n_steps = 1 + n_pages // ppb

    def page_map(i):
        def index(s, p, pt_ref):
            logical = n_pages - jnp.maximum(p, 1) * ppb + i
            return (pt_ref[s * n_pages + logical], layer, 0, 0)
        return index

    kern = functools.partial(_slc_decode_kernel, n_heads=n_heads, n_kv=n_kv, n_pages=n_pages, n_new=n_new,
                             ppb=ppb)
    return pl.pallas_call(
        kern,
        grid_spec=pltpu.PrefetchScalarGridSpec(
            num_scalar_prefetch=1,
            grid=(S, n_steps),
            in_specs=[pl.BlockSpec((ROW_GROUP, nq.shape[1]), lambda s, p, pt_ref: (s, 0)),
                      pl.BlockSpec((ROW_GROUP, kv_new.shape[1]), lambda s, p, pt_ref: (s, 0)),
                      pl.BlockSpec((None, n_kv, ROW_GROUP, nsp), lambda s, p, pt_ref: (s, 0, 0, 0)),
                      pl.BlockSpec((rows, ppb * LANES),
                                   lambda s, p, pt_ref: (0, jnp.where(p == 0, n_pages // ppb, n_pages // ppb - p)))]
            + [pl.BlockSpec((None, None, 2 * n_kv * HEAD_DIM, LANES), page_map(i)) for i in range(ppb)],
            out_specs=pl.BlockSpec((ROW_GROUP, nq.shape[1]), lambda s, p, pt_ref: (s, 0)),
            scratch_shapes=[pltpu.VMEM((rows, 1), F32), pltpu.VMEM((rows, LANES), F32),
                            pltpu.VMEM((rows, n_kv * HEAD_DIM), F32),
                            pltpu.VMEM((rows, n_kv * HEAD_DIM), BF16)]),
        out_shape=jax.ShapeDtypeStruct((S * ROW_GROUP, nq.shape[1]), F32),
        compiler_params=_cparams("parallel", "arbitrary"),
        name="slc_decode",
    )(pt, nq, kv_new, sel, bias, *([cache] * ppb))


def _win_decode_kernel(q_ref, kvn_ref, bias_ref, kv_ref, o_ref, m_ref, l_ref, acc_ref, *,
                       n_heads, n_kv, n_new, layer_rows):
    rows = n_heads * ROW_GROUP
    wk = n_kv * HEAD_DIM
    m_ref[...] = jnp.full_like(m_ref, M_FLOOR)
    l_ref[...] = jnp.zeros_like(l_ref)
    acc_ref[...] = jnp.zeros_like(acc_ref)
    qbd = _kv_queries(q_ref[...].astype(F32), n_heads, n_kv)
    i_q = jnp.bitwise_and(_iota((rows, layer_rows), 0), ROW_GROUP - 1) - (ROW_GROUP - n_new)
    j = _iota((rows, layer_rows), 1)
    dist = layer_rows + i_q - j
    valid = (dist >= 0) & (dist < WINDOW)
    kb = kv_ref[:, pl.ds(0, wk)].astype(BF16)
    vb = kv_ref[:, pl.ds(wk, wk)].astype(BF16)
    sc = _dot(qbd, kb, NT) + jnp.where(valid, bias_ref[:, pl.ds(0, layer_rows)], NEG)
    _softmax_step(sc, vb, m_ref, l_ref, acc_ref)
    r_in = jnp.bitwise_and(_iota((rows, LANES), 0), ROW_GROUP - 1)
    col = _iota((rows, LANES), 1)
    valid = (col >= ROW_GROUP - n_new) & (col < ROW_GROUP) & (col <= r_in)
    kb = _new_rows(kvn_ref, slice(0, wk))
    vb = _new_rows(kvn_ref, slice(wk, 2 * wk))
    sc = _dot(qbd, kb, NT) + jnp.where(valid, bias_ref[:, pl.ds(layer_rows, LANES)], NEG)
    _softmax_step(sc, vb, m_ref, l_ref, acc_ref)
    den = l_ref[:, pl.ds(0, 1)]
    o_ref[...] = _kv_outputs(acc_ref[...] / jnp.where(den > 0, den, 1.0), n_heads, n_kv)


def win_decode(nq, kv_new, bias, cache_win, layer, n_new):
    S, _, wc, width = cache_win.shape
    n_kv = width // (2 * HEAD_DIM)
    n_heads = nq.shape[1] // HEAD_DIM
    rows = n_heads * ROW_GROUP
    kern = functools.partial(_win_decode_kernel, n_heads=n_heads, n_kv=n_kv, n_new=n_new, layer_rows=wc)
    return pl.pallas_call(
        kern,
        grid=(S,),
        in_specs=[pl.BlockSpec((ROW_GROUP, nq.shape[1]), lambda s: (s, 0)),
                  pl.BlockSpec((ROW_GROUP, width), lambda s: (s, 0)),
                  pl.BlockSpec((rows, wc + LANES), lambda s: (0, 0)),
                  pl.BlockSpec((None, None, wc, width), lambda s: (s, layer, 0, 0))],
        out_specs=pl.BlockSpec((ROW_GROUP, nq.shape[1]), lambda s: (s, 0)),
        out_shape=jax.ShapeDtypeStruct((S * ROW_GROUP, nq.shape[1]), F32),
        scratch_shapes=[pltpu.VMEM((rows, 1), F32), pltpu.VMEM((rows, LANES), F32),
                        pltpu.VMEM((rows, n_kv * HEAD_DIM), F32)],
        compiler_params=_cparams("parallel"),
        name="win_decode",
    )(nq, kv_new, bias, cache_win)


def _nsa_combine_kernel(oc_ref, os_ref, ow_ref, gate_ref, o_ref, *, lane0, n_heads):
    hi, lw = _split(jax.nn.sigmoid(gate_ref[...]))
    width = n_heads * HEAD_DIM
    acc = jnp.zeros(oc_ref.shape, F32)
    for br, ref in enumerate((oc_ref, os_ref, ow_ref)):
        src = lane0 + br * n_heads + lax.shift_right_logical(_iota((LANES, width), 1), 6)
        expand = jnp.where(_iota((LANES, width), 0) == src, 1.0, 0.0).astype(BF16)
        acc = acc + (_dot(hi, expand) + _dot(lw, expand)) * ref[...]
    o_ref[...] = acc.astype(o_ref.dtype)


def nsa_combine(o_cmp, o_slc, o_win, blk_c, *, gate_block, lane0):
    M, width = o_cmp.shape
    tm = _pick(M, (512, 256, 128, 64, 32, 16, 8))
    kern = functools.partial(_nsa_combine_kernel, lane0=lane0, n_heads=width // HEAD_DIM)
    ospec = pl.BlockSpec((tm, width), lambda i: (i, 0))
    return pl.pallas_call(
        kern,
        grid=(M // tm,),
        in_specs=[ospec, ospec, ospec, pl.BlockSpec((tm, LANES), lambda i: (i, gate_block))],
        out_specs=ospec,
        out_shape=jax.ShapeDtypeStruct((M, width), BF16),
        compiler_params=_cparams("parallel"),
        name="nsa_combine",
    )(o_cmp, o_slc, o_win, blk_c)


RW_COLS = 3584


def _rwkv_prep_kernel(c_ref, head_ref, mu_ref, vec_ref, w2_ref, a2_ref, g2_ref,
                      r_ref, lw_ref, k_ref, v_ref, kk_ref, b_ref, bonus_ref, g_ref,
                      xs_ref, carry_ref, *, tm, tps, whole, n_new):
    i = pl.program_id(0)
    cols = c_ref[...]
    if whole:
        cols = cols + head_ref[...]
        head = jnp.zeros((SUBLANES, RW_COLS), F32)
    else:
        head = jnp.where(i % tps == 0, head_ref[0], carry_ref[...])
    xs_ref[pl.ds(0, SUBLANES), :] = head
    xs_ref[pl.ds(SUBLANES, tm), :] = cols
    prev = xs_ref[pl.ds(SUBLANES - 1, tm), :]
    if not whole:
        carry_ref[...] = xs_ref[pl.ds(tm, SUBLANES), :]
    xs_ref[pl.ds(SUBLANES, tm), :] = cols + (prev - cols) * mu_ref[...]
    small = xs_ref[pl.ds(SUBLANES, tm), pl.ds(3328, 256)]
    th = jnp.tanh(small).astype(BF16)
    sm = small.astype(BF16)
    sg = jax.nn.sigmoid(xs_ref[pl.ds(SUBLANES, tm), pl.ds(3072, 256)]).astype(BF16)
    seg = _seg_mat(LANES, HEAD_DIM, 1.0)
    if whole:
        real = jnp.bitwise_and(_iota((tm, LANES), 0), ROW_GROUP - 1) >= ROW_GROUP - n_new
    for j in range(8):
        cs = pl.ds(j * LANES, LANES)
        r = xs_ref[pl.ds(SUBLANES, tm), pl.ds(j * LANES, LANES)]
        k = xs_ref[pl.ds(SUBLANES, tm), pl.ds(1024 + j * LANES, LANES)]
        v = xs_ref[pl.ds(SUBLANES, tm), pl.ds(2048 + j * LANES, LANES)]
        y = vec_ref[0:1, cs] + _dot(th, w2_ref[:, cs])
        w_log = -(jnp.maximum(-y, 0.0) + jnp.log(1.0 + jnp.exp(-jnp.abs(y)))) - 0.5
        lw = -jnp.exp(w_log)
        a = jax.nn.sigmoid(vec_ref[1:2, cs] + _dot(sm, a2_ref[:, cs]))
        g = _dot(sg, g2_ref[:, cs])
        kk = k * vec_ref[2:3, cs]
        kk = kk * lax.rsqrt(jnp.maximum(_dot_hl(kk * kk, seg), 1e-24))
        k2 = k * (1.0 + (a - 1.0) * vec_ref[3:4, cs])
        b = kk * a
        bonus = _dot_hl(r * k2 * vec_ref[4:5, cs], seg) * v
        if whole:
            r, k2, v, kk, b, lw = [jnp.where(real, t, 0.0) for t in (r, k2, v, kk, b, lw)]
        r_ref[:, cs] = r
        lw_ref[:, cs] = lw
        k_ref[:, cs] = k2
        v_ref[:, cs] = v
        kk_ref[:, cs] = kk
        b_ref[:, cs] = b
        bonus_ref[:, cs] = bonus
        g_ref[:, cs] = g


def rwkv_prep(blk_c, head, mu, vec, w2p, a2p, g2, *, seq_rows, whole, n_new=0):
    M = blk_c.shape[0]
    if whole:
        tm, tps = M, 1
        head_spec = pl.BlockSpec((tm, RW_COLS), lambda i: (0, 0))
    else:
        tm = _pick(seq_rows, (256, 128, 64))
        tps = seq_rows // tm
        head_spec = pl.BlockSpec((1, SUBLANES, RW_COLS), lambda i: (i // tps, 0, 0))
    kern = functools.partial(_rwkv_prep_kernel, tm=tm, tps=tps, whole=whole, n_new=n_new)
    full = lambda shape: pl.BlockSpec(shape, lambda i: (0,) * len(shape))
    return pl.pallas_call(
        kern,
        grid=(M // tm,),
        in_specs=[pl.BlockSpec((tm, RW_COLS), lambda i: (i, 0)), head_spec, full((1, RW_COLS)),
                  full((SUBLANES, 1024)), full((256, 1024)), full((256, 1024)), full((256, 1024))],
        out_specs=[pl.BlockSpec((tm, 1024), lambda i: (i, 0))] * 8,
        out_shape=[jax.ShapeDtypeStruct((M, 1024), F32)] * 8,
        scratch_shapes=[pltpu.VMEM((tm + SUBLANES, RW_COLS), F32), pltpu.VMEM((SUBLANES, RW_COLS), F32)],
        compiler_params=_cparams("arbitrary"),
        name="rwkv_prep",
    )(blk_c, head, mu, vec, w2p, a2p, g2)


def _rwkv_par_kernel(r_ref, lw_ref, k_ref, v_ref, kk_ref, b_ref, r2_ref, y2_ref, m_ref, g_ref, *, C, npair):
    C2 = 2 * C
    cum = jnp.where(_iota((C, C), 0) >= _iota((C, C), 1), 1.0, 0.0).astype(BF16)
    keep = (_iota((C2, LANES), 0) < C) == (_iota((C2, LANES), 1) < HEAD_DIM)
    rb = _iota((C2, C2), 0)
    cb = _iota((C2, C2), 1)
    same = (rb < C) == (cb < C)
    rr = jnp.bitwise_and(rb, C - 1)
    cc = jnp.bitwise_and(cb, C - 1)
    strict = same & (rr > cc)
    incl = same & (rr >= cc)
    eye2 = jnp.where(rb == cb, 1.0, 0.0)
    eye_l = _iota((LANES, LANES), 0) == _iota((LANES, LANES), 1)

    def stack(x):
        return jnp.where(keep, jnp.concatenate([x, x], axis=0), 0.0)

    group = 8
    for p0 in range(0, npair, group):
        ps = range(p0, min(p0 + group, npair))
        cols = [pl.ds(p * LANES, LANES) for p in ps]
        n = len(cols)
        lw = [lw_ref[:, cs] for cs in cols]
        log_p = [_dot_hl_rhs(cum, x) for x in lw]
        log_end = [x[C - 1:C, :] for x in log_p]
        e_pos = [jnp.exp(x) for x in log_p]
        e_neg = [jnp.exp(-x) for x in log_p]
        e_end = [jnp.exp(le - x) for le, x in zip(log_end, log_p)]
        k = [k_ref[:, cs] for cs in cols]
        b = [b_ref[:, cs] for cs in cols]
        rt = [stack(r_ref[:, cs] * e) for cs, e in zip(cols, e_pos)]
        kt = [stack(x * e) for x, e in zip(k, e_neg)]
        bt = [stack(x * e) for x, e in zip(b, e_neg)]
        at = [stack(-kk_ref[:, cs] * jnp.exp(x - y)) for cs, x, y in zip(cols, log_p, lw)]
        kendb = [stack(x * e).astype(BF16) for x, e in zip(k, e_end)]
        bendb = [stack(x * e).astype(BF16) for x, e in zip(b, e_end)]
        v2b = [stack(v_ref[:, cs]).astype(BF16) for cs in cols]
        atb = [x.astype(BF16) for x in at]
        cross = [_dot(jnp.concatenate([atb[i], rt[i].astype(BF16)], axis=0),
                      jnp.concatenate([kt[i], bt[i]], axis=0).astype(BF16), NT) for i in range(n)]
        a_ak = [jnp.where(strict, x[:C2, :C2], 0.0).astype(BF16) for x in cross]
        a_ab = [jnp.where(strict, x[:C2, C2:], 0.0) for x in cross]
        a_rk = [jnp.where(incl, x[C2:, :C2], 0.0).astype(BF16) for x in cross]
        a_rb = [jnp.where(incl, x[C2:, C2:], 0.0).astype(BF16) for x in cross]
        akv = [_dot(x, y).astype(BF16) for x, y in zip(a_ak, v2b)]
        inv = [eye2 + x for x in a_ab]
        powr = [x.astype(BF16) for x in a_ab]
        for _ in range(int(math.log2(C)) - 1):
            powr = [_dot(x, x).astype(BF16) for x in powr]
            inv = [x + _dot(x.astype(BF16), y) for x, y in zip(inv, powr)]
        aub = [_dot(x.astype(BF16), jnp.concatenate([y, z], axis=1)).astype(BF16)
               for x, y, z in zip(inv, atb, akv)]
        ry = [_dot(x, y) for x, y in zip(a_rb, aub)]
        rkv = [_dot(x, y) for x, y in zip(a_rk, v2b)]
        mg = [_dot(x, y, TN) for x, y in zip(aub, bendb)]
        vk = [_dot(x, y, TN) for x, y in zip(v2b, kendb)]
        for i, cs in enumerate(cols):
            r2_ref[:, cs] = rt[i] + ry[i][:, :LANES]
            y2_ref[:, cs] = rkv[i] + ry[i][:, LANES:]
            decay_end = jnp.where(eye_l, jnp.broadcast_to(jnp.exp(log_end[i]), (LANES, LANES)), 0.0)
            m_ref[:, cs] = decay_end + mg[i][:LANES]
            g_ref[:, cs] = vk[i] + mg[i][LANES:]


def rwkv_par(r, lw, k, v, kk, b, C):
    M, W = r.shape
    npair = W // LANES
    nch = M // C
    kern = functools.partial(_rwkv_par_kernel, C=C, npair=npair)
    ispec = pl.BlockSpec((C, W), lambda i: (i, 0))
    return pl.pallas_call(
        kern,
        grid=(nch,),
        in_specs=[ispec] * 6,
        out_specs=[pl.BlockSpec((None, 2 * C, W), lambda i: (i, 0, 0))] * 2
        + [pl.BlockSpec((None, LANES, W), lambda i: (i, 0, 0))] * 2,
        out_shape=[jax.ShapeDtypeStruct((nch, 2 * C, W), F32)] * 2
        + [jax.ShapeDtypeStruct((nch, LANES, W), F32)] * 2,
        compiler_params=_cparams("parallel"),
        name="rwkv_par",
    )(r, lw, k, v, kk, b)


def _rwkv_seq_kernel(r2_ref, y2_ref, m_ref, g_ref, s0_ref, bonus_ref, gate_ref, ln_ref,
                     o_ref, sf_ref, s_ref, *, C, npair):
    c = pl.program_id(1)

    @pl.when(c == 0)
    def _():
        s_ref[...] = s0_ref[...]

    seg = _seg_mat(LANES, HEAD_DIM, 1.0 / HEAD_DIM)
    cols = [pl.ds(p * LANES, LANES) for p in range(npair)]
    st = [s_ref[:, cs] for cs in cols]
    y2 = [_dot3(r2_ref[:, cs], x, NT) + y2_ref[:, cs] for cs, x in zip(cols, st)]
    s_new = [_dot3(x, m_ref[:, cs]) + g_ref[:, cs] for cs, x in zip(cols, st)]
    for cs, x in zip(cols, s_new):
        s_ref[:, cs] = x
    y = [x[:C] + x[C:] for x in y2]
    mu = [_dot_hl(x, seg) for x in y]
    dev = [x - m for x, m in zip(y, mu)]
    var = [_dot_hl(x * x, seg) for x in dev]
    for cs, d, v in zip(cols, dev, var):
        yn = d * lax.rsqrt(v + GN_EPS) * ln_ref[0:1, cs] + ln_ref[1:2, cs]
        o_ref[:, cs] = ((yn + bonus_ref[:, cs]) * gate_ref[:, cs]).astype(o_ref.dtype)

    @pl.when(c == pl.num_programs(1) - 1)
    def _():
        sf_ref[...] = s_ref[...]


def rwkv_seq(r2, y2, mt, gt, s0, bonus, gate, ln, *, n_seq):
    nch_total, C2, W = r2.shape
    C = C2 // 2
    nch = nch_total // n_seq
    npair = W // LANES
    kern = functools.partial(_rwkv_seq_kernel, C=C, npair=npair)
    cspec = lambda rows: pl.BlockSpec((None, rows, W), lambda s, c: (s * nch + c, 0, 0))
    return pl.pallas_call(
        kern,
        grid=(n_seq, nch),
        in_specs=[cspec(C2), cspec(C2), cspec(LANES), cspec(LANES),
                  pl.BlockSpec((None, LANES, W), lambda s, c: (s, 0, 0)),
                  pl.BlockSpec((C, W), lambda s, c: (s * nch + c, 0)),
                  pl.BlockSpec((C, W), lambda s, c: (s * nch + c, 0)),
                  pl.BlockSpec((SUBLANES, W), lambda s, c: (0, 0))],
        out_specs=[pl.BlockSpec((C, W), lambda s, c: (s * nch + c, 0)),
                   pl.BlockSpec((None, LANES, W), lambda s, c: (s, 0, 0))],
        out_shape=[jax.ShapeDtypeStruct((nch_total * C, W), BF16),
                   jax.ShapeDtypeStruct((n_seq, LANES, W), F32)],
        scratch_shapes=[pltpu.VMEM((LANES, W), F32)],
        compiler_params=_cparams("parallel", "arbitrary"),
        name="rwkv_seq",
    )(r2, y2, mt, gt, s0, bonus, gate, ln)


def _bucket_np(d):
    d = np.maximum(d, 0)
    ratio = np.log(np.maximum(d, 1).astype(np.float32) / np.float32(N_BUCKETS // 2)) / np.float32(
        math.log(MAX_DISTANCE / (N_BUCKETS // 2)))
    large = np.minimum(N_BUCKETS // 2 + (ratio * np.float32(N_BUCKETS - N_BUCKETS // 2)).astype(np.int32),
                       N_BUCKETS - 1)
    return np.where(d < N_BUCKETS // 2, d, large).astype(np.int32)


def _overlap_np(n_cmp, n_sel):
    c0 = np.arange(n_cmp)[:, None] * CMP_STRIDE
    j0 = np.arange(n_sel)[None, :] * SEL_BLOCK
    ov = np.clip(np.minimum(c0 + CMP_BLOCK, j0 + SEL_BLOCK) - np.maximum(c0, j0), 0, None)
    return (ov / CMP_BLOCK).astype(np.float32)


def _bias_table_kernel(tab_ref, idx_ref, o_ref):
    h = pl.program_id(0)
    idx = idx_ref[...]
    acc = jnp.zeros(idx.shape, F32)
    for b in range(N_BUCKETS):
        acc = jnp.where(idx == b, tab_ref[h, b], acc)
    o_ref[...] = acc


def bias_table(tab_h, idx):
    H = tab_h.shape[0]
    R, C = idx.shape
    tr = _pick(R, (256, 128, 64, 32, 16, 8))
    return pl.pallas_call(
        _bias_table_kernel,
        grid=(H, R // tr),
        in_specs=[pl.BlockSpec(memory_space=pltpu.SMEM),
                  pl.BlockSpec((tr, C), lambda h, i: (i, 0))],
        out_specs=pl.BlockSpec((None, tr, C), lambda h, i: (h, i, 0)),
        out_shape=jax.ShapeDtypeStruct((H, R, C), F32),
        compiler_params=_cparams("parallel", "parallel"),
        name="bias_table",
    )(tab_h, idx)


def _make_tables(rel_bias, T, past, n_new, n_heads):
    tab_h = rel_bias.astype(F32).T

    def take(idx):
        idx = np.asarray(idx, np.int32)
        out = bias_table(tab_h, jnp.asarray(idx.reshape(-1, idx.shape[-1])))
        return out.reshape((tab_h.shape[0],) + idx.shape)

    t = {}
    tq = _pick(T, (256, 128))
    nq_t = T // tq
    far = 0
    while _bucket_np(np.array([far]))[0] < N_BUCKETS - 1:
        far += 1
    nd = min(-(-(far + tq - 1) // tq) + 1, nq_t)
    i = np.arange(tq)[:, None]
    j = np.arange(tq)[None, :]
    t["tiles"] = take(np.stack([_bucket_np(dl * tq + i - j) for dl in range(nd)]))
    ncp = T // CMP_STRIDE
    tt = np.arange(T)[:, None]
    cc = np.arange(ncp)[None, :]
    t["cmp_p"] = take(_bucket_np(tt - (cc * CMP_STRIDE + CMP_BLOCK - 1)))
    n_sel = -(-T // SEL_BLOCK)
    nsp = -(-n_sel // SUBLANES) * SUBLANES
    ov = np.zeros((nsp, ncp), np.float32)
    ov[:n_sel, :ncp - 1] = _overlap_np(ncp - 1, n_sel).T
    t["ov_p"] = jnp.asarray(ov, BF16)
    t["n_sel_p"] = n_sel
    t["tq"] = tq
    rows = n_heads * ROW_GROUP
    tpos = past - (ROW_GROUP - n_new) + np.arange(ROW_GROUP)
    ppb = _pick(past // LANES, (4, 2, 1))
    kpos = np.concatenate([np.arange(past), past - (ROW_GROUP - n_new) + np.arange(LANES),
                           np.zeros((ppb - 1) * LANES, np.int64)])
    idx = _bucket_np(tpos[:, None] - kpos[None, :])
    t["slc_d"] = take(idx).reshape(rows, past + ppb * LANES)
    wc = min(WINDOW, past)
    kpos = np.concatenate([past - wc + np.arange(wc), past - (ROW_GROUP - n_new) + np.arange(LANES)])
    t["win_d"] = take(_bucket_np(tpos[:, None] - kpos[None, :])).reshape(rows, wc + LANES)
    ncp_d = past // CMP_STRIDE
    cc = np.arange(ncp_d)[None, :]
    t["cmp_d"] = take(_bucket_np(tpos[:, None] - (cc * CMP_STRIDE + CMP_BLOCK - 1)))
    n_sel_d = -(-(past + n_new) // SEL_BLOCK)
    nsp_d = -(-n_sel_d // LANES) * LANES
    n_cmp_d = (past + n_new) // CMP_STRIDE - 1
    ov = np.zeros((ncp_d, nsp_d), np.float32)
    ov[:n_cmp_d, :n_sel_d] = _overlap_np(n_cmp_d, n_sel_d)
    t["ov_d"] = jnp.asarray(ov, BF16)
    t["n_sel_d"] = n_sel_d
    return t


def _prep_layer(l, P):
    W = {}
    w_in = P["w_in"][l]
    sbw = 1024
    W["wA"] = jnp.concatenate([w_in[:, :sbw] * HEAD_DIM ** -0.5, w_in[:, sbw:3 * sbw]], axis=1).astype(BF16)
    W["wB"] = w_in[:, 3072:5632].astype(BF16)
    D = w_in.shape[0]
    W["wC"] = jnp.concatenate([w_in[:, 5680:8752], w_in[:, 8944:9200], w_in[:, 8752:8944],
                               w_in[:, 5632:5680], jnp.zeros((D, 16), F32)], axis=1).astype(BF16)
    W["wD"] = w_in[:, 9200:].astype(BF16)
    W["norm_mix"] = P["norm_mix_g"][l]
    W["norm_ffn"] = P["norm_ffn_g"][l]
    gains = P["nsa_qk_gain"][l]
    W["gains"] = jnp.pad(jnp.tile(gains, (1, 2)), ((0, 4), (0, 0)))
    W["gain1"] = jnp.tile(gains[1:2], (1, 2))
    w1 = P["cmp_w1"][l].reshape(2, 2, CMP_STRIDE, HEAD_DIM, HEAD_DIM)
    first, second = w1[:, 0], w1[:, 1]
    z = jnp.zeros_like(first)
    W["w1c"] = jnp.concatenate([jnp.concatenate([first, z, second, z], axis=-1),
                                jnp.concatenate([z, first, z, second], axis=-1)], axis=-2).astype(BF16)
    pos = jnp.einsum("cld,clde->ce", P["cmp_pe"][l], P["cmp_w1"][l], precision=lax.Precision.HIGHEST)
    W["posb"] = jnp.pad(jnp.tile(pos + P["cmp_b1"][l], (1, 2)), ((0, 6), (0, 0)))
    w2 = P["cmp_w2"][l]
    w2d = jnp.concatenate([w2, w2], axis=-1)
    z2 = jnp.zeros_like(w2d)
    W["w2dup"] = jnp.stack([jnp.concatenate([w2d, z2], axis=1), jnp.concatenate([z2, w2d], axis=1)],
                           axis=1).astype(BF16)
    W["b2"] = jnp.pad(jnp.tile(P["cmp_b2"][l], (1, 2)), ((0, 6), (0, 0)))
    mu = P["rwkv_mu"][l]
    W["mu"] = jnp.concatenate([mu[:3072], mu[3264:3520], mu[3072:3264], jnp.zeros((64,), F32)])[None]
    W["vec"] = jnp.pad(jnp.stack([P["rwkv_w0"][l], P["rwkv_a0"][l], P["rwkv_kk"][l], P["rwkv_ka"][l],
                                  P["rwkv_rk"][l].reshape(-1)]), ((0, 3), (0, 0)))
    W["w2p"] = jnp.pad(P["rwkv_w2"][l], ((0, 160), (0, 0))).astype(BF16)
    W["a2p"] = jnp.pad(P["rwkv_a2"][l], ((96, 64), (0, 0))).astype(BF16)
    W["g2"] = P["rwkv_g2"][l].astype(BF16)
    W["ln"] = jnp.pad(jnp.stack([P["rwkv_lnx_w"][l], P["rwkv_lnx_b"][l]]), ((0, 6), (0, 0)))
    W["wb"] = P["w_branch"][l].astype(BF16)
    W["wout"] = P["w_out"][l].astype(BF16)
    F = P["ffn_conv_w"].shape[-1]
    Fp = -(-F // 512) * 512
    padc = lambda a: jnp.pad(a, ((0, 0), (0, Fp - F)))
    w_up = P["ffn_w_up"][l]
    W["wu"] = padc(w_up[:, :F]).astype(BF16)
    W["wg"] = padc(w_up[:, F:]).astype(BF16)
    W["cw"] = jnp.pad(padc(P["ffn_conv_w"][l]), ((0, 5), (0, 0)))
    W["cb"] = padc(P["ffn_conv_b"][l][None])
    W["wd"] = jnp.pad(P["ffn_w_down"][l], ((0, Fp - F), (0, 0))).astype(BF16)
    W["F"], W["Fp"] = F, Fp
    return W


def _rw_to_mine(a):
    pad = jnp.zeros(a.shape[:-1] + (64,), a.dtype)
    return jnp.concatenate([a[..., :3072], a[..., 3264:3520], a[..., 3072:3264], pad], axis=-1)


def _rw_from_mine(a):
    return jnp.concatenate([a[..., :3072], a[..., 3328:3520], a[..., 3072:3328]], axis=-1)


def _state_to_pairs(s):
    S, H = s.shape[:2]
    sp = s.reshape(S, H // 2, 2, HEAD_DIM, HEAD_DIM)
    z = jnp.zeros_like(sp[:, :, 0])
    top = jnp.concatenate([sp[:, :, 0], z], axis=-1)
    bot = jnp.concatenate([z, sp[:, :, 1]], axis=-1)
    bd = jnp.concatenate([top, bot], axis=-2)
    return bd.transpose(0, 2, 1, 3).reshape(S, LANES, H // 2 * LANES)


def _state_from_pairs(bd, H):
    S = bd.shape[0]
    b4 = bd.reshape(S, LANES, H // 2, LANES).transpose(0, 2, 1, 3)
    return jnp.stack([b4[:, :, :HEAD_DIM, :HEAD_DIM], b4[:, :, HEAD_DIM:, HEAD_DIM:]], axis=2).reshape(
        S, H, HEAD_DIM, HEAD_DIM)


def _layer(x, W, tb, *, prompt, n_seq, seq_rows, n_new, past, dec):
    blk_a = rms_matmul(x, W["norm_mix"], W["wA"])
    blk_b = rms_matmul(x, W["norm_mix"], W["wB"])
    blk_c = rms_matmul(x, W["norm_mix"], W["wC"])
    blk_d = rms_matmul(x, W["norm_mix"], W["wD"])
    nq, cmp_new, slc_new, win_new = nsa_prep(blk_b, W["gains"])
    M = x.shape[0]
    if prompt:
        o_a = sb_prompt(blk_a, n_seq, seq_rows, 16)
        fs = compress(cmp_new.reshape(M // LANES, 1, LANES, 512), 0, W["w1c"])
        kd, vd = cmp_finish(fs.reshape(n_seq, seq_rows // CMP_STRIDE, 1024), W["posb"], W["w2dup"], W["b2"],
                            W["gain1"])
        tq = tb["tq"]
        o_cmp, sel = cmp_attn(nq, kd, vd, tb["cmp_p"], tb["ov_p"], S=n_seq, tq=tq, n_q=seq_rows // tq,
                              n_sel=tb["n_sel_p"], pos0=0, transposed=True)
        o_slc = flash_prompt(nq, slc_new, tb["tiles"], sel, B=n_seq, T=seq_rows, mode="slc")
        o_win = flash_prompt(nq, win_new, tb["tiles"], None, B=n_seq, T=seq_rows, mode="win")
        head = jnp.zeros((n_seq, SUBLANES, RW_COLS), F32)
        rw = rwkv_prep(blk_c, head, W["mu"], W["vec"], W["w2p"], W["a2p"], W["g2"], seq_rows=seq_rows,
                       whole=False)
        C = RWKV_CHUNK
        s0 = jnp.zeros((n_seq, LANES, 1024), F32)
    else:
        lyr = dec["layer"]
        pt = dec["page_table"]
        o_a = sb_decode(blk_a, dec["cache_sb"], pt, lyr, n_new)
        fs_phys = compress(dec["cache_cmp"], lyr, W["w1c"])
        fs = gather_page_rows(fs_phys, pt)
        kd, vd = cmp_finish(fs, W["posb"], W["w2dup"], W["b2"], W["gain1"])
        o_cmp, sel = cmp_attn(nq, kd, vd, tb["cmp_d"], tb["ov_d"], S=n_seq, tq=ROW_GROUP, n_q=1,
                              n_sel=tb["n_sel_d"], pos0=past - (ROW_GROUP - n_new), transposed=False)
        o_slc = slc_decode(nq, slc_new, sel, tb["slc_d"], dec["cache_slc"], pt, lyr, n_new)
        o_win = win_decode(nq, win_new, tb["win_d"], dec["cache_win"], lyr, n_new)
        shift = _rw_to_mine(dec["shift"][:, lyr])
        head = jnp.zeros((n_seq, ROW_GROUP, RW_COLS), F32).at[:, ROW_GROUP - n_new - 1].set(shift)
        rw = rwkv_prep(blk_c, head.reshape(M, RW_COLS), W["mu"], W["vec"], W["w2p"], W["a2p"], W["g2"],
                       seq_rows=ROW_GROUP, whole=True, n_new=n_new)
        C = ROW_GROUP
        s0 = _state_to_pairs(dec["rwkv"][:, lyr].astype(F32))
    o_b = nsa_combine(o_cmp, o_slc, o_win, blk_c, gate_block=27, lane0=64)
    r, lw, k2, v, kk, b, bonus, g = rw
    r2, y2, mt, gt = rwkv_par(r, lw, k2, v, kk, b, C)
    o_c, s_fin = rwkv_seq(r2, y2, mt, gt, s0, bonus, g, W["ln"], n_seq=n_seq)
    mixed = merge_mix(o_a, o_b, o_c, W["wb"], blk_d)
    x1 = matmul_res(x, mixed, W["wout"])
    F, Fp = W["F"], W["Fp"]
    if prompt:
        x2, tail = conv_ffn(x1, W["norm_ffn"], W["wu"], W["wg"], W["cw"], W["cb"], W["wd"],
                            jnp.zeros((n_seq, SUBLANES, Fp), F32), seq_rows=seq_rows, whole_gate=False)
        conv = tail[:, SUBLANES - 2:, :F]
        last = blk_c.reshape(n_seq, seq_rows, RW_COLS)[:, -1]
    else:
        inj = jnp.zeros((n_seq, ROW_GROUP, Fp), F32).at[:, ROW_GROUP - n_new - 2:ROW_GROUP - n_new, :F].set(
            dec["conv"][:, lyr])
        real = (jnp.arange(M) % ROW_GROUP >= ROW_GROUP - n_new)[:, None]
        x2, gate = conv_ffn(jnp.where(real, x1, 0.0), W["norm_ffn"], W["wu"], W["wg"], W["cw"], W["cb"], W["wd"],
                            inj.reshape(M, Fp), seq_rows=ROW_GROUP, whole_gate=True)
        x2 = jnp.where(real, x2, 0.0)
        conv = gate.reshape(n_seq, ROW_GROUP, Fp)[:, ROW_GROUP - 2:, :F]
        last = blk_c.reshape(n_seq, ROW_GROUP, RW_COLS)[:, -1]
    new = {"sb": blk_a[:, 1024:], "cmp": cmp_new, "slc": slc_new, "win": win_new,
           "rwkv": _state_from_pairs(s_fin, 16), "shift": _rw_from_mine(last), "conv": conv}
    return x2, new


def kernel(x_prompt, x_sample, cache_sb_kv, cache_cmp_kv, cache_slc_kv, cache_win_kv, state_rwkv, state_rwkv_shift, state_conv, page_table, rel_bias, norm_mix_g, norm_ffn_g, w_in, nsa_qk_gain, cmp_pe, cmp_w1, cmp_b1, cmp_w2, cmp_b2, rwkv_mu, rwkv_w0, rwkv_w2, rwkv_a0, rwkv_a2, rwkv_g2, rwkv_kk, rwkv_ka, rwkv_rk, rwkv_lnx_w, rwkv_lnx_b, w_branch, w_out, ffn_w_up, ffn_conv_w, ffn_conv_b, ffn_w_down):
    P = dict(norm_mix_g=norm_mix_g, norm_ffn_g=norm_ffn_g, w_in=w_in, nsa_qk_gain=nsa_qk_gain, cmp_pe=cmp_pe,
             cmp_w1=cmp_w1, cmp_b1=cmp_b1, cmp_w2=cmp_w2, cmp_b2=cmp_b2, rwkv_mu=rwkv_mu, rwkv_w0=rwkv_w0,
             rwkv_w2=rwkv_w2, rwkv_a0=rwkv_a0, rwkv_a2=rwkv_a2, rwkv_g2=rwkv_g2, rwkv_kk=rwkv_kk,
             rwkv_ka=rwkv_ka, rwkv_rk=rwkv_rk, rwkv_lnx_w=rwkv_lnx_w, rwkv_lnx_b=rwkv_lnx_b,
             w_branch=w_branch, w_out=w_out, ffn_w_up=ffn_w_up, ffn_conv_w=ffn_conv_w,
             ffn_conv_b=ffn_conv_b, ffn_w_down=ffn_w_down)
    B, T, D = x_prompt.shape
    S, n_new, _ = x_sample.shape
    depth = w_in.shape[0]
    n_phys, _, page = cache_sb_kv.shape[:3]
    n_pages = page_table.shape[1]
    past = n_pages * page
    n_heads = rel_bias.shape[1]
    assert page == LANES and n_new + 2 <= ROW_GROUP and T % 128 == 0 and past % LANES == 0
    assert cache_win_kv.shape[2] == WINDOW <= past
    tb = _make_tables(rel_bias, T, past, n_new, n_heads)
    dec = {"page_table": page_table,
           "cache_sb": jnp.transpose(cache_sb_kv, (0, 1, 3, 4, 5, 2)).reshape(n_phys, depth, -1, page),
           "cache_cmp": cache_cmp_kv.reshape(n_phys, depth, page, -1),
           "cache_slc": jnp.transpose(cache_slc_kv, (0, 1, 3, 4, 5, 2)).reshape(n_phys, depth, -1, page),
           "cache_win": cache_win_kv.reshape(S, depth, WINDOW, -1),
           "rwkv": state_rwkv, "shift": state_rwkv_shift, "conv": state_conv}
    xp = x_prompt.reshape(B * T, D)
    xs = jnp.zeros((S, ROW_GROUP, D), F32).at[:, ROW_GROUP - n_new:].set(x_sample).reshape(S * ROW_GROUP, D)
    new_p, new_s = [], []
    for l in range(depth):
        W = _prep_layer(l, P)
        xp, st = _layer(xp, W, tb, prompt=True, n_seq=B, seq_rows=T, n_new=0, past=0, dec=None)
        new_p.append(st)
        dec["layer"] = l
        xs, st = _layer(xs, W, tb, prompt=False, n_seq=S, seq_rows=ROW_GROUP, n_new=n_new, past=past, dec=dec)
        new_s.append(st)

    n_win = min(WINDOW, T)

    def stack_p(name, shape):
        return jnp.stack([st[name].reshape((B, -1) + shape) for st in new_p], axis=1)

    def rows_s(a):
        return a.reshape(S, ROW_GROUP, -1)[:, ROW_GROUP - n_new:]

    def stack_s(name, shape):
        return jnp.stack([rows_s(st[name]).reshape((S, n_new) + shape) for st in new_s], axis=1)

    kvh = cache_cmp_kv.shape[4]
    sbh = cache_sb_kv.shape[4]
    p_win = jnp.stack([st["win"].reshape(B, T, 2, kvh, HEAD_DIM)[:, T - n_win:] for st in new_p], axis=1)
    s_win = jnp.stack([jnp.concatenate([cache_win_kv[:, l], rows_s(st["win"]).reshape(S, n_new, 2, kvh, HEAD_DIM)],
                                       axis=1)[:, n_new:] for l, st in enumerate(new_s)], axis=1)
    outs = (xp.reshape(B, T, D), rows_s(xs),
            stack_p("sb", (2, sbh, HEAD_DIM)), stack_p("cmp", (2, kvh, HEAD_DIM)),
            stack_p("slc", (2, kvh, HEAD_DIM)), p_win,
            jnp.stack([st["rwkv"] for st in new_p], axis=1), jnp.stack([st["shift"] for st in new_p], axis=1),
            jnp.stack([st["conv"] for st in new_p], axis=1),
            stack_s("sb", (2, sbh, HEAD_DIM)), stack_s("cmp", (2, kvh, HEAD_DIM)),
            stack_s("slc", (2, kvh, HEAD_DIM)), s_win,
            jnp.stack([st["rwkv"] for st in new_s], axis=1), jnp.stack([st["shift"] for st in new_s], axis=1),
            jnp.stack([st["conv"] for st in new_s], axis=1))
    return outs
```

```python
import functools
import math

import numpy as np
import jax
import jax.numpy as jnp
from jax import lax
from jax.experimental import pallas as pl
from jax.experimental.pallas import tpu as pltpu

F32 = jnp.float32
BF16 = jnp.bfloat16

HEAD_DIM = 64
LANES = 128
SUBLANES = 8
ROW_GROUP = 8
VMEM_LIMIT = 56 * 2 ** 20

RMS_EPS = 1e-6
GN_EPS = 64e-5
CMP_STRIDE = 16
CMP_BLOCK = 32
SEL_BLOCK = 64
N_SEL = 16
WINDOW = 512
N_BUCKETS = 32
MAX_DISTANCE = 1024
NEG = -1e30
M_FLOOR = -1e20
RWKV_CHUNK = 64

NN = (((1,), (0,)), ((), ()))
NT = (((1,), (1,)), ((), ()))
TN = (((0,), (0,)), ((), ()))


def _dot(a, b, dims=NN):
    return lax.dot_general(a, b, dims, preferred_element_type=F32)


def _split(x):
    hi = x.astype(BF16)
    lo = (x - hi.astype(F32)).astype(BF16)
    return hi, lo


def _dot_hl(a, b_exact, dims=NN):
    hi, lo = _split(a)
    return _dot(hi, b_exact, dims) + _dot(lo, b_exact, dims)


def _dot_hl_rhs(a_exact, b):
    hi, lo = _split(b)
    return _dot(a_exact, hi) + _dot(a_exact, lo)


def _dot3(a, b, dims=NN):
    ah, al = _split(a)
    bh, bl = _split(b)
    return _dot(ah, bh, dims) + (_dot(ah, bl, dims) + _dot(al, bh, dims))


def _iota(shape, dim):
    return lax.broadcasted_iota(jnp.int32, shape, dim)


def _pick(n, cands):
    for c in cands:
        if n % c == 0:
            return c
    raise ValueError(f"no tile for {n} in {cands}")


def _cparams(*sem):
    return pltpu.CompilerParams(dimension_semantics=sem, vmem_limit_bytes=VMEM_LIMIT)


def _seg_mat(n, seg, scale):
    r = lax.shift_right_logical(_iota((n, n), 0), int(math.log2(seg)))
    c = lax.shift_right_logical(_iota((n, n), 1), int(math.log2(seg)))
    return jnp.where(r == c, scale, 0.0).astype(BF16)


def _lo_mask():
    return _iota((1, LANES), 1) < HEAD_DIM


def _rms_rows(x, g):
    ms = jnp.mean(x * x, axis=-1, keepdims=True)
    return x * lax.rsqrt(ms + RMS_EPS) * g


def _rms_matmul_kernel(x_ref, g_ref, w_ref, o_ref, h_ref):
    @pl.when(pl.program_id(1) == 0)
    def _():
        h_ref[...] = _rms_rows(x_ref[...], g_ref[...]).astype(BF16)

    o_ref[...] = _dot(h_ref[...], w_ref[...])


def rms_matmul(x, g, w, tn=512):
    M, K = x.shape
    N = w.shape[1]
    tm = _pick(M, (1024, 512, 256, 128, 64, 32, 16, 8))
    tn = _pick(N, (tn, 256, 128))
    return pl.pallas_call(
        _rms_matmul_kernel,
        grid=(M // tm, N // tn),
        in_specs=[pl.BlockSpec((tm, K), lambda i, j: (i, 0)),
                  pl.BlockSpec((1, K), lambda i, j: (0, 0)),
                  pl.BlockSpec((K, tn), lambda i, j: (0, j))],
        out_specs=pl.BlockSpec((tm, tn), lambda i, j: (i, j)),
        out_shape=jax.ShapeDtypeStruct((M, N), F32),
        scratch_shapes=[pltpu.VMEM((tm, K), BF16)],
        compiler_params=_cparams("parallel", "arbitrary"),
        name="rms_matmul",
    )(x, g.reshape(1, K), w)


def _matmul_res_kernel(x_ref, a_ref, w_ref, o_ref):
    o_ref[...] = x_ref[...] + _dot(a_ref[...], w_ref[...])


def matmul_res(x, a, w, tn=512):
    M, N = x.shape
    K = a.shape[1]
    tm = _pick(M, (1024, 512, 256, 128, 64, 32, 16, 8))
    tn = _pick(N, (tn, 256, 128))
    return pl.pallas_call(
        _matmul_res_kernel,
        grid=(M // tm, N // tn),
        in_specs=[pl.BlockSpec((tm, tn), lambda i, j: (i, j)),
                  pl.BlockSpec((tm, K), lambda i, j: (i, 0)),
                  pl.BlockSpec((K, tn), lambda i, j: (0, j))],
        out_specs=pl.BlockSpec((tm, tn), lambda i, j: (i, j)),
        out_shape=jax.ShapeDtypeStruct((M, N), F32),
        compiler_params=_cparams("parallel", "arbitrary"),
        name="matmul_res",
    )(x, a, w)


def _merge_mix_kernel(oa_ref, ob_ref, oc_ref, wb_ref, m0_ref, m1_ref, m2_ref, o_ref):
    acc = jax.nn.sigmoid(m0_ref[...]) * _dot(oa_ref[...], wb_ref[0])
    acc = acc + jax.nn.sigmoid(m1_ref[...]) * _dot(ob_ref[...], wb_ref[1])
    acc = acc + jax.nn.sigmoid(m2_ref[...]) * _dot(oc_ref[...], wb_ref[2])
    o_ref[...] = acc.astype(o_ref.dtype)


def merge_mix(oa, ob, oc, wb, merge, tn=512):
    M, W = oa.shape
    D = wb.shape[2]
    tm = _pick(M, (512, 256, 128, 64, 32, 16, 8))
    nb = D // tn
    ospec = pl.BlockSpec((tm, W), lambda i, j: (i, 0))
    return pl.pallas_call(
        _merge_mix_kernel,
        grid=(M // tm, nb),
        in_specs=[ospec, ospec, ospec,
                  pl.BlockSpec((3, W, tn), lambda i, j: (0, 0, j)),
                  pl.BlockSpec((tm, tn), lambda i, j: (i, j)),
                  pl.BlockSpec((tm, tn), lambda i, j: (i, j + nb)),
                  pl.BlockSpec((tm, tn), lambda i, j: (i, j + 2 * nb))],
        out_specs=pl.BlockSpec((tm, tn), lambda i, j: (i, j)),
        out_shape=jax.ShapeDtypeStruct((M, D), BF16),
        compiler_params=_cparams("parallel", "arbitrary"),
        name="merge_mix",
    )(oa, ob, oc, wb, merge, merge, merge)


def _ffn_kernel(x_ref, g_ref, wu_ref, wg_ref, cw_ref, cb_ref, wd_ref, prev_ref,
                o_ref, st_ref, h_ref, acc_ref, gs_ref, carry_ref, *, tm, tps, whole_gate):
    i = pl.program_id(0)
    n = pl.program_id(1)

    @pl.when(n == 0)
    def _():
        h_ref[...] = _rms_rows(x_ref[...], g_ref[...]).astype(BF16)
        acc_ref[...] = jnp.zeros_like(acc_ref)

    h = h_ref[...]
    u = _dot(h, wu_ref[...])
    g = _dot(h, wg_ref[...])
    if whole_gate:
        g = g + prev_ref[...]
        head = jnp.zeros((SUBLANES, g.shape[1]), F32)
    else:
        head = jnp.where(i % tps == 0, prev_ref[0], carry_ref[n])
    gs_ref[pl.ds(0, SUBLANES), :] = head
    gs_ref[pl.ds(SUBLANES, tm), :] = g
    gm1 = gs_ref[pl.ds(SUBLANES - 1, tm), :]
    gm2 = gs_ref[pl.ds(SUBLANES - 2, tm), :]
    gc = cb_ref[...] + gm2 * cw_ref[0:1, :] + gm1 * cw_ref[1:2, :] + g * cw_ref[2:3, :]
    act = (gc * jax.nn.sigmoid(gc) * u).astype(BF16)
    acc_ref[...] += _dot(act, wd_ref[...])
    if whole_gate:
        st_ref[...] = g
    else:
        tail = gs_ref[pl.ds(tm, SUBLANES), :]
        carry_ref[n] = tail
        st_ref[0] = tail

    @pl.when(n == pl.num_programs(1) - 1)
    def _():
        o_ref[...] = x_ref[...] + acc_ref[...]


def conv_ffn(x, g, wu, wg, cw, cb, wd, prev, *, seq_rows, whole_gate, tn=512):
    M, D = x.shape
    Fp = wu.shape[1]
    nb = Fp // tn
    if whole_gate:
        tm, tps = M, 1
        prev_spec = pl.BlockSpec((tm, tn), lambda i, n: (0, n))
        st_spec = pl.BlockSpec((tm, tn), lambda i, n: (0, n))
        st_shape = jax.ShapeDtypeStruct((M, Fp), F32)
    else:
        tm = _pick(seq_rows, (512, 256, 128, 64))
        tps = seq_rows // tm
        prev_spec = pl.BlockSpec((1, SUBLANES, tn), lambda i, n: (i // tps, 0, n))
        st_spec = pl.BlockSpec((1, SUBLANES, tn), lambda i, n: (i, 0, n))
        st_shape = jax.ShapeDtypeStruct((M // tm, SUBLANES, Fp), F32)
    kern = functools.partial(_ffn_kernel, tm=tm, tps=tps, whole_gate=whole_gate)
    y, st = pl.pallas_call(
        kern,
        grid=(M // tm, nb),
        in_specs=[pl.BlockSpec((tm, D), lambda i, n: (i, 0)),
                  pl.BlockSpec((1, D), lambda i, n: (0, 0)),
                  pl.BlockSpec((D, tn), lambda i, n: (0, n)),
                  pl.BlockSpec((D, tn), lambda i, n: (0, n)),
                  pl.BlockSpec((SUBLANES, tn), lambda i, n: (0, n)),
                  pl.BlockSpec((1, tn), lambda i, n: (0, n)),
                  pl.BlockSpec((tn, D), lambda i, n: (n, 0)),
                  prev_spec],
        out_specs=[pl.BlockSpec((tm, D), lambda i, n: (i, 0)), st_spec],
        out_shape=[jax.ShapeDtypeStruct((M, D), F32), st_shape],
        scratch_shapes=[pltpu.VMEM((tm, D), BF16), pltpu.VMEM((tm, D), F32),
                        pltpu.VMEM((tm + SUBLANES, tn), F32), pltpu.VMEM((nb, SUBLANES, tn), F32)],
        compiler_params=_cparams("arbitrary", "arbitrary"),
        name="conv_ffn",
    )(x, g.reshape(1, D), wu, wg, cw, cb, wd, prev)
    return (y, st) if whole_gate else (y, st[tps - 1::tps])


def _sb_rhs():
    j = jnp.bitwise_and(_iota((2 * LANES, 2 * LANES), 0), LANES - 1)
    s = _iota((2 * LANES, 2 * LANES), 1)
    return jnp.where((j > s) | (s >= LANES), -1.0, 0.0).astype(BF16)


def _sb_sums(sp, nrhs):
    hi, lw = _split(sp)
    return _dot(jnp.concatenate([hi, lw], axis=1), nrhs)


LOG2E = math.log2(math.e)


def _softplus2(z):
    return jnp.maximum(z, 0.0) + jnp.log2(1.0 + jnp.exp2(-jnp.abs(z)))


def _sb_block(qm, kb, vb, before, c, nrhs):
    z = _dot(qm, kb, NT)
    sp = _softplus2(z)
    if before is not None:
        sp = jnp.where(before, sp, 0.0)
    cs2 = _sb_sums(sp, nrhs)
    w = jnp.exp2((z - sp) + (cs2[:, :LANES] + c))
    if before is not None:
        w = jnp.where(before, w, 0.0)
    return _dot(w.astype(BF16), vb), c + cs2[:, LANES:]


def _sb_pair_blocks(q, kbs, vbs, befores, c, nrhs, lo):
    def per_head(x):
        zero = jnp.zeros_like(x)
        return jnp.concatenate([jnp.where(lo, x, zero), jnp.where(lo, zero, x)], axis=0)

    kbd = [per_head(x) for x in kbs]
    vbd = [per_head(x) for x in vbs]
    z = [_dot(q, x, NT) for x in kbd]
    sp = [_softplus2(x) for x in z]
    sp = [x if m is None else jnp.where(m, x, 0.0) for x, m in zip(sp, befores)]
    cs = [[_sb_sums(x[:, h * LANES:(h + 1) * LANES], nrhs) for h in range(2)] for x in sp]
    suffix = [jnp.concatenate([y[0][:, :LANES], y[1][:, :LANES]], axis=1) for y in cs]
    total = [jnp.concatenate([y[0][:, LANES:], y[1][:, LANES:]], axis=1) for y in cs]
    carry = [c]
    for t in total:
        carry.append(carry[-1] + t)
    w = [jnp.exp2((a - b) + (s + cc)) for a, b, s, cc in zip(z, sp, suffix, carry)]
    w = [x if m is None else jnp.where(m, x, 0.0) for x, m in zip(w, befores)]
    pv = [_dot(x.astype(BF16), y) for x, y in zip(w, vbd)]
    acc = pv[0]
    for x in pv[1:]:
        acc = acc + x
    return acc, carry[-1]


def _sb_prompt_kernel(qi_ref, kj_ref, q_ref, k_ref, v_ref, o_ref, acc_ref, c_ref, *, tq, tk):
    s = pl.program_id(2)
    qi = qi_ref[s]
    kj = kj_ref[s]
    lo = _lo_mask()
    first = (kj + 1) * tk == (qi + 1) * tq
    overlap = (kj + 1) * tk > qi * tq

    def tile(masked):
        q = (q_ref[...] * LOG2E).astype(BF16)
        nrhs = _sb_rhs()
        c = jnp.where(first, 0.0, c_ref[...])
        subs = list(reversed(range(tk // LANES)))
        kbs = [k_ref[pl.ds(sub * LANES, LANES), :].astype(BF16) for sub in subs]
        vbs = [v_ref[pl.ds(sub * LANES, LANES), :].astype(BF16) for sub in subs]
        befores = [None] * len(subs)
        if masked:
            lane_key = jnp.bitwise_and(_iota((tq, 2 * LANES), 1), LANES - 1)
            row = qi * tq + _iota((tq, 2 * LANES), 0)
            befores = [kj * tk + sub * LANES + lane_key < row for sub in subs]
        acc, c = _sb_pair_blocks(q, kbs, vbs, befores, c, nrhs, lo)
        c_ref[...] = c
        acc_ref[...] = jnp.where(first, 0.0, acc_ref[...]) + acc

    @pl.when(overlap)
    def _():
        tile(True)

    @pl.when(jnp.logical_not(overlap))
    def _():
        tile(False)

    @pl.when(kj == 0)
    def _():
        o_ref[...] = acc_ref[...].astype(o_ref.dtype)


def _tri_steps(n, lookback=None):
    qs, ks, first, last = [], [], [], []
    for q in range(n):
        k_lo = 0 if lookback is None else max(0, q - lookback)
        for k in range(q, k_lo - 1, -1):
            qs.append(q)
            ks.append(k)
            first.append(int(k == q))
            last.append(int(k == k_lo))
    return tuple(jnp.asarray(np.array(a, np.int32)) for a in (qs, ks, first, last))


def _sb_steps(nq, ratio):
    qs, ks = [], []
    for q in range(nq):
        for k in range((q + 1) * ratio - 1, -1, -1):
            qs.append(q)
            ks.append(k)
    return jnp.asarray(np.array(qs, np.int32)), jnp.asarray(np.array(ks, np.int32))


def sb_prompt(qkv, B, T, n_heads, tq=None, tk=None):
    M = B * T
    tk = tk or _pick(T, (512, 256, 128))
    tq = tq or _pick(T, (512, 256, 128))
    assert tq % tk == 0
    nq, nk = T // tq, T // tk
    npair = n_heads // 2
    qs, ks = _sb_steps(nq, tq // tk)
    kern = functools.partial(_sb_prompt_kernel, tq=tq, tk=tk)
    return pl.pallas_call(
        kern,
        grid_spec=pltpu.PrefetchScalarGridSpec(
            num_scalar_prefetch=2,
            grid=(B, npair, int(qs.shape[0])),
            in_specs=[pl.BlockSpec((tq, LANES), lambda b, p, s, qi, kj: (b * nq + qi[s], p)),
                      pl.BlockSpec((tk, LANES), lambda b, p, s, qi, kj: (b * nk + kj[s], npair + p)),
                      pl.BlockSpec((tk, LANES), lambda b, p, s, qi, kj: (b * nk + kj[s], 2 * npair + p))],
            out_specs=pl.BlockSpec((tq, LANES), lambda b, p, s, qi, kj: (b * nq + qi[s], p)),
            scratch_shapes=[pltpu.VMEM((tq, LANES), F32), pltpu.VMEM((tq, 2 * LANES), F32)]),
        out_shape=jax.ShapeDtypeStruct((M, n_heads * HEAD_DIM), BF16),
        compiler_params=_cparams("parallel", "parallel", "arbitrary"),
        name="sb_prompt",
    )(qs, ks, qkv, qkv, qkv)


def _block_diag_rows(q8, n_heads):
    rows = n_heads * ROW_GROUP
    width = q8.shape[1]
    tiled = jnp.concatenate([q8] * n_heads, axis=0)
    rh = lax.shift_right_logical(_iota((rows, width), 0), 3)
    ch = lax.shift_right_logical(_iota((rows, width), 1), 6)
    return jnp.where(rh == ch, tiled, 0.0), rh == ch


def _sb_blocks(qm, kts, vts, c, nrhs):
    z = [_dot(qm, x) for x in kts]
    sp = [_softplus2(x) for x in z]
    cs = [_sb_sums(x, nrhs) for x in sp]
    carry = [c]
    for x in cs:
        carry.append(carry[-1] + x[:, LANES:])
    w = [jnp.exp2((a - b) + (x[:, :LANES] + cc)).astype(BF16) for a, b, x, cc in zip(z, sp, cs, carry)]
    pv = [_dot(x, y, NT) for x, y in zip(w, vts)]
    acc = pv[0]
    for x in pv[1:]:
        acc = acc + x
    return acc, carry[-1]


def _sb_decode_kernel(pt_ref, q_ref, kn_ref, vn_ref, *rest, n_heads, n_steps, n_new, ppb):
    page_refs = rest[:ppb]
    o_ref, acc_ref, c_ref, qbd_ref = rest[ppb:]
    p = pl.program_id(1)
    rows = n_heads * ROW_GROUP
    width = n_heads * HEAD_DIM
    rhs = _sb_rhs()

    @pl.when(p == 0)
    def _():
        qbd, _ = _block_diag_rows(q_ref[...] * LOG2E, n_heads)
        qbd = qbd.astype(BF16)
        qbd_ref[...] = qbd
        pad = jnp.zeros((LANES - ROW_GROUP, width), F32)
        kb = jnp.concatenate([kn_ref[...], pad], axis=0).astype(BF16)
        vb = jnp.concatenate([vn_ref[...], pad], axis=0).astype(BF16)
        r_in = jnp.bitwise_and(_iota((rows, LANES), 0), ROW_GROUP - 1)
        col = _iota((rows, LANES), 1)
        before = (col >= ROW_GROUP - n_new) & (col < ROW_GROUP) & (col < r_in)
        pv, c_new = _sb_block(qbd, kb, vb, before, jnp.zeros((rows, LANES), F32), rhs)
        acc_ref[...] = pv
        c_ref[...] = c_new

    @pl.when(p > 0)
    def _():
        kts = [ref[pl.ds(0, width), :].astype(BF16) for ref in page_refs]
        vts = [ref[pl.ds(width, width), :].astype(BF16) for ref in page_refs]
        pv, c_new = _sb_blocks(qbd_ref[...], kts, vts, c_ref[...], rhs)
        acc_ref[...] += pv
        c_ref[...] = c_new

    @pl.when(p == n_steps - 1)
    def _():
        own = (lax.shift_right_logical(_iota((rows, width), 0), 3)
               == lax.shift_right_logical(_iota((rows, width), 1), 6))
        m = jnp.where(own, acc_ref[...], 0.0).reshape(n_heads, ROW_GROUP, width)
        o_ref[...] = jnp.sum(m, axis=0).astype(o_ref.dtype)


def sb_decode(qkv, cache, page_table, layer, n_new):
    S, n_pages = page_table.shape
    page = cache.shape[3]
    W = cache.shape[2] // 2
    n_heads = W // HEAD_DIM
    assert page == LANES
    rows = n_heads * ROW_GROUP
    pt = page_table.reshape(-1).astype(jnp.int32)
    ppb = _pick(n_pages, (4, 2, 1))
    n_steps = 1 + n_pages // ppb

    def page_map(i):
        def index(s, p, pt_ref):
            logical = n_pages - 1 - (jnp.maximum(p, 1) - 1) * ppb - i
            return (pt_ref[s * n_pages + logical], layer, 0, 0)
        return index

    kern = functools.partial(_sb_decode_kernel, n_heads=n_heads, n_steps=n_steps, n_new=n_new, ppb=ppb)
    return pl.pallas_call(
        kern,
        grid_spec=pltpu.PrefetchScalarGridSpec(
            num_scalar_prefetch=1,
            grid=(S, n_steps),
            in_specs=[pl.BlockSpec((ROW_GROUP, W), lambda s, p, pt_ref: (s, 0)),
                      pl.BlockSpec((ROW_GROUP, W), lambda s, p, pt_ref: (s, 1)),
                      pl.BlockSpec((ROW_GROUP, W), lambda s, p, pt_ref: (s, 2))]
            + [pl.BlockSpec((None, None, 2 * W, page), page_map(i)) for i in range(ppb)],
            out_specs=pl.BlockSpec((ROW_GROUP, W), lambda s, p, pt_ref: (s, 0)),
            scratch_shapes=[pltpu.VMEM((rows, W), F32), pltpu.VMEM((rows, LANES), F32),
                            pltpu.VMEM((rows, W), BF16)]),
        out_shape=jax.ShapeDtypeStruct((S * ROW_GROUP, W), BF16),
        compiler_params=_cparams("parallel", "arbitrary"),
        name="sb_decode",
    )(pt, qkv, qkv, qkv, *([cache] * ppb))


def _nsa_prep_kernel(x_ref, gain_ref, nq_ref, cmp_ref, slc_ref, win_ref):
    seg = _seg_mat(LANES, HEAD_DIM, 1.0 / HEAD_DIM)

    def norm(col, gain_row):
        blk = x_ref[:, pl.ds(col, LANES)]
        ms = _dot_hl(blk * blk, seg)
        return blk * lax.rsqrt(ms + RMS_EPS) * gain_ref[gain_row:gain_row + 1, :]

    for j in range(8):
        nq_ref[:, pl.ds(j * LANES, LANES)] = (norm(j * LANES, 0) * HEAD_DIM ** -0.5).astype(BF16)
    cmp_ref[...] = x_ref[:, pl.ds(1024, 512)]
    for j in range(2):
        slc_ref[:, pl.ds(j * LANES, LANES)] = norm(1536 + j * LANES, 2)
        win_ref[:, pl.ds(j * LANES, LANES)] = norm(2048 + j * LANES, 3)
    slc_ref[:, pl.ds(256, 256)] = x_ref[:, pl.ds(1792, 256)]
    win_ref[:, pl.ds(256, 256)] = x_ref[:, pl.ds(2304, 256)]


def nsa_prep(blk_b, gains):
    M = blk_b.shape[0]
    tm = _pick(M, (512, 256, 128, 64, 32, 16, 8))
    return pl.pallas_call(
        _nsa_prep_kernel,
        grid=(M // tm,),
        in_specs=[pl.BlockSpec((tm, 2560), lambda i: (i, 0)),
                  pl.BlockSpec((SUBLANES, LANES), lambda i: (0, 0))],
        out_specs=[pl.BlockSpec((tm, 1024), lambda i: (i, 0)),
                   pl.BlockSpec((tm, 512), lambda i: (i, 0)),
                   pl.BlockSpec((tm, 512), lambda i: (i, 0)),
                   pl.BlockSpec((tm, 512), lambda i: (i, 0))],
        out_shape=[jax.ShapeDtypeStruct((M, 1024), BF16)] + [jax.ShapeDtypeStruct((M, 512), F32)] * 3,
        compiler_params=_cparams("parallel"),
        name="nsa_prep",
    )(blk_b, gains)


def _compress_kernel(x0_ref, x1_ref, x2_ref, x3_ref, w_ref, o_ref, *, G):
    for p, x_ref in enumerate((x0_ref, x1_ref, x2_ref, x3_ref)):
        acc = jnp.zeros((G * SUBLANES, 2 * LANES), F32)
        for s in range(CMP_STRIDE):
            xs = x_ref[:, pl.ds(s, SUBLANES, stride=CMP_STRIDE), :].reshape(G * SUBLANES, LANES).astype(BF16)
            acc = acc + _dot(xs, w_ref[p // 2, s])
        o_ref[:, pl.ds(p * 2 * LANES, 2 * LANES)] = acc


def _compress_t_kernel(x0_ref, x1_ref, x2_ref, x3_ref, w_ref, o_ref, rows_ref, *, G):
    for p, x_ref in enumerate((x0_ref, x1_ref, x2_ref, x3_ref)):
        for g in range(G):
            rows_ref[g] = x_ref[g].T
        acc = jnp.zeros((G * SUBLANES, 2 * LANES), F32)
        for s in range(CMP_STRIDE):
            xs = rows_ref[:, pl.ds(s, SUBLANES, stride=CMP_STRIDE), :].reshape(G * SUBLANES, LANES).astype(BF16)
            acc = acc + _dot(xs, w_ref[p // 2, s])
        o_ref[:, pl.ds(p * 2 * LANES, 2 * LANES)] = acc


def compress(pages, layer, w1c, transposed=False):
    n_pages = pages.shape[0]
    G = _pick(n_pages, (16, 8, 5, 4, 3, 2, 1))
    if transposed:
        kern = functools.partial(_compress_t_kernel, G=G)
        page_spec = lambda p: pl.BlockSpec((G, None, LANES, LANES), lambda i: (i, layer, p, 0))
        scratch = [pltpu.VMEM((G, LANES, LANES), F32)]
    else:
        kern = functools.partial(_compress_kernel, G=G)
        page_spec = lambda p: pl.BlockSpec((G, None, LANES, LANES), lambda i: (i, layer, 0, p))
        scratch = []
    return pl.pallas_call(
        kern,
        grid=(n_pages // G,),
        in_specs=[page_spec(p) for p in range(4)]
        + [pl.BlockSpec((2, CMP_STRIDE, LANES, 2 * LANES), lambda i: (0, 0, 0, 0))],
        out_specs=pl.BlockSpec((G * SUBLANES, 1024), lambda i: (i, 0)),
        out_shape=jax.ShapeDtypeStruct((n_pages * SUBLANES, 1024), F32),
        scratch_shapes=scratch,
        compiler_params=_cparams("parallel"),
        name="nsa_compress",
    )(pages, pages, pages, pages, w1c)


def _gather_rows_kernel(pt_ref, src_ref, o_ref, sem, *, n_pages):
    s = pl.program_id(0)

    def copy(p):
        return pltpu.make_async_copy(src_ref.at[pl.ds(pt_ref[s * n_pages + p] * SUBLANES, SUBLANES)],
                                     o_ref.at[0, pl.ds(p * SUBLANES, SUBLANES)], sem)

    def start(p, carry):
        copy(p).start()
        return carry

    def wait(p, carry):
        copy(p).wait()
        return carry

    lax.fori_loop(0, n_pages, start, 0)
    lax.fori_loop(0, n_pages, wait, 0)


def gather_page_rows(src, page_table):
    S, n_pages = page_table.shape
    W = src.shape[1]
    kern = functools.partial(_gather_rows_kernel, n_pages=n_pages)
    return pl.pallas_call(
        kern,
        grid_spec=pltpu.PrefetchScalarGridSpec(
            num_scalar_prefetch=1,
            grid=(S,),
            in_specs=[pl.BlockSpec(memory_space=pl.ANY)],
            out_specs=pl.BlockSpec((1, n_pages * SUBLANES, W), lambda s, pt_ref: (s, 0, 0)),
            scratch_shapes=[pltpu.SemaphoreType.DMA(())]),
        out_shape=jax.ShapeDtypeStruct((S, n_pages * SUBLANES, W), F32),
        compiler_params=_cparams("arbitrary"),
        name="gather_page_rows",
    )(page_table.reshape(-1).astype(jnp.int32), src)


def _cmp_finish_kernel(fs_ref, posb_ref, w2_ref, b2_ref, gain_ref, kd_ref, vd_ref, *, nch):
    for p in range(4):
        c, pp = p // 2, p % 2
        first = fs_ref[:, pl.ds(p * 2 * LANES, LANES)]
        second = pltpu.roll(fs_ref[:, pl.ds(p * 2 * LANES + LANES, LANES)], nch - 1, 0)
        hid = jax.nn.gelu(first + second + posb_ref[c:c + 1, :]).astype(BF16)
        for e in range(2):
            out = _dot(hid, w2_ref[c, e]) + b2_ref[c:c + 1, :]
            if c == 0:
                ms = jnp.mean(out * out, axis=-1, keepdims=True)
                kd_ref[pp * 2 + e] = (out * lax.rsqrt(ms + RMS_EPS) * gain_ref[...]).astype(BF16)
            else:
                vd_ref[pp * 2 + e] = out.astype(BF16)


def cmp_finish(fs, posb, w2dup, b2, gain):
    S, nch, _ = fs.shape
    kern = functools.partial(_cmp_finish_kernel, nch=nch)
    small = pl.BlockSpec((SUBLANES, LANES), lambda s: (0, 0))
    return pl.pallas_call(
        kern,
        grid=(S,),
        in_specs=[pl.BlockSpec((None, nch, 1024), lambda s: (s, 0, 0)), small,
                  pl.BlockSpec((2, 2, LANES, LANES), lambda s: (0, 0, 0, 0)), small,
                  pl.BlockSpec((1, LANES), lambda s: (0, 0))],
        out_specs=[pl.BlockSpec((None, 4, nch, LANES), lambda s: (s, 0, 0, 0))] * 2,
        out_shape=[jax.ShapeDtypeStruct((S, 4, nch, LANES), BF16)] * 2,
        compiler_params=_cparams("parallel"),
        name="cmp_finish",
    )(fs, posb, w2dup, b2, gain)


def _head_queries(q, lo):
    out = []
    for g in range(4):
        blk = q[:, (g // 2) * LANES:(g // 2 + 1) * LANES]
        keep = lo if g % 2 == 0 else jnp.logical_not(lo)
        out.append(jnp.where(keep, blk, jnp.zeros_like(blk)))
    return out


def _pair_out(o, lo):
    return jnp.concatenate([jnp.where(lo, o[0], o[1]), jnp.where(lo, o[2], o[3])], axis=1)


def _topk_mask(score, n_sel, k_eff, axis):
    j = _iota(score.shape, axis)
    rank = jnp.zeros(score.shape, jnp.int32)
    for i in range(n_sel):
        si = score[i:i + 1, :] if axis == 0 else score[:, i:i + 1]
        ahead = (si > score) | ((si == score) & (j > i))
        rank = rank + jnp.where(ahead, 1, 0)
    return rank < k_eff


def _cmp_attn_kernel(q_ref, kd_ref, vd_ref, bias_ref, ov_ref, o_ref, sel_ref, *,
                     tq, ncp, n_sel, k_eff, pos0, transposed):
    t0 = pos0 + pl.program_id(2) * tq
    t = t0 + _iota((tq, ncp), 0)
    c = _iota((tq, ncp), 1)
    valid = t - (c * CMP_STRIDE + CMP_BLOCK - 1) >= 0
    lo = _lo_mask()
    kd = kd_ref[...]
    vd = vd_ref[...]
    qms = _head_queries(q_ref[...], lo)
    sc = [jnp.where(valid, _dot(qm, kd, NT) + bias_ref[g], NEG) for g, qm in enumerate(qms)]
    m = [jnp.max(x, axis=1, keepdims=True) for x in sc]
    e = [jnp.where(valid, jnp.exp(x - y), 0.0) for x, y in zip(sc, m)]
    den = [jnp.sum(x, axis=1, keepdims=True) for x in e]
    p = [x / jnp.where(y > 0, y, 1.0) for x, y in zip(e, den)]
    outs = [_dot(x.astype(BF16), vd) for x in p]
    psum = (p[0] + p[1]) + (p[2] + p[3])
    o_ref[...] = _pair_out(outs, lo)
    hi, lw = _split(psum)
    if transposed:
        imp = _dot(ov_ref[...], hi, NT) + _dot(ov_ref[...], lw, NT)
        j = _iota(imp.shape, 0)
        tt = t0 + _iota(imp.shape, 1)
    else:
        imp = _dot(hi, ov_ref[...]) + _dot(lw, ov_ref[...])
        j = _iota(imp.shape, 1)
        tt = t0 + _iota(imp.shape, 0)
    causal = j * SEL_BLOCK <= tt
    cur = lax.shift_right_logical(tt, 6)
    forced = causal & ((j == 0) | (j == cur) | (j == cur - 1))
    score = jnp.where(forced, -NEG, jnp.where(causal, imp, NEG))
    score = jnp.where(j < n_sel, score, 2 * NEG)
    sel = _topk_mask(score, n_sel, k_eff, 0 if transposed else 1)
    sel_ref[...] = jnp.where(sel, 1.0, 0.0)


def cmp_attn(nq, kd, vd, bias, ov, *, S, tq, n_q, n_sel, pos0, transposed):
    ncp = kd.shape[2]
    nsp = ov.shape[0] if transposed else ov.shape[1]
    k_eff = min(N_SEL, n_sel)
    rows = S * n_q * tq
    if transposed:
        sel_spec = pl.BlockSpec((None, None, nsp, tq), lambda b, k, i: (b, k, 0, i))
        sel_shape = jax.ShapeDtypeStruct((S, 4, nsp, n_q * tq), F32)
    else:
        sel_spec = pl.BlockSpec((None, None, tq, nsp), lambda b, k, i: (b, k, i, 0))
        sel_shape = jax.ShapeDtypeStruct((S, 4, n_q * tq, nsp), F32)
    kern = functools.partial(_cmp_attn_kernel, tq=tq, ncp=ncp, n_sel=n_sel, k_eff=k_eff, pos0=pos0,
                             transposed=transposed)
    return pl.pallas_call(
        kern,
        grid=(S, 4, n_q),
        in_specs=[pl.BlockSpec((tq, 2 * LANES), lambda b, k, i: (b * n_q + i, k)),
                  pl.BlockSpec((None, None, ncp, LANES), lambda b, k, i: (b, k, 0, 0)),
                  pl.BlockSpec((None, None, ncp, LANES), lambda b, k, i: (b, k, 0, 0)),
                  pl.BlockSpec((4, tq, ncp), lambda b, k, i: (k, i, 0)),
                  pl.BlockSpec(ov.shape, lambda b, k, i: (0, 0))],
        out_specs=[pl.BlockSpec((tq, 2 * LANES), lambda b, k, i: (b * n_q + i, k)), sel_spec],
        out_shape=[jax.ShapeDtypeStruct((rows, 1024), F32), sel_shape],
        compiler_params=_cparams("parallel", "parallel", "arbitrary"),
        name="cmp_attn",
    )(nq, kd, vd, bias, ov)


def _dup_half(x, odd):
    lo = _lo_mask()
    take_x = jnp.logical_xor(lo, odd)
    return jnp.where(take_x, x, pltpu.roll(x, HEAD_DIM, 1))


def _flash_kernel(qi_ref, kj_ref, dl_ref, first_ref, last_ref, q_ref, k_ref, v_ref, bias_ref, *rest,
                  tq, tk, mode):
    if mode == "slc":
        sel_ref, o_ref, m_ref, l_ref, acc_ref = rest
    else:
        o_ref, m_ref, l_ref, acc_ref = rest
    kvh = pl.program_id(1)
    s = pl.program_id(2)
    qi = qi_ref[s]
    kj = kj_ref[s]

    @pl.when(first_ref[s] == 1)
    def _():
        m_ref[...] = jnp.full_like(m_ref, M_FLOOR)
        l_ref[...] = jnp.zeros_like(l_ref)
        acc_ref[...] = jnp.zeros_like(acc_ref)

    lo = _lo_mask()
    odd = (kvh % 2) == 1
    kd = _dup_half(k_ref[...], odd).astype(BF16)
    vd = _dup_half(v_ref[...], odd).astype(BF16)
    ones = jnp.ones((tk, LANES), BF16)
    if mode == "slc":
        nsp = sel_ref.shape[0]
        blk = kj * (tk // SEL_BLOCK) + lax.shift_right_logical(_iota((nsp, tk), 1), 6)
        expand = jnp.where(_iota((nsp, tk), 0) == blk, 1.0, 0.0).astype(BF16)
    rc = min(tq, LANES)
    units, valids = [], []
    for r0 in range(0, tq, rc):
        rows = pl.ds(r0, rc)
        d = (qi * tq + r0 + _iota((rc, tk), 0)) - (kj * tk + _iota((rc, tk), 1))
        if mode == "slc":
            chosen = _dot(sel_ref[:, rows].astype(BF16), expand, TN)
            valids.append((chosen > 0.5) & (d >= 0))
        else:
            valids.append((d >= 0) & (d < WINDOW))
        for g, qm in enumerate(_head_queries(q_ref[rows, :], lo)):
            units.append((g, rows, qm, len(valids) - 1))
    sc = [_dot(qm, kd, NT) + jnp.where(valids[v], bias_ref[g, rows, :], NEG) for g, rows, qm, v in units]
    m_old = [m_ref[g, rows, :] for g, rows, _, _ in units]
    m_new = [jnp.maximum(mo, jnp.broadcast_to(jnp.max(x, axis=1, keepdims=True), mo.shape))
             for mo, x in zip(m_old, sc)]
    p = [jnp.exp(x - jnp.concatenate([mn] * (tk // LANES), axis=1)).astype(BF16) for x, mn in zip(sc, m_new)]
    alpha = [jnp.exp(mo - mn) for mo, mn in zip(m_old, m_new)]
    rowsum = [_dot(x, ones) for x in p]
    pv = [_dot(x, vd) for x in p]
    for (g, rows, _, _), a, rs, o, mn in zip(units, alpha, rowsum, pv, m_new):
        l_ref[g, rows, :] = a * l_ref[g, rows, :] + rs
        acc_ref[g, rows, :] = a * acc_ref[g, rows, :] + o
        m_ref[g, rows, :] = mn

    @pl.when(last_ref[s] == 1)
    def _():
        outs = []
        for g in range(4):
            den = l_ref[g]
            outs.append(acc_ref[g] / jnp.where(den > 0, den, 1.0))
        o_ref[...] = _pair_out(outs, lo)


def flash_prompt(nq, kv, bias_tiles, sel, *, B, T, mode):
    tq = tk = bias_tiles.shape[2]
    nq_t = T // tq
    nd = bias_tiles.shape[1]
    lookback = None if mode == "slc" else -(-(WINDOW - 1) // tk)
    qs, ks, first, last = _tri_steps(nq_t, lookback)
    dl = jnp.minimum(qs - ks, nd - 1)
    n_steps = int(qs.shape[0])
    in_specs = [pl.BlockSpec((tq, 2 * LANES), lambda b, k, s, qi, kj, dl, f, l: (b * nq_t + qi[s], k)),
                pl.BlockSpec((tk, LANES), lambda b, k, s, qi, kj, dl, f, l: (b * nq_t + kj[s], k // 2)),
                pl.BlockSpec((tk, LANES), lambda b, k, s, qi, kj, dl, f, l: (b * nq_t + kj[s], 2 + k // 2)),
                pl.BlockSpec((4, None, tq, tk), lambda b, k, s, qi, kj, dl, f, l: (k, dl[s], 0, 0))]
    args = [nq, kv, kv, bias_tiles]
    if mode == "slc":
        nsp = sel.shape[2]
        in_specs.append(pl.BlockSpec((None, None, nsp, tq), lambda b, k, s, qi, kj, dl, f, l: (b, k, 0, qi[s])))
        args.append(sel)
    kern = functools.partial(_flash_kernel, tq=tq, tk=tk, mode=mode)
    return pl.pallas_call(
        kern,
        grid_spec=pltpu.PrefetchScalarGridSpec(
            num_scalar_prefetch=5,
            grid=(B, 4, n_steps),
            in_specs=in_specs,
            out_specs=pl.BlockSpec((tq, 2 * LANES), lambda b, k, s, qi, kj, dl, f, l: (b * nq_t + qi[s], k)),
            scratch_shapes=[pltpu.VMEM((4, tq, LANES), F32), pltpu.VMEM((4, tq, LANES), F32),
                            pltpu.VMEM((4, tq, LANES), F32)]),
        out_shape=jax.ShapeDtypeStruct((B * T, 1024), F32),
        compiler_params=_cparams("parallel", "parallel", "arbitrary"),
        name="nsa_" + mode,
    )(qs, ks, dl, first, last, *args)


def _kv_queries(q8, n_heads, n_kv):
    qbd, _ = _block_diag_rows(q8, n_heads)
    wq, wk = n_heads * HEAD_DIM, n_kv * HEAD_DIM
    gshift = int(math.log2(n_heads // n_kv)) + 6
    r = _iota((wq, wk), 0)
    c = _iota((wq, wk), 1)
    fold = jnp.where((lax.shift_right_logical(r, gshift) == lax.shift_right_logical(c, 6))
                     & (jnp.bitwise_and(r, 63) == jnp.bitwise_and(c, 63)), 1.0, 0.0).astype(BF16)
    return _dot(qbd.astype(BF16), fold).astype(BF16)


def _kv_outputs(o, n_heads, n_kv):
    rows = n_heads * ROW_GROUP
    wq, wk = n_heads * HEAD_DIM, n_kv * HEAD_DIM
    gshift = int(math.log2(n_heads // n_kv))
    own_kv = lax.shift_right_logical(_iota((rows, wk), 0), 3 + gshift) == lax.shift_right_logical(_iota((rows, wk), 1), 6)
    r = _iota((wk, wq), 0)
    c = _iota((wk, wq), 1)
    unfold = jnp.where((lax.shift_right_logical(c, gshift + 6) == lax.shift_right_logical(r, 6))
                       & (jnp.bitwise_and(r, 63) == jnp.bitwise_and(c, 63)), 1.0, 0.0).astype(BF16)
    wide = _dot_hl(jnp.where(own_kv, o, 0.0), unfold)
    own = lax.shift_right_logical(_iota((rows, wq), 0), 3) == lax.shift_right_logical(_iota((rows, wq), 1), 6)
    return jnp.sum(jnp.where(own, wide, 0.0).reshape(n_heads, ROW_GROUP, wq), axis=0)


def _softmax_step(sc, vb, m_ref, l_ref, acc_ref, v_dims=NN):
    m_old = m_ref[...]
    m_new = jnp.maximum(m_old, jnp.broadcast_to(jnp.max(sc, axis=1, keepdims=True), m_old.shape))
    p = jnp.exp(sc - jnp.concatenate([m_new] * (sc.shape[1] // LANES), axis=1)).astype(BF16)
    alpha = jnp.exp(m_old - m_new)
    l_ref[...] = alpha * l_ref[...] + _dot(p, jnp.ones((sc.shape[1], LANES), BF16))
    acc = acc_ref[...]
    acc_ref[...] = jnp.concatenate([alpha] * (acc.shape[1] // LANES), axis=1) * acc + _dot(p, vb, v_dims)
    m_ref[...] = m_new


def _new_rows(ref, cols):
    blk = ref[:, cols]
    return jnp.concatenate([blk, jnp.zeros((LANES - ROW_GROUP, blk.shape[1]), F32)], axis=0).astype(BF16)


def _slc_decode_kernel(pt_ref, q_ref, kvn_ref, sel_ref, bias_ref, *rest, n_heads, n_kv, n_pages, n_new, ppb):
    page_refs = rest[:ppb]
    o_ref, m_ref, l_ref, acc_ref, qbd_ref = rest[ppb:]
    p = pl.program_id(1)
    rows = n_heads * ROW_GROUP
    wk = n_kv * HEAD_DIM
    group = n_heads // n_kv
    nsp = sel_ref.shape[2]
    sel_rows = jnp.concatenate([sel_ref[k] for k in range(n_kv) for _ in range(group)], axis=0).astype(BF16)

    def chosen(first_blk, n_keys):
        blk = first_blk + lax.shift_right_logical(_iota((nsp, n_keys), 1), 6)
        expand = jnp.where(_iota((nsp, n_keys), 0) == blk, 1.0, 0.0).astype(BF16)
        return _dot(sel_rows, expand) > 0.5

    @pl.when(p == 0)
    def _():
        m_ref[...] = jnp.full_like(m_ref, M_FLOOR)
        l_ref[...] = jnp.zeros_like(l_ref)
        acc_ref[...] = jnp.zeros_like(acc_ref)
        qbd = _kv_queries(q_ref[...].astype(F32), n_heads, n_kv)
        qbd_ref[...] = qbd
        r_in = jnp.bitwise_and(_iota((rows, LANES), 0), ROW_GROUP - 1)
        col = _iota((rows, LANES), 1)
        kb = _new_rows(kvn_ref, slice(0, wk))
        vb = _new_rows(kvn_ref, slice(wk, 2 * wk))
        valid = (chosen(n_pages * (LANES // SEL_BLOCK), LANES)
                 & (col >= ROW_GROUP - n_new) & (col < ROW_GROUP) & (col <= r_in))
        sc = _dot(qbd, kb, NT) + jnp.where(valid, bias_ref[:, pl.ds(0, LANES)], NEG)
        _softmax_step(sc, vb, m_ref, l_ref, acc_ref)

    @pl.when(p > 0)
    def _():
        kt = jnp.concatenate([ref[pl.ds(0, wk), :] for ref in page_refs], axis=1).astype(BF16)
        vt = jnp.concatenate([ref[pl.ds(wk, wk), :] for ref in page_refs], axis=1).astype(BF16)
        base = n_pages - p * ppb
        valid = chosen(base * (LANES // SEL_BLOCK), ppb * LANES)
        sc = _dot(qbd_ref[...], kt) + jnp.where(valid, bias_ref[...], NEG)
        _softmax_step(sc, vt, m_ref, l_ref, acc_ref, NT)

    @pl.when(p == n_pages // ppb)
    def _():
        den = l_ref[:, pl.ds(0, 1)]
        o_ref[...] = _kv_outputs(acc_ref[...] / jnp.where(den > 0, den, 1.0), n_heads, n_kv)


def slc_decode(nq, kv_new, sel, bias, cache, page_table, layer, n_new):
    S, n_pages = page_table.shape
    n_kv = cache.shape[2] // (2 * HEAD_DIM)
    n_heads = nq.shape[1] // HEAD_DIM
    rows = n_heads * ROW_GROUP
    nsp = sel.shape[3]
    pt = page_table.reshape(-1).astype(jnp.int32)
    ppb = (bias.shape[1] - n_pages * LANES) // LANES
    assert n_pages % ppb == 0
    n_steps = 1 + n_pages // ppb

    def page_map(i):
        def index(s, p, pt_ref):
            logical = n_pages - jnp.maximum(p, 1) * ppb + i
            return (pt_ref[s * n_pages + logical], layer, 0, 0)
        return index

    kern = functools.partial(_slc_decode_kernel, n_heads=n_heads, n_kv=n_kv, n_pages=n_pages, n_new=n_new,
                             ppb=ppb)
    return pl.pallas_call(
        kern,
        grid_spec=pltpu.PrefetchScalarGridSpec(
            num_scalar_prefetch=1,
            grid=(S, n_steps),
            in_specs=[pl.BlockSpec((ROW_GROUP, nq.shape[1]), lambda s, p, pt_ref: (s, 0)),
                      pl.BlockSpec((ROW_GROUP, kv_new.shape[1]), lambda s, p, pt_ref: (s, 0)),
                      pl.BlockSpec((None, n_kv, ROW_GROUP, nsp), lambda s, p, pt_ref: (s, 0, 0, 0)),
                      pl.BlockSpec((rows, ppb * LANES),
                                   lambda s, p, pt_ref: (0, jnp.where(p == 0, n_pages // ppb, n_pages // ppb - p)))]
            + [pl.BlockSpec((None, None, 2 * n_kv * HEAD_DIM, LANES), page_map(i)) for i in range(ppb)],
            out_specs=pl.BlockSpec((ROW_GROUP, nq.shape[1]), lambda s, p, pt_ref: (s, 0)),
            scratch_shapes=[pltpu.VMEM((rows, LANES), F32), pltpu.VMEM((rows, LANES), F32),
                            pltpu.VMEM((rows, n_kv * HEAD_DIM), F32),
                            pltpu.VMEM((rows, n_kv * HEAD_DIM), BF16)]),
        out_shape=jax.ShapeDtypeStruct((S * ROW_GROUP, nq.shape[1]), F32),
        compiler_params=_cparams("parallel", "arbitrary"),
        name="slc_decode",
    )(pt, nq, kv_new, sel, bias, *([cache] * ppb))


def _win_decode_kernel(q_ref, kvn_ref, bias_ref, kv_ref, o_ref, m_ref, l_ref, acc_ref, *,
                       n_heads, n_kv, n_new, layer_rows):
    rows = n_heads * ROW_GROUP
    wk = n_kv * HEAD_DIM
    m_ref[...] = jnp.full_like(m_ref, M_FLOOR)
    l_ref[...] = jnp.zeros_like(l_ref)
    acc_ref[...] = jnp.zeros_like(acc_ref)
    qbd = _kv_queries(q_ref[...].astype(F32), n_heads, n_kv)
    i_q = jnp.bitwise_and(_iota((rows, layer_rows), 0), ROW_GROUP - 1) - (ROW_GROUP - n_new)
    j = _iota((rows, layer_rows), 1)
    dist = layer_rows + i_q - j
    valid = (dist >= 0) & (dist < WINDOW)
    kb = kv_ref[:, pl.ds(0, wk)].astype(BF16)
    vb = kv_ref[:, pl.ds(wk, wk)].astype(BF16)
    sc = _dot(qbd, kb, NT) + jnp.where(valid, bias_ref[:, pl.ds(0, layer_rows)], NEG)
    _softmax_step(sc, vb, m_ref, l_ref, acc_ref)
    r_in = jnp.bitwise_and(_iota((rows, LANES), 0), ROW_GROUP - 1)
    col = _iota((rows, LANES), 1)
    valid = (col >= ROW_GROUP - n_new) & (col < ROW_GROUP) & (col <= r_in)
    kb = _new_rows(kvn_ref, slice(0, wk))
    vb = _new_rows(kvn_ref, slice(wk, 2 * wk))
    sc = _dot(qbd, kb, NT) + jnp.where(valid, bias_ref[:, pl.ds(layer_rows, LANES)], NEG)
    _softmax_step(sc, vb, m_ref, l_ref, acc_ref)
    den = l_ref[:, pl.ds(0, 1)]
    o_ref[...] = _kv_outputs(acc_ref[...] / jnp.where(den > 0, den, 1.0), n_heads, n_kv)


def win_decode(nq, kv_new, bias, cache_win, layer, n_new):
    S, _, wc, width = cache_win.shape
    n_kv = width // (2 * HEAD_DIM)
    n_heads = nq.shape[1] // HEAD_DIM
    rows = n_heads * ROW_GROUP
    kern = functools.partial(_win_decode_kernel, n_heads=n_heads, n_kv=n_kv, n_new=n_new, layer_rows=wc)
    return pl.pallas_call(
        kern,
        grid=(S,),
        in_specs=[pl.BlockSpec((ROW_GROUP, nq.shape[1]), lambda s: (s, 0)),
                  pl.BlockSpec((ROW_GROUP, width), lambda s: (s, 0)),
                  pl.BlockSpec((rows, wc + LANES), lambda s: (0, 0)),
                  pl.BlockSpec((None, None, wc, width), lambda s: (s, layer, 0, 0))],
        out_specs=pl.BlockSpec((ROW_GROUP, nq.shape[1]), lambda s: (s, 0)),
        out_shape=jax.ShapeDtypeStruct((S * ROW_GROUP, nq.shape[1]), F32),
        scratch_shapes=[pltpu.VMEM((rows, LANES), F32), pltpu.VMEM((rows, LANES), F32),
                        pltpu.VMEM((rows, n_kv * HEAD_DIM), F32)],
        compiler_params=_cparams("parallel"),
        name="win_decode",
    )(nq, kv_new, bias, cache_win)


def _nsa_combine_kernel(oc_ref, os_ref, ow_ref, gate_ref, o_ref, *, lane0, n_heads):
    hi, lw = _split(jax.nn.sigmoid(gate_ref[...]))
    width = n_heads * HEAD_DIM
    acc = jnp.zeros(oc_ref.shape, F32)
    for br, ref in enumerate((oc_ref, os_ref, ow_ref)):
        src = lane0 + br * n_heads + lax.shift_right_logical(_iota((LANES, width), 1), 6)
        expand = jnp.where(_iota((LANES, width), 0) == src, 1.0, 0.0).astype(BF16)
        acc = acc + (_dot(hi, expand) + _dot(lw, expand)) * ref[...]
    o_ref[...] = acc.astype(o_ref.dtype)


def nsa_combine(o_cmp, o_slc, o_win, blk_c, *, gate_block, lane0):
    M, width = o_cmp.shape
    tm = _pick(M, (512, 256, 128, 64, 32, 16, 8))
    kern = functools.partial(_nsa_combine_kernel, lane0=lane0, n_heads=width // HEAD_DIM)
    ospec = pl.BlockSpec((tm, width), lambda i: (i, 0))
    return pl.pallas_call(
        kern,
        grid=(M // tm,),
        in_specs=[ospec, ospec, ospec, pl.BlockSpec((tm, LANES), lambda i: (i, gate_block))],
        out_specs=ospec,
        out_shape=jax.ShapeDtypeStruct((M, width), BF16),
        compiler_params=_cparams("parallel"),
        name="nsa_combine",
    )(o_cmp, o_slc, o_win, blk_c)


RW_COLS = 3584


def _rwkv_prep_kernel(c_ref, head_ref, mu_ref, vec_ref, w2_ref, a2_ref, g2_ref,
                      r_ref, lw_ref, k_ref, v_ref, kk_ref, b_ref, bonus_ref, g_ref,
                      xs_ref, carry_ref, *, tm, tps, whole, n_new):
    i = pl.program_id(0)
    cols = c_ref[...]
    if whole:
        cols = cols + head_ref[...]
        head = jnp.zeros((SUBLANES, RW_COLS), F32)
    else:
        head = jnp.where(i % tps == 0, head_ref[0], carry_ref[...])
    xs_ref[pl.ds(0, SUBLANES), :] = head
    xs_ref[pl.ds(SUBLANES, tm), :] = cols
    prev = xs_ref[pl.ds(SUBLANES - 1, tm), :]
    if not whole:
        carry_ref[...] = xs_ref[pl.ds(tm, SUBLANES), :]
    xs_ref[pl.ds(SUBLANES, tm), :] = cols + (prev - cols) * mu_ref[...]
    small = xs_ref[pl.ds(SUBLANES, tm), pl.ds(3328, 256)]
    th = jnp.tanh(small).astype(BF16)
    sm = small.astype(BF16)
    sg = jax.nn.sigmoid(xs_ref[pl.ds(SUBLANES, tm), pl.ds(3072, 256)]).astype(BF16)
    seg = _seg_mat(LANES, HEAD_DIM, 1.0)
    if whole:
        real = jnp.bitwise_and(_iota((tm, LANES), 0), ROW_GROUP - 1) >= ROW_GROUP - n_new
    for j in range(8):
        cs = pl.ds(j * LANES, LANES)
        r = xs_ref[pl.ds(SUBLANES, tm), pl.ds(j * LANES, LANES)]
        k = xs_ref[pl.ds(SUBLANES, tm), pl.ds(1024 + j * LANES, LANES)]
        v = xs_ref[pl.ds(SUBLANES, tm), pl.ds(2048 + j * LANES, LANES)]
        y = vec_ref[0:1, cs] + _dot(th, w2_ref[:, cs])
        w_log = -(jnp.maximum(-y, 0.0) + jnp.log(1.0 + jnp.exp(-jnp.abs(y)))) - 0.5
        lw = -jnp.exp(w_log)
        a = jax.nn.sigmoid(vec_ref[1:2, cs] + _dot(sm, a2_ref[:, cs]))
        g = _dot(sg, g2_ref[:, cs])
        kk = k * vec_ref[2:3, cs]
        kk = kk * lax.rsqrt(jnp.maximum(_dot_hl(kk * kk, seg), 1e-24))
        k2 = k * (1.0 + (a - 1.0) * vec_ref[3:4, cs])
        b = kk * a
        bonus = _dot_hl(r * k2 * vec_ref[4:5, cs], seg) * v
        if whole:
            r, k2, v, kk, b, lw = [jnp.where(real, t, 0.0) for t in (r, k2, v, kk, b, lw)]
        r_ref[:, cs] = r
        lw_ref[:, cs] = lw
        k_ref[:, cs] = k2
        v_ref[:, cs] = v
        kk_ref[:, cs] = kk
        b_ref[:, cs] = b
        bonus_ref[:, cs] = bonus
        g_ref[:, cs] = g


def rwkv_prep(blk_c, head, mu, vec, w2p, a2p, g2, *, seq_rows, whole, n_new=0):
    M = blk_c.shape[0]
    if whole:
        tm, tps = M, 1
        head_spec = pl.BlockSpec((tm, RW_COLS), lambda i: (0, 0))
    else:
        tm = _pick(seq_rows, (256, 128, 64))
        tps = seq_rows // tm
        head_spec = pl.BlockSpec((1, SUBLANES, RW_COLS), lambda i: (i // tps, 0, 0))
    kern = functools.partial(_rwkv_prep_kernel, tm=tm, tps=tps, whole=whole, n_new=n_new)
    full = lambda shape: pl.BlockSpec(shape, lambda i: (0,) * len(shape))
    return pl.pallas_call(
        kern,
        grid=(M // tm,),
        in_specs=[pl.BlockSpec((tm, RW_COLS), lambda i: (i, 0)), head_spec, full((1, RW_COLS)),
                  full((SUBLANES, 1024)), full((256, 1024)), full((256, 1024)), full((256, 1024))],
        out_specs=[pl.BlockSpec((tm, 1024), lambda i: (i, 0))] * 8,
        out_shape=[jax.ShapeDtypeStruct((M, 1024), F32)] * 8,
        scratch_shapes=[pltpu.VMEM((tm + SUBLANES, RW_COLS), F32), pltpu.VMEM((SUBLANES, RW_COLS), F32)],
        compiler_params=_cparams("arbitrary"),
        name="rwkv_prep",
    )(blk_c, head, mu, vec, w2p, a2p, g2)


def _rwkv_par_kernel(r_ref, lw_ref, k_ref, v_ref, kk_ref, b_ref, r2_ref, y2_ref, m_ref, g_ref, *, C, npair):
    C2 = 2 * C
    cum = jnp.where(_iota((C, C), 0) >= _iota((C, C), 1), 1.0, 0.0).astype(BF16)
    keep = (_iota((C2, LANES), 0) < C) == (_iota((C2, LANES), 1) < HEAD_DIM)
    rb = _iota((C2, C2), 0)
    cb = _iota((C2, C2), 1)
    same = (rb < C) == (cb < C)
    rr = jnp.bitwise_and(rb, C - 1)
    cc = jnp.bitwise_and(cb, C - 1)
    strict = same & (rr > cc)
    incl = same & (rr >= cc)
    eye2 = jnp.where(rb == cb, 1.0, 0.0)
    eye_l = _iota((LANES, LANES), 0) == _iota((LANES, LANES), 1)

    def stack(x):
        return jnp.where(keep, jnp.concatenate([x, x], axis=0), 0.0)

    group = 8
    for p0 in range(0, npair, group):
        ps = range(p0, min(p0 + group, npair))
        cols = [pl.ds(p * LANES, LANES) for p in ps]
        n = len(cols)
        lw = [lw_ref[:, cs] for cs in cols]
        log_p = [_dot_hl_rhs(cum, x) for x in lw]
        log_end = [x[C - 1:C, :] for x in log_p]
        e_pos = [jnp.exp(x) for x in log_p]
        e_neg = [jnp.exp(-x) for x in log_p]
        e_end = [jnp.exp(le - x) for le, x in zip(log_end, log_p)]
        k = [k_ref[:, cs] for cs in cols]
        b = [b_ref[:, cs] for cs in cols]
        rt = [stack(r_ref[:, cs] * e) for cs, e in zip(cols, e_pos)]
        kt = [stack(x * e) for x, e in zip(k, e_neg)]
        bt = [stack(x * e) for x, e in zip(b, e_neg)]
        at = [stack(-kk_ref[:, cs] * jnp.exp(x - y)) for cs, x, y in zip(cols, log_p, lw)]
        kendb = [stack(x * e).astype(BF16) for x, e in zip(k, e_end)]
        bendb = [stack(x * e).astype(BF16) for x, e in zip(b, e_end)]
        v2b = [stack(v_ref[:, cs]).astype(BF16) for cs in cols]
        atb = [x.astype(BF16) for x in at]
        cross = [_dot(jnp.concatenate([atb[i], rt[i].astype(BF16)], axis=0),
                      jnp.concatenate([kt[i], bt[i]], axis=0).astype(BF16), NT) for i in range(n)]
        a_ak = [jnp.where(strict, x[:C2, :C2], 0.0).astype(BF16) for x in cross]
        a_ab = [jnp.where(strict, x[:C2, C2:], 0.0) for x in cross]
        a_rk = [jnp.where(incl, x[C2:, :C2], 0.0).astype(BF16) for x in cross]
        a_rb = [jnp.where(incl, x[C2:, C2:], 0.0).astype(BF16) for x in cross]
        akv = [_dot(x, y).astype(BF16) for x, y in zip(a_ak, v2b)]
        inv = [eye2 + x for x in a_ab]
        powr = [x.astype(BF16) for x in a_ab]
        for _ in range(int(math.log2(C)) - 1):
            powr = [_dot(x, x).astype(BF16) for x in powr]
            inv = [x + _dot(x.astype(BF16), y) for x, y in zip(inv, powr)]
        aub = [_dot(x.astype(BF16), jnp.concatenate([y, z], axis=1)).astype(BF16)
               for x, y, z in zip(inv, atb, akv)]
        ry = [_dot(x, y) for x, y in zip(a_rb, aub)]
        rkv = [_dot(x, y) for x, y in zip(a_rk, v2b)]
        mg = [_dot(x, y, TN) for x, y in zip(aub, bendb)]
        vk = [_dot(x, y, TN) for x, y in zip(v2b, kendb)]
        for i, cs in enumerate(cols):
            r2_ref[:, cs] = rt[i] + ry[i][:, :LANES]
            y2_ref[:, cs] = rkv[i] + ry[i][:, LANES:]
            decay_end = jnp.where(eye_l, jnp.broadcast_to(jnp.exp(log_end[i]), (LANES, LANES)), 0.0)
            m_ref[:, cs] = decay_end + mg[i][:LANES]
            g_ref[:, cs] = vk[i] + mg[i][LANES:]


def rwkv_par(r, lw, k, v, kk, b, C):
    M, W = r.shape
    npair = W // LANES
    nch = M // C
    kern = functools.partial(_rwkv_par_kernel, C=C, npair=npair)
    ispec = pl.BlockSpec((C, W), lambda i: (i, 0))
    return pl.pallas_call(
        kern,
        grid=(nch,),
        in_specs=[ispec] * 6,
        out_specs=[pl.BlockSpec((None, 2 * C, W), lambda i: (i, 0, 0))] * 2
        + [pl.BlockSpec((None, LANES, W), lambda i: (i, 0, 0))] * 2,
        out_shape=[jax.ShapeDtypeStruct((nch, 2 * C, W), F32)] * 2
        + [jax.ShapeDtypeStruct((nch, LANES, W), F32)] * 2,
        compiler_params=_cparams("parallel"),
        name="rwkv_par",
    )(r, lw, k, v, kk, b)


def _rwkv_seq_kernel(r2_ref, y2_ref, m_ref, g_ref, s0_ref, bonus_ref, gate_ref, ln_ref,
                     o_ref, sf_ref, s_ref, *, C, npair):
    c = pl.program_id(1)

    @pl.when(c == 0)
    def _():
        s_ref[...] = s0_ref[...]

    seg = _seg_mat(LANES, HEAD_DIM, 1.0 / HEAD_DIM)
    cols = [pl.ds(p * LANES, LANES) for p in range(npair)]
    st = [s_ref[:, cs] for cs in cols]
    y2 = [_dot3(r2_ref[:, cs], x, NT) + y2_ref[:, cs] for cs, x in zip(cols, st)]
    s_new = [_dot3(x, m_ref[:, cs]) + g_ref[:, cs] for cs, x in zip(cols, st)]
    for cs, x in zip(cols, s_new):
        s_ref[:, cs] = x
    y = [x[:C] + x[C:] for x in y2]
    mu = [_dot_hl(x, seg) for x in y]
    dev = [x - m for x, m in zip(y, mu)]
    var = [_dot_hl(x * x, seg) for x in dev]
    for cs, d, v in zip(cols, dev, var):
        yn = d * lax.rsqrt(v + GN_EPS) * ln_ref[0:1, cs] + ln_ref[1:2, cs]
        o_ref[:, cs] = ((yn + bonus_ref[:, cs]) * gate_ref[:, cs]).astype(o_ref.dtype)

    @pl.when(c == pl.num_programs(1) - 1)
    def _():
        sf_ref[...] = s_ref[...]


def rwkv_seq(r2, y2, mt, gt, s0, bonus, gate, ln, *, n_seq):
    nch_total, C2, W = r2.shape
    C = C2 // 2
    nch = nch_total // n_seq
    npair = W // LANES
    kern = functools.partial(_rwkv_seq_kernel, C=C, npair=npair)
    cspec = lambda rows: pl.BlockSpec((None, rows, W), lambda s, c: (s * nch + c, 0, 0))
    return pl.pallas_call(
        kern,
        grid=(n_seq, nch),
        in_specs=[cspec(C2), cspec(C2), cspec(LANES), cspec(LANES),
                  pl.BlockSpec((None, LANES, W), lambda s, c: (s, 0, 0)),
                  pl.BlockSpec((C, W), lambda s, c: (s * nch + c, 0)),
                  pl.BlockSpec((C, W), lambda s, c: (s * nch + c, 0)),
                  pl.BlockSpec((SUBLANES, W), lambda s, c: (0, 0))],
        out_specs=[pl.BlockSpec((C, W), lambda s, c: (s * nch + c, 0)),
                   pl.BlockSpec((None, LANES, W), lambda s, c: (s, 0, 0))],
        out_shape=[jax.ShapeDtypeStruct((nch_total * C, W), BF16),
                   jax.ShapeDtypeStruct((n_seq, LANES, W), F32)],
        scratch_shapes=[pltpu.VMEM((LANES, W), F32)],
        compiler_params=_cparams("parallel", "arbitrary"),
        name="rwkv_seq",
    )(r2, y2, mt, gt, s0, bonus, gate, ln)


def _bucket_np(d):
    d = np.maximum(d, 0)
    ratio = np.log(np.maximum(d, 1).astype(np.float32) / np.float32(N_BUCKETS // 2)) / np.float32(
        math.log(MAX_DISTANCE / (N_BUCKETS // 2)))
    large = np.minimum(N_BUCKETS // 2 + (ratio * np.float32(N_BUCKETS - N_BUCKETS // 2)).astype(np.int32),
                       N_BUCKETS - 1)
    return np.where(d < N_BUCKETS // 2, d, large).astype(np.int32)


def _overlap_np(n_cmp, n_sel):
    c0 = np.arange(n_cmp)[:, None] * CMP_STRIDE
    j0 = np.arange(n_sel)[None, :] * SEL_BLOCK
    ov = np.clip(np.minimum(c0 + CMP_BLOCK, j0 + SEL_BLOCK) - np.maximum(c0, j0), 0, None)
    return (ov / CMP_BLOCK).astype(np.float32)


def _bias_table_kernel(tab_ref, idx_ref, o_ref):
    h = pl.program_id(0)
    idx = idx_ref[...]
    acc = jnp.zeros(idx.shape, F32)
    for b in range(N_BUCKETS):
        acc = jnp.where(idx == b, tab_ref[h, b], acc)
    o_ref[...] = acc


def bias_table(tab_h, idx):
    H = tab_h.shape[0]
    R, C = idx.shape
    tr = _pick(R, (256, 128, 64, 32, 16, 8))
    return pl.pallas_call(
        _bias_table_kernel,
        grid=(H, R // tr),
        in_specs=[pl.BlockSpec(memory_space=pltpu.SMEM),
                  pl.BlockSpec((tr, C), lambda h, i: (i, 0))],
        out_specs=pl.BlockSpec((None, tr, C), lambda h, i: (h, i, 0)),
        out_shape=jax.ShapeDtypeStruct((H, R, C), F32),
        compiler_params=_cparams("parallel", "parallel"),
        name="bias_table",
    )(tab_h, idx)


def _make_tables(rel_bias, T, past, n_new, n_heads):
    tab_h = rel_bias.astype(F32).T

    def take(idx):
        idx = np.asarray(idx, np.int32)
        out = bias_table(tab_h, jnp.asarray(idx.reshape(-1, idx.shape[-1])))
        return out.reshape((tab_h.shape[0],) + idx.shape)

    t = {}
    tq = _pick(T, (256, 128))
    nq_t = T // tq
    far = 0
    while _bucket_np(np.array([far]))[0] < N_BUCKETS - 1:
        far += 1
    nd = min(-(-(far + tq - 1) // tq) + 1, nq_t)
    i = np.arange(tq)[:, None]
    j = np.arange(tq)[None, :]
    t["tiles"] = take(np.stack([_bucket_np(dl * tq + i - j) for dl in range(nd)]))
    ncp = T // CMP_STRIDE
    tt = np.arange(T)[:, None]
    cc = np.arange(ncp)[None, :]
    t["cmp_p"] = take(_bucket_np(tt - (cc * CMP_STRIDE + CMP_BLOCK - 1)))
    n_sel = -(-T // SEL_BLOCK)
    nsp = -(-n_sel // SUBLANES) * SUBLANES
    ov = np.zeros((nsp, ncp), np.float32)
    ov[:n_sel, :ncp - 1] = _overlap_np(ncp - 1, n_sel).T
    t["ov_p"] = jnp.asarray(ov, BF16)
    t["n_sel_p"] = n_sel
    t["tq"] = tq
    rows = n_heads * ROW_GROUP
    tpos = past - (ROW_GROUP - n_new) + np.arange(ROW_GROUP)
    ppb = _pick(past // LANES, (4, 2, 1))
    kpos = np.concatenate([np.arange(past), past - (ROW_GROUP - n_new) + np.arange(LANES),
                           np.zeros((ppb - 1) * LANES, np.int64)])
    idx = _bucket_np(tpos[:, None] - kpos[None, :])
    t["slc_d"] = take(idx).reshape(rows, past + ppb * LANES)
    wc = min(WINDOW, past)
    kpos = np.concatenate([past - wc + np.arange(wc), past - (ROW_GROUP - n_new) + np.arange(LANES)])
    t["win_d"] = take(_bucket_np(tpos[:, None] - kpos[None, :])).reshape(rows, wc + LANES)
    ncp_d = past // CMP_STRIDE
    cc = np.arange(ncp_d)[None, :]
    t["cmp_d"] = take(_bucket_np(tpos[:, None] - (cc * CMP_STRIDE + CMP_BLOCK - 1)))
    n_sel_d = -(-(past + n_new) // SEL_BLOCK)
    nsp_d = -(-n_sel_d // LANES) * LANES
    n_cmp_d = (past + n_new) // CMP_STRIDE - 1
    ov = np.zeros((ncp_d, nsp_d), np.float32)
    ov[:n_cmp_d, :n_sel_d] = _overlap_np(n_cmp_d, n_sel_d)
    t["ov_d"] = jnp.asarray(ov, BF16)
    t["n_sel_d"] = n_sel_d
    return t


def _prep_layer(l, P):
    W = {}
    w_in = P["w_in"][l]
    sbw = 1024
    W["wA"] = jnp.concatenate([w_in[:, :sbw] * HEAD_DIM ** -0.5, w_in[:, sbw:3 * sbw]], axis=1).astype(BF16)
    W["wB"] = w_in[:, 3072:5632].astype(BF16)
    D = w_in.shape[0]
    W["wC"] = jnp.concatenate([w_in[:, 5680:8752], w_in[:, 8944:9200], w_in[:, 8752:8944],
                               w_in[:, 5632:5680], jnp.zeros((D, 16), F32)], axis=1).astype(BF16)
    W["wD"] = w_in[:, 9200:].astype(BF16)
    W["norm_mix"] = P["norm_mix_g"][l]
    W["norm_ffn"] = P["norm_ffn_g"][l]
    gains = P["nsa_qk_gain"][l]
    W["gains"] = jnp.pad(jnp.tile(gains, (1, 2)), ((0, 4), (0, 0)))
    W["gain1"] = jnp.tile(gains[1:2], (1, 2))
    w1 = P["cmp_w1"][l].reshape(2, 2, CMP_STRIDE, HEAD_DIM, HEAD_DIM)
    first, second = w1[:, 0], w1[:, 1]
    z = jnp.zeros_like(first)
    W["w1c"] = jnp.concatenate([jnp.concatenate([first, z, second, z], axis=-1),
                                jnp.concatenate([z, first, z, second], axis=-1)], axis=-2).astype(BF16)
    pos = jnp.einsum("cld,clde->ce", P["cmp_pe"][l], P["cmp_w1"][l], precision=lax.Precision.HIGHEST)
    W["posb"] = jnp.pad(jnp.tile(pos + P["cmp_b1"][l], (1, 2)), ((0, 6), (0, 0)))
    w2 = P["cmp_w2"][l]
    w2d = jnp.concatenate([w2, w2], axis=-1)
    z2 = jnp.zeros_like(w2d)
    W["w2dup"] = jnp.stack([jnp.concatenate([w2d, z2], axis=1), jnp.concatenate([z2, w2d], axis=1)],
                           axis=1).astype(BF16)
    W["b2"] = jnp.pad(jnp.tile(P["cmp_b2"][l], (1, 2)), ((0, 6), (0, 0)))
    mu = P["rwkv_mu"][l]
    W["mu"] = jnp.concatenate([mu[:3072], mu[3264:3520], mu[3072:3264], jnp.zeros((64,), F32)])[None]
    W["vec"] = jnp.pad(jnp.stack([P["rwkv_w0"][l], P["rwkv_a0"][l], P["rwkv_kk"][l], P["rwkv_ka"][l],
                                  P["rwkv_rk"][l].reshape(-1)]), ((0, 3), (0, 0)))
    W["w2p"] = jnp.pad(P["rwkv_w2"][l], ((0, 160), (0, 0))).astype(BF16)
    W["a2p"] = jnp.pad(P["rwkv_a2"][l], ((96, 64), (0, 0))).astype(BF16)
    W["g2"] = P["rwkv_g2"][l].astype(BF16)
    W["ln"] = jnp.pad(jnp.stack([P["rwkv_lnx_w"][l], P["rwkv_lnx_b"][l]]), ((0, 6), (0, 0)))
    W["wb"] = P["w_branch"][l].astype(BF16)
    W["wout"] = P["w_out"][l].astype(BF16)
    F = P["ffn_conv_w"].shape[-1]
    Fp = -(-F // 512) * 512
    padc = lambda a: jnp.pad(a, ((0, 0), (0, Fp - F)))
    w_up = P["ffn_w_up"][l]
    W["wu"] = padc(w_up[:, :F]).astype(BF16)
    W["wg"] = padc(w_up[:, F:]).astype(BF16)
    W["cw"] = jnp.pad(padc(P["ffn_conv_w"][l]), ((0, 5), (0, 0)))
    W["cb"] = padc(P["ffn_conv_b"][l][None])
    W["wd"] = jnp.pad(P["ffn_w_down"][l], ((0, Fp - F), (0, 0))).astype(BF16)
    W["F"], W["Fp"] = F, Fp
    return W


def _rw_to_mine(a):
    pad = jnp.zeros(a.shape[:-1] + (64,), a.dtype)
    return jnp.concatenate([a[..., :3072], a[..., 3264:3520], a[..., 3072:3264], pad], axis=-1)


def _rw_from_mine(a):
    return jnp.concatenate([a[..., :3072], a[..., 3328:3520], a[..., 3072:3328]], axis=-1)


def _state_to_pairs(s):
    S, H = s.shape[:2]
    sp = s.reshape(S, H // 2, 2, HEAD_DIM, HEAD_DIM)
    z = jnp.zeros_like(sp[:, :, 0])
    top = jnp.concatenate([sp[:, :, 0], z], axis=-1)
    bot = jnp.concatenate([z, sp[:, :, 1]], axis=-1)
    bd = jnp.concatenate([top, bot], axis=-2)
    return bd.transpose(0, 2, 1, 3).reshape(S, LANES, H // 2 * LANES)


def _state_from_pairs(bd, H):
    S = bd.shape[0]
    b4 = bd.reshape(S, LANES, H // 2, LANES).transpose(0, 2, 1, 3)
    return jnp.stack([b4[:, :, :HEAD_DIM, :HEAD_DIM], b4[:, :, HEAD_DIM:, HEAD_DIM:]], axis=2).reshape(
        S, H, HEAD_DIM, HEAD_DIM)


def _layer(x, W, tb, *, prompt, n_seq, seq_rows, n_new, past, dec):
    blk_a = rms_matmul(x, W["norm_mix"], W["wA"])
    blk_b = rms_matmul(x, W["norm_mix"], W["wB"])
    blk_c = rms_matmul(x, W["norm_mix"], W["wC"])
    blk_d = rms_matmul(x, W["norm_mix"], W["wD"])
    nq, cmp_new, slc_new, win_new = nsa_prep(blk_b, W["gains"])
    M = x.shape[0]
    if prompt:
        o_a = sb_prompt(blk_a, n_seq, seq_rows, 16)
        fs = compress(cmp_new.reshape(M // LANES, 1, LANES, 512), 0, W["w1c"])
        kd, vd = cmp_finish(fs.reshape(n_seq, seq_rows // CMP_STRIDE, 1024), W["posb"], W["w2dup"], W["b2"],
                            W["gain1"])
        tq = tb["tq"]
        o_cmp, sel = cmp_attn(nq, kd, vd, tb["cmp_p"], tb["ov_p"], S=n_seq, tq=tq, n_q=seq_rows // tq,
                              n_sel=tb["n_sel_p"], pos0=0, transposed=True)
        o_slc = flash_prompt(nq, slc_new, tb["tiles"], sel, B=n_seq, T=seq_rows, mode="slc")
        o_win = flash_prompt(nq, win_new, tb["tiles"], None, B=n_seq, T=seq_rows, mode="win")
        head = jnp.zeros((n_seq, SUBLANES, RW_COLS), F32)
        rw = rwkv_prep(blk_c, head, W["mu"], W["vec"], W["w2p"], W["a2p"], W["g2"], seq_rows=seq_rows,
                       whole=False)
        C = RWKV_CHUNK
        s0 = jnp.zeros((n_seq, LANES, 1024), F32)
    else:
        lyr = dec["layer"]
        pt = dec["page_table"]
        o_a = sb_decode(blk_a, dec["cache_sb"], pt, lyr, n_new)
        fs_phys = compress(dec["cache_cmp"], lyr, W["w1c"], transposed=True)
        fs = gather_page_rows(fs_phys, pt)
        kd, vd = cmp_finish(fs, W["posb"], W["w2dup"], W["b2"], W["gain1"])
        o_cmp, sel = cmp_attn(nq, kd, vd, tb["cmp_d"], tb["ov_d"], S=n_seq, tq=ROW_GROUP, n_q=1,
                              n_sel=tb["n_sel_d"], pos0=past - (ROW_GROUP - n_new), transposed=False)
        o_slc = slc_decode(nq, slc_new, sel, tb["slc_d"], dec["cache_slc"], pt, lyr, n_new)
        o_win = win_decode(nq, win_new, tb["win_d"], dec["cache_win"], lyr, n_new)
        shift = _rw_to_mine(dec["shift"][:, lyr])
        head = jnp.zeros((n_seq, ROW_GROUP, RW_COLS), F32).at[:, ROW_GROUP - n_new - 1].set(shift)
        rw = rwkv_prep(blk_c, head.reshape(M, RW_COLS), W["mu"], W["vec"], W["w2p"], W["a2p"], W["g2"],
                       seq_rows=ROW_GROUP, whole=True, n_new=n_new)
        C = ROW_GROUP
        s0 = _state_to_pairs(dec["rwkv"][:, lyr].astype(F32))
    o_b = nsa_combine(o_cmp, o_slc, o_win, blk_c, gate_block=27, lane0=64)
    r, lw, k2, v, kk, b, bonus, g = rw
    r2, y2, mt, gt = rwkv_par(r, lw, k2, v, kk, b, C)
    o_c, s_fin = rwkv_seq(r2, y2, mt, gt, s0, bonus, g, W["ln"], n_seq=n_seq)
    mixed = merge_mix(o_a, o_b, o_c, W["wb"], blk_d)
    x1 = matmul_res(x, mixed, W["wout"])
    F, Fp = W["F"], W["Fp"]
    if prompt:
        x2, tail = conv_ffn(x1, W["norm_ffn"], W["wu"], W["wg"], W["cw"], W["cb"], W["wd"],
                            jnp.zeros((n_seq, SUBLANES, Fp), F32), seq_rows=seq_rows, whole_gate=False)
        conv = tail[:, SUBLANES - 2:, :F]
        last = blk_c.reshape(n_seq, seq_rows, RW_COLS)[:, -1]
    else:
        inj = jnp.zeros((n_seq, ROW_GROUP, Fp), F32).at[:, ROW_GROUP - n_new - 2:ROW_GROUP - n_new, :F].set(
            dec["conv"][:, lyr])
        real = (jnp.arange(M) % ROW_GROUP >= ROW_GROUP - n_new)[:, None]
        x2, gate = conv_ffn(jnp.where(real, x1, 0.0), W["norm_ffn"], W["wu"], W["wg"], W["cw"], W["cb"], W["wd"],
                            inj.reshape(M, Fp), seq_rows=ROW_GROUP, whole_gate=True)
        x2 = jnp.where(real, x2, 0.0)
        conv = gate.reshape(n_seq, ROW_GROUP, Fp)[:, ROW_GROUP - 2:, :F]
        last = blk_c.reshape(n_seq, ROW_GROUP, RW_COLS)[:, -1]
    new = {"sb": blk_a[:, 1024:], "cmp": cmp_new, "slc": slc_new, "win": win_new,
           "rwkv": _state_from_pairs(s_fin, 16), "shift": _rw_from_mine(last), "conv": conv}
    return x2, new


def kernel(x_prompt, x_sample, cache_sb_kv, cache_cmp_kv, cache_slc_kv, cache_win_kv, state_rwkv, state_rwkv_shift, state_conv, page_table, rel_bias, norm_mix_g, norm_ffn_g, w_in, nsa_qk_gain, cmp_pe, cmp_w1, cmp_b1, cmp_w2, cmp_b2, rwkv_mu, rwkv_w0, rwkv_w2, rwkv_a0, rwkv_a2, rwkv_g2, rwkv_kk, rwkv_ka, rwkv_rk, rwkv_lnx_w, rwkv_lnx_b, w_branch, w_out, ffn_w_up, ffn_conv_w, ffn_conv_b, ffn_w_down):
    P = dict(norm_mix_g=norm_mix_g, norm_ffn_g=norm_ffn_g, w_in=w_in, nsa_qk_gain=nsa_qk_gain, cmp_pe=cmp_pe,
             cmp_w1=cmp_w1, cmp_b1=cmp_b1, cmp_w2=cmp_w2, cmp_b2=cmp_b2, rwkv_mu=rwkv_mu, rwkv_w0=rwkv_w0,
             rwkv_w2=rwkv_w2, rwkv_a0=rwkv_a0, rwkv_a2=rwkv_a2, rwkv_g2=rwkv_g2, rwkv_kk=rwkv_kk,
             rwkv_ka=rwkv_ka, rwkv_rk=rwkv_rk, rwkv_lnx_w=rwkv_lnx_w, rwkv_lnx_b=rwkv_lnx_b,
             w_branch=w_branch, w_out=w_out, ffn_w_up=ffn_w_up, ffn_conv_w=ffn_conv_w,
             ffn_conv_b=ffn_conv_b, ffn_w_down=ffn_w_down)
    B, T, D = x_prompt.shape
    S, n_new, _ = x_sample.shape
    depth = w_in.shape[0]
    n_phys, _, page = cache_sb_kv.shape[:3]
    n_pages = page_table.shape[1]
    past = n_pages * page
    n_heads = rel_bias.shape[1]
    assert page == LANES and n_new + 2 <= ROW_GROUP and T % 128 == 0 and past % LANES == 0
    assert cache_win_kv.shape[2] == WINDOW <= past
    tb = _make_tables(rel_bias, T, past, n_new, n_heads)
    dec = {"page_table": page_table,
           "cache_sb": jnp.transpose(cache_sb_kv, (0, 1, 3, 4, 5, 2)).reshape(n_phys, depth, -1, page),
           "cache_cmp": jnp.transpose(cache_cmp_kv, (0, 1, 3, 4, 5, 2)).reshape(n_phys, depth, -1, page),
           "cache_slc": jnp.transpose(cache_slc_kv, (0, 1, 3, 4, 5, 2)).reshape(n_phys, depth, -1, page),
           "cache_win": cache_win_kv.reshape(S, depth, WINDOW, -1),
           "rwkv": state_rwkv, "shift": state_rwkv_shift, "conv": state_conv}
    xp = x_prompt.reshape(B * T, D)
    xs = jnp.zeros((S, ROW_GROUP, D), F32).at[:, ROW_GROUP - n_new:].set(x_sample).reshape(S * ROW_GROUP, D)
    new_p, new_s = [], []
    for l in range(depth):
        W = _prep_layer(l, P)
        xp, st = _layer(xp, W, tb, prompt=True, n_seq=B, seq_rows=T, n_new=0, past=0, dec=None)
        new_p.append(st)
        dec["layer"] = l
        xs, st = _layer(xs, W, tb, prompt=False, n_seq=S, seq_rows=ROW_GROUP, n_new=n_new, past=past, dec=dec)
        new_s.append(st)

    n_win = min(WINDOW, T)

    def stack_p(name, shape):
        return jnp.stack([st[name].reshape((B, -1) + shape) for st in new_p], axis=1)

    def rows_s(a):
        return a.reshape(S, ROW_GROUP, -1)[:, ROW_GROUP - n_new:]

    def stack_s(name, shape):
        return jnp.stack([rows_s(st[name]).reshape((S, n_new) + shape) for st in new_s], axis=1)

    kvh = cache_cmp_kv.shape[4]
    sbh = cache_sb_kv.shape[4]
    p_win = jnp.stack([st["win"].reshape(B, T, 2, kvh, HEAD_DIM)[:, T - n_win:] for st in new_p], axis=1)
    s_win = jnp.stack([jnp.concatenate([cache_win_kv[:, l], rows_s(st["win"]).reshape(S, n_new, 2, kvh, HEAD_DIM)],
                                       axis=1)[:, n_new:] for l, st in enumerate(new_s)], axis=1)
    outs = (xp.reshape(B, T, D), rows_s(xs),
            stack_p("sb", (2, sbh, HEAD_DIM)), stack_p("cmp", (2, kvh, HEAD_DIM)),
            stack_p("slc", (2, kvh, HEAD_DIM)), p_win,
            jnp.stack([st["rwkv"] for st in new_p], axis=1), jnp.stack([st["shift"] for st in new_p], axis=1),
            jnp.stack([st["conv"] for st in new_p], axis=1),
            stack_s("sb", (2, sbh, HEAD_DIM)), stack_s("cmp", (2, kvh, HEAD_DIM)),
            stack_s("slc", (2, kvh, HEAD_DIM)), s_win,
            jnp.stack([st["rwkv"] for st in new_s], axis=1), jnp.stack([st["shift"] for st in new_s], axis=1),
            jnp.stack([st["conv"] for st in new_s], axis=1))
    return outs
```

```python
import functools
import math

import numpy as np
import jax
import jax.numpy as jnp
from jax import lax
from jax.experimental import pallas as pl
from jax.experimental.pallas import tpu as pltpu

F32 = jnp.float32
BF16 = jnp.bfloat16

HEAD_DIM = 64
LANES = 128
SUBLANES = 8
ROW_GROUP = 8
VMEM_LIMIT = 56 * 2 ** 20

RMS_EPS = 1e-6
GN_EPS = 64e-5
CMP_STRIDE = 16
CMP_BLOCK = 32
SEL_BLOCK = 64
N_SEL = 16
WINDOW = 512
N_BUCKETS = 32
MAX_DISTANCE = 1024
NEG = -1e30
M_FLOOR = -1e20
RWKV_CHUNK = 64

NN = (((1,), (0,)), ((), ()))
NT = (((1,), (1,)), ((), ()))
TN = (((0,), (0,)), ((), ()))


def _dot(a, b, dims=NN):
    return lax.dot_general(a, b, dims, preferred_element_type=F32)


def _split(x):
    hi = x.astype(BF16)
    lo = (x - hi.astype(F32)).astype(BF16)
    return hi, lo


def _dot_hl(a, b_exact, dims=NN):
    hi, lo = _split(a)
    return _dot(hi, b_exact, dims) + _dot(lo, b_exact, dims)


def _dot_hl_rhs(a_exact, b):
    hi, lo = _split(b)
    return _dot(a_exact, hi) + _dot(a_exact, lo)


def _dot3(a, b, dims=NN):
    ah, al = _split(a)
    bh, bl = _split(b)
    return _dot(ah, bh, dims) + (_dot(ah, bl, dims) + _dot(al, bh, dims))


def _iota(shape, dim):
    return lax.broadcasted_iota(jnp.int32, shape, dim)


def _pick(n, cands):
    for c in cands:
        if n % c == 0:
            return c
    raise ValueError(f"no tile for {n} in {cands}")


def _cparams(*sem):
    return pltpu.CompilerParams(dimension_semantics=sem, vmem_limit_bytes=VMEM_LIMIT)


def _seg_mat(n, seg, scale):
    r = lax.shift_right_logical(_iota((n, n), 0), int(math.log2(seg)))
    c = lax.shift_right_logical(_iota((n, n), 1), int(math.log2(seg)))
    return jnp.where(r == c, scale, 0.0).astype(BF16)


def _lo_mask():
    return _iota((1, LANES), 1) < HEAD_DIM


def _rms_rows(x, g):
    ms = jnp.mean(x * x, axis=-1, keepdims=True)
    return x * lax.rsqrt(ms + RMS_EPS) * g


def _rms_matmul_kernel(x_ref, g_ref, w_ref, o_ref, h_ref):
    @pl.when(pl.program_id(1) == 0)
    def _():
        h_ref[...] = _rms_rows(x_ref[...], g_ref[...]).astype(BF16)

    o_ref[...] = _dot(h_ref[...], w_ref[...])


def rms_matmul(x, g, w, tn=512):
    M, K = x.shape
    N = w.shape[1]
    tm = _pick(M, (1024, 512, 256, 128, 64, 32, 16, 8))
    tn = _pick(N, (tn, 256, 128))
    return pl.pallas_call(
        _rms_matmul_kernel,
        grid=(M // tm, N // tn),
        in_specs=[pl.BlockSpec((tm, K), lambda i, j: (i, 0)),
                  pl.BlockSpec((1, K), lambda i, j: (0, 0)),
                  pl.BlockSpec((K, tn), lambda i, j: (0, j))],
        out_specs=pl.BlockSpec((tm, tn), lambda i, j: (i, j)),
        out_shape=jax.ShapeDtypeStruct((M, N), F32),
        scratch_shapes=[pltpu.VMEM((tm, K), BF16)],
        compiler_params=_cparams("parallel", "arbitrary"),
        name="rms_matmul",
    )(x, g.reshape(1, K), w)


def _matmul_res_kernel(x_ref, a_ref, w_ref, o_ref):
    o_ref[...] = x_ref[...] + _dot(a_ref[...], w_ref[...])


def matmul_res(x, a, w, tn=512):
    M, N = x.shape
    K = a.shape[1]
    tm = _pick(M, (1024, 512, 256, 128, 64, 32, 16, 8))
    tn = _pick(N, (tn, 256, 128))
    return pl.pallas_call(
        _matmul_res_kernel,
        grid=(M // tm, N // tn),
        in_specs=[pl.BlockSpec((tm, tn), lambda i, j: (i, j)),
                  pl.BlockSpec((tm, K), lambda i, j: (i, 0)),
                  pl.BlockSpec((K, tn), lambda i, j: (0, j))],
        out_specs=pl.BlockSpec((tm, tn), lambda i, j: (i, j)),
        out_shape=jax.ShapeDtypeStruct((M, N), F32),
        compiler_params=_cparams("parallel", "arbitrary"),
        name="matmul_res",
    )(x, a, w)


def _merge_mix_kernel(oa_ref, ob_ref, oc_ref, wb_ref, m0_ref, m1_ref, m2_ref, o_ref):
    acc = jax.nn.sigmoid(m0_ref[...]) * _dot(oa_ref[...], wb_ref[0])
    acc = acc + jax.nn.sigmoid(m1_ref[...]) * _dot(ob_ref[...], wb_ref[1])
    acc = acc + jax.nn.sigmoid(m2_ref[...]) * _dot(oc_ref[...], wb_ref[2])
    o_ref[...] = acc.astype(o_ref.dtype)


def merge_mix(oa, ob, oc, wb, merge, tn=512):
    M, W = oa.shape
    D = wb.shape[2]
    tm = _pick(M, (512, 256, 128, 64, 32, 16, 8))
    nb = D // tn
    ospec = pl.BlockSpec((tm, W), lambda i, j: (i, 0))
    return pl.pallas_call(
        _merge_mix_kernel,
        grid=(M // tm, nb),
        in_specs=[ospec, ospec, ospec,
                  pl.BlockSpec((3, W, tn), lambda i, j: (0, 0, j)),
                  pl.BlockSpec((tm, tn), lambda i, j: (i, j)),
                  pl.BlockSpec((tm, tn), lambda i, j: (i, j + nb)),
                  pl.BlockSpec((tm, tn), lambda i, j: (i, j + 2 * nb))],
        out_specs=pl.BlockSpec((tm, tn), lambda i, j: (i, j)),
        out_shape=jax.ShapeDtypeStruct((M, D), BF16),
        compiler_params=_cparams("parallel", "arbitrary"),
        name="merge_mix",
    )(oa, ob, oc, wb, merge, merge, merge)


def _ffn_kernel(x_ref, g_ref, wu_ref, wg_ref, cw_ref, cb_ref, wd_ref, prev_ref,
                o_ref, st_ref, h_ref, acc_ref, gs_ref, carry_ref, *, tm, tps, whole_gate):
    i = pl.program_id(0)
    n = pl.program_id(1)

    @pl.when(n == 0)
    def _():
        h_ref[...] = _rms_rows(x_ref[...], g_ref[...]).astype(BF16)
        acc_ref[...] = jnp.zeros_like(acc_ref)

    h = h_ref[...]
    u = _dot(h, wu_ref[...])
    g = _dot(h, wg_ref[...])
    if whole_gate:
        g = g + prev_ref[...]
        head = jnp.zeros((SUBLANES, g.shape[1]), F32)
    else:
        head = jnp.where(i % tps == 0, prev_ref[0], carry_ref[n])
    gs_ref[pl.ds(0, SUBLANES), :] = head
    gs_ref[pl.ds(SUBLANES, tm), :] = g
    gm1 = gs_ref[pl.ds(SUBLANES - 1, tm), :]
    gm2 = gs_ref[pl.ds(SUBLANES - 2, tm), :]
    gc = cb_ref[...] + gm2 * cw_ref[0:1, :] + gm1 * cw_ref[1:2, :] + g * cw_ref[2:3, :]
    act = (gc * jax.nn.sigmoid(gc) * u).astype(BF16)
    acc_ref[...] += _dot(act, wd_ref[...])
    if whole_gate:
        st_ref[...] = g
    else:
        tail = gs_ref[pl.ds(tm, SUBLANES), :]
        carry_ref[n] = tail
        st_ref[0] = tail

    @pl.when(n == pl.num_programs(1) - 1)
    def _():
        o_ref[...] = x_ref[...] + acc_ref[...]


def conv_ffn(x, g, wu, wg, cw, cb, wd, prev, *, seq_rows, whole_gate, tn=512):
    M, D = x.shape
    Fp = wu.shape[1]
    nb = Fp // tn
    if whole_gate:
        tm, tps = M, 1
        prev_spec = pl.BlockSpec((tm, tn), lambda i, n: (0, n))
        st_spec = pl.BlockSpec((tm, tn), lambda i, n: (0, n))
        st_shape = jax.ShapeDtypeStruct((M, Fp), F32)
    else:
        tm = _pick(seq_rows, (512, 256, 128, 64))
        tps = seq_rows // tm
        prev_spec = pl.BlockSpec((1, SUBLANES, tn), lambda i, n: (i // tps, 0, n))
        st_spec = pl.BlockSpec((1, SUBLANES, tn), lambda i, n: (i, 0, n))
        st_shape = jax.ShapeDtypeStruct((M // tm, SUBLANES, Fp), F32)
    kern = functools.partial(_ffn_kernel, tm=tm, tps=tps, whole_gate=whole_gate)
    y, st = pl.pallas_call(
        kern,
        grid=(M // tm, nb),
        in_specs=[pl.BlockSpec((tm, D), lambda i, n: (i, 0)),
                  pl.BlockSpec((1, D), lambda i, n: (0, 0)),
                  pl.BlockSpec((D, tn), lambda i, n: (0, n)),
                  pl.BlockSpec((D, tn), lambda i, n: (0, n)),
                  pl.BlockSpec((SUBLANES, tn), lambda i, n: (0, n)),
                  pl.BlockSpec((1, tn), lambda i, n: (0, n)),
                  pl.BlockSpec((tn, D), lambda i, n: (n, 0)),
                  prev_spec],
        out_specs=[pl.BlockSpec((tm, D), lambda i, n: (i, 0)), st_spec],
        out_shape=[jax.ShapeDtypeStruct((M, D), F32), st_shape],
        scratch_shapes=[pltpu.VMEM((tm, D), BF16), pltpu.VMEM((tm, D), F32),
                        pltpu.VMEM((tm + SUBLANES, tn), F32), pltpu.VMEM((nb, SUBLANES, tn), F32)],
        compiler_params=_cparams("arbitrary", "arbitrary"),
        name="conv_ffn",
    )(x, g.reshape(1, D), wu, wg, cw, cb, wd, prev)
    return (y, st) if whole_gate else (y, st[tps - 1::tps])


def _sb_rhs():
    j = jnp.bitwise_and(_iota((2 * LANES, 2 * LANES), 0), LANES - 1)
    s = _iota((2 * LANES, 2 * LANES), 1)
    return jnp.where((j > s) | (s >= LANES), -1.0, 0.0).astype(BF16)


def _sb_sums(sp, nrhs):
    hi, lw = _split(sp)
    return _dot(jnp.concatenate([hi, lw], axis=1), nrhs)


LOG2E = math.log2(math.e)


def _softplus2(z):
    return jnp.maximum(z, 0.0) + jnp.log2(1.0 + jnp.exp2(-jnp.abs(z)))


def _sb_block(qm, kb, vb, before, c, nrhs):
    z = _dot(qm, kb, NT)
    sp = _softplus2(z)
    if before is not None:
        sp = jnp.where(before, sp, 0.0)
    cs2 = _sb_sums(sp, nrhs)
    w = jnp.exp2((z - sp) + (cs2[:, :LANES] + c))
    if before is not None:
        w = jnp.where(before, w, 0.0)
    return _dot(w.astype(BF16), vb), c + cs2[:, LANES:]


def _sb_pair_blocks(q, kbs, vbs, befores, c, nrhs, lo, row0s=None):
    def per_head(x):
        zero = jnp.zeros_like(x)
        return jnp.concatenate([jnp.where(lo, x, zero), jnp.where(lo, zero, x)], axis=0)

    def add_from(full, r0, part):
        return full + part if r0 == 0 else jnp.concatenate([full[:r0], full[r0:] + part], axis=0)

    row0s = row0s or [0] * len(kbs)
    kbd = [per_head(x) for x in kbs]
    vbd = [per_head(x) for x in vbs]
    z = [_dot(q[r0:], x, NT) for x, r0 in zip(kbd, row0s)]
    sp = [_softplus2(x) for x in z]
    sp = [x if m is None else jnp.where(m, x, 0.0) for x, m in zip(sp, befores)]
    cs = [[_sb_sums(x[:, h * LANES:(h + 1) * LANES], nrhs) for h in range(2)] for x in sp]
    suffix = [jnp.concatenate([y[0][:, :LANES], y[1][:, :LANES]], axis=1) for y in cs]
    total = [jnp.concatenate([y[0][:, LANES:], y[1][:, LANES:]], axis=1) for y in cs]
    carry = [c]
    for t, r0 in zip(total, row0s):
        carry.append(add_from(carry[-1], r0, t))
    w = [jnp.exp2((a - b) + (s + cc[r0:])) for a, b, s, cc, r0 in zip(z, sp, suffix, carry, row0s)]
    w = [x if m is None else jnp.where(m, x, 0.0) for x, m in zip(w, befores)]
    pv = [_dot(x.astype(BF16), y) for x, y in zip(w, vbd)]
    acc = jnp.zeros((q.shape[0], LANES), F32)
    for x, r0 in zip(pv, row0s):
        acc = add_from(acc, r0, x)
    return acc, carry[-1]


def _sb_prompt_kernel(qi_ref, kj_ref, q_ref, k_ref, v_ref, o_ref, acc_ref, c_ref, *, tq, tk):
    s = pl.program_id(2)
    qi = qi_ref[s]
    kj = kj_ref[s]
    lo = _lo_mask()
    first = (kj + 1) * tk == (qi + 1) * tq
    overlap = (kj + 1) * tk > qi * tq

    def tile(masked):
        q = (q_ref[...] * LOG2E).astype(BF16)
        nrhs = _sb_rhs()
        c = jnp.where(first, 0.0, c_ref[...])
        subs = list(reversed(range(tk // LANES)))
        kbs = [k_ref[pl.ds(sub * LANES, LANES), :].astype(BF16) for sub in subs]
        vbs = [v_ref[pl.ds(sub * LANES, LANES), :].astype(BF16) for sub in subs]
        befores = [None] * len(subs)
        row0s = None
        if masked:
            row0s = [sub * LANES if tq == tk else 0 for sub in subs]
            befores = []
            for sub, r0 in zip(subs, row0s):
                lane_key = jnp.bitwise_and(_iota((tq - r0, 2 * LANES), 1), LANES - 1)
                row = qi * tq + r0 + _iota((tq - r0, 2 * LANES), 0)
                befores.append(kj * tk + sub * LANES + lane_key < row)
        acc, c = _sb_pair_blocks(q, kbs, vbs, befores, c, nrhs, lo, row0s)
        c_ref[...] = c
        acc_ref[...] = jnp.where(first, 0.0, acc_ref[...]) + acc

    @pl.when(overlap)
    def _():
        tile(True)

    @pl.when(jnp.logical_not(overlap))
    def _():
        tile(False)

    @pl.when(kj == 0)
    def _():
        o_ref[...] = acc_ref[...].astype(o_ref.dtype)


def _tri_steps(n, lookback=None):
    qs, ks, first, last = [], [], [], []
    for q in range(n):
        k_lo = 0 if lookback is None else max(0, q - lookback)
        for k in range(q, k_lo - 1, -1):
            qs.append(q)
            ks.append(k)
            first.append(int(k == q))
            last.append(int(k == k_lo))
    return tuple(jnp.asarray(np.array(a, np.int32)) for a in (qs, ks, first, last))


def _sb_steps(nq, ratio):
    qs, ks = [], []
    for q in range(nq):
        for k in range((q + 1) * ratio - 1, -1, -1):
            qs.append(q)
            ks.append(k)
    return jnp.asarray(np.array(qs, np.int32)), jnp.asarray(np.array(ks, np.int32))


def sb_prompt(qkv, B, T, n_heads, tq=None, tk=None):
    M = B * T
    tk = tk or _pick(T, (512, 256, 128))
    tq = tq or _pick(T, (512, 256, 128))
    assert tq % tk == 0
    nq, nk = T // tq, T // tk
    npair = n_heads // 2
    qs, ks = _sb_steps(nq, tq // tk)
    kern = functools.partial(_sb_prompt_kernel, tq=tq, tk=tk)
    return pl.pallas_call(
        kern,
        grid_spec=pltpu.PrefetchScalarGridSpec(
            num_scalar_prefetch=2,
            grid=(B, npair, int(qs.shape[0])),
            in_specs=[pl.BlockSpec((tq, LANES), lambda b, p, s, qi, kj: (b * nq + qi[s], p)),
                      pl.BlockSpec((tk, LANES), lambda b, p, s, qi, kj: (b * nk + kj[s], npair + p)),
                      pl.BlockSpec((tk, LANES), lambda b, p, s, qi, kj: (b * nk + kj[s], 2 * npair + p))],
            out_specs=pl.BlockSpec((tq, LANES), lambda b, p, s, qi, kj: (b * nq + qi[s], p)),
            scratch_shapes=[pltpu.VMEM((tq, LANES), F32), pltpu.VMEM((tq, 2 * LANES), F32)]),
        out_shape=jax.ShapeDtypeStruct((M, n_heads * HEAD_DIM), BF16),
        compiler_params=_cparams("parallel", "parallel", "arbitrary"),
        name="sb_prompt",
    )(qs, ks, qkv, qkv, qkv)


def _block_diag_rows(q8, n_heads):
    rows = n_heads * ROW_GROUP
    width = q8.shape[1]
    tiled = jnp.concatenate([q8] * n_heads, axis=0)
    rh = lax.shift_right_logical(_iota((rows, width), 0), 3)
    ch = lax.shift_right_logical(_iota((rows, width), 1), 6)
    return jnp.where(rh == ch, tiled, 0.0), rh == ch


def _sb_blocks(qm, kts, vts, c, nrhs):
    z = [_dot(qm, x) for x in kts]
    sp = [_softplus2(x) for x in z]
    cs = [_sb_sums(x, nrhs) for x in sp]
    carry = [c]
    for x in cs:
        carry.append(carry[-1] + x[:, LANES:])
    w = [jnp.exp2((a - b) + (x[:, :LANES] + cc)).astype(BF16) for a, b, x, cc in zip(z, sp, cs, carry)]
    pv = [_dot(x, y, NT) for x, y in zip(w, vts)]
    acc = pv[0]
    for x in pv[1:]:
        acc = acc + x
    return acc, carry[-1]


def _sb_decode_kernel(pt_ref, q_ref, kn_ref, vn_ref, *rest, n_heads, n_steps, n_new, ppb):
    page_refs = rest[:ppb]
    o_ref, acc_ref, c_ref, qbd_ref = rest[ppb:]
    p = pl.program_id(1)
    rows = n_heads * ROW_GROUP
    width = n_heads * HEAD_DIM
    rhs = _sb_rhs()

    @pl.when(p == 0)
    def _():
        qbd, _ = _block_diag_rows(q_ref[...] * LOG2E, n_heads)
        qbd = qbd.astype(BF16)
        qbd_ref[...] = qbd
        pad = jnp.zeros((LANES - ROW_GROUP, width), F32)
        kb = jnp.concatenate([kn_ref[...], pad], axis=0).astype(BF16)
        vb = jnp.concatenate([vn_ref[...], pad], axis=0).astype(BF16)
        r_in = jnp.bitwise_and(_iota((rows, LANES), 0), ROW_GROUP - 1)
        col = _iota((rows, LANES), 1)
        before = (col >= ROW_GROUP - n_new) & (col < ROW_GROUP) & (col < r_in)
        pv, c_new = _sb_block(qbd, kb, vb, before, jnp.zeros((rows, LANES), F32), rhs)
        acc_ref[...] = pv
        c_ref[...] = c_new

    @pl.when(p > 0)
    def _():
        kts = [ref[pl.ds(0, width), :].astype(BF16) for ref in page_refs]
        vts = [ref[pl.ds(width, width), :].astype(BF16) for ref in page_refs]
        pv, c_new = _sb_blocks(qbd_ref[...], kts, vts, c_ref[...], rhs)
        acc_ref[...] += pv
        c_ref[...] = c_new

    @pl.when(p == n_steps - 1)
    def _():
        own = (lax.shift_right_logical(_iota((rows, width), 0), 3)
               == lax.shift_right_logical(_iota((rows, width), 1), 6))
        m = jnp.where(own, acc_ref[...], 0.0).reshape(n_heads, ROW_GROUP, width)
        o_ref[...] = jnp.sum(m, axis=0).astype(o_ref.dtype)


def sb_decode(qkv, cache, page_table, layer, n_new):
    S, n_pages = page_table.shape
    page = cache.shape[3]
    W = cache.shape[2] // 2
    n_heads = W // HEAD_DIM
    assert page == LANES
    rows = n_heads * ROW_GROUP
    pt = page_table.reshape(-1).astype(jnp.int32)
    ppb = _pick(n_pages, (4, 2, 1))
    n_steps = 1 + n_pages // ppb

    def page_map(i):
        def index(s, p, pt_ref):
            logical = n_pages - 1 - (jnp.maximum(p, 1) - 1) * ppb - i
            return (pt_ref[s * n_pages + logical], layer, 0, 0)
        return index

    kern = functools.partial(_sb_decode_kernel, n_heads=n_heads, n_steps=n_steps, n_new=n_new, ppb=ppb)
    return pl.pallas_call(
        kern,
        grid_spec=pltpu.PrefetchScalarGridSpec(
            num_scalar_prefetch=1,
            grid=(S, n_steps),
            in_specs=[pl.BlockSpec((ROW_GROUP, W), lambda s, p, pt_ref: (s, 0)),
                      pl.BlockSpec((ROW_GROUP, W), lambda s, p, pt_ref: (s, 1)),
                      pl.BlockSpec((ROW_GROUP, W), lambda s, p, pt_ref: (s, 2))]
            + [pl.BlockSpec((None, None, 2 * W, page), page_map(i)) for i in range(ppb)],
            out_specs=pl.BlockSpec((ROW_GROUP, W), lambda s, p, pt_ref: (s, 0)),
            scratch_shapes=[pltpu.VMEM((rows, W), F32), pltpu.VMEM((rows, LANES), F32),
                            pltpu.VMEM((rows, W), BF16)]),
        out_shape=jax.ShapeDtypeStruct((S * ROW_GROUP, W), BF16),
        compiler_params=_cparams("parallel", "arbitrary"),
        name="sb_decode",
    )(pt, qkv, qkv, qkv, *([cache] * ppb))


def _nsa_prep_kernel(x_ref, gain_ref, nq_ref, cmp_ref, slc_ref, win_ref):
    seg = _seg_mat(LANES, HEAD_DIM, 1.0 / HEAD_DIM)

    def norm(col, gain_row):
        blk = x_ref[:, pl.ds(col, LANES)]
        ms = _dot_hl(blk * blk, seg)
        return blk * lax.rsqrt(ms + RMS_EPS) * gain_ref[gain_row:gain_row + 1, :]

    for j in range(8):
        nq_ref[:, pl.ds(j * LANES, LANES)] = (norm(j * LANES, 0) * HEAD_DIM ** -0.5).astype(BF16)
    cmp_ref[...] = x_ref[:, pl.ds(1024, 512)]
    for j in range(2):
        slc_ref[:, pl.ds(j * LANES, LANES)] = norm(1536 + j * LANES, 2)
        win_ref[:, pl.ds(j * LANES, LANES)] = norm(2048 + j * LANES, 3)
    slc_ref[:, pl.ds(256, 256)] = x_ref[:, pl.ds(1792, 256)]
    win_ref[:, pl.ds(256, 256)] = x_ref[:, pl.ds(2304, 256)]


def nsa_prep(blk_b, gains):
    M = blk_b.shape[0]
    tm = _pick(M, (512, 256, 128, 64, 32, 16, 8))
    return pl.pallas_call(
        _nsa_prep_kernel,
        grid=(M // tm,),
        in_specs=[pl.BlockSpec((tm, 2560), lambda i: (i, 0)),
                  pl.BlockSpec((SUBLANES, LANES), lambda i: (0, 0))],
        out_specs=[pl.BlockSpec((tm, 1024), lambda i: (i, 0)),
                   pl.BlockSpec((tm, 512), lambda i: (i, 0)),
                   pl.BlockSpec((tm, 512), lambda i: (i, 0)),
                   pl.BlockSpec((tm, 512), lambda i: (i, 0))],
        out_shape=[jax.ShapeDtypeStruct((M, 1024), BF16)] + [jax.ShapeDtypeStruct((M, 512), F32)] * 3,
        compiler_params=_cparams("parallel"),
        name="nsa_prep",
    )(blk_b, gains)


def _chunk_rows(x_ref, G):
    return jnp.concatenate(
        [x_ref[:, pl.ds(s, SUBLANES, stride=CMP_STRIDE), :].reshape(G * SUBLANES, LANES).astype(BF16)
         for s in range(CMP_STRIDE)], axis=1)


def _compress_kernel(x0_ref, x1_ref, x2_ref, x3_ref, w_ref, o_ref, *, G):
    for p, x_ref in enumerate((x0_ref, x1_ref, x2_ref, x3_ref)):
        o_ref[:, pl.ds(p * 2 * LANES, 2 * LANES)] = _dot(_chunk_rows(x_ref, G), w_ref[p // 2])


def _compress_t_kernel(x0_ref, x1_ref, x2_ref, x3_ref, w_ref, o_ref, rows_ref, *, G):
    for p, x_ref in enumerate((x0_ref, x1_ref, x2_ref, x3_ref)):
        for g in range(G):
            rows_ref[p, g] = x_ref[g].T
    for p in range(4):
        o_ref[:, pl.ds(p * 2 * LANES, 2 * LANES)] = _dot(_chunk_rows(rows_ref.at[p], G), w_ref[p // 2])


def compress(pages, layer, w1c, transposed=False):
    n_pages = pages.shape[0]
    G = _pick(n_pages, (16, 8, 5, 4, 3, 2, 1))
    if transposed:
        kern = functools.partial(_compress_t_kernel, G=G)
        page_spec = lambda p: pl.BlockSpec((G, None, LANES, LANES), lambda i: (i, layer, p, 0))
        scratch = [pltpu.VMEM((4, G, LANES, LANES), F32)]
    else:
        kern = functools.partial(_compress_kernel, G=G)
        page_spec = lambda p: pl.BlockSpec((G, None, LANES, LANES), lambda i: (i, layer, 0, p))
        scratch = []
    w1c = w1c.reshape(2, CMP_STRIDE * LANES, 2 * LANES)
    return pl.pallas_call(
        kern,
        grid=(n_pages // G,),
        in_specs=[page_spec(p) for p in range(4)]
        + [pl.BlockSpec((2, CMP_STRIDE * LANES, 2 * LANES), lambda i: (0, 0, 0))],
        out_specs=pl.BlockSpec((G * SUBLANES, 1024), lambda i: (i, 0)),
        out_shape=jax.ShapeDtypeStruct((n_pages * SUBLANES, 1024), F32),
        scratch_shapes=scratch,
        compiler_params=_cparams("parallel"),
        name="nsa_compress",
    )(pages, pages, pages, pages, w1c)


def _gather_rows_kernel(pt_ref, src_ref, o_ref, sem, *, n_pages):
    s = pl.program_id(0)

    def copy(p):
        return pltpu.make_async_copy(src_ref.at[pl.ds(pt_ref[s * n_pages + p] * SUBLANES, SUBLANES)],
                                     o_ref.at[0, pl.ds(p * SUBLANES, SUBLANES)], sem)

    def start(p, carry):
        copy(p).start()
        return carry

    def wait(p, carry):
        copy(p).wait()
        return carry

    lax.fori_loop(0, n_pages, start, 0)
    lax.fori_loop(0, n_pages, wait, 0)


def gather_page_rows(src, page_table):
    S, n_pages = page_table.shape
    W = src.shape[1]
    kern = functools.partial(_gather_rows_kernel, n_pages=n_pages)
    return pl.pallas_call(
        kern,
        grid_spec=pltpu.PrefetchScalarGridSpec(
            num_scalar_prefetch=1,
            grid=(S,),
            in_specs=[pl.BlockSpec(memory_space=pl.ANY)],
            out_specs=pl.BlockSpec((1, n_pages * SUBLANES, W), lambda s, pt_ref: (s, 0, 0)),
            scratch_shapes=[pltpu.SemaphoreType.DMA(())]),
        out_shape=jax.ShapeDtypeStruct((S, n_pages * SUBLANES, W), F32),
        compiler_params=_cparams("arbitrary"),
        name="gather_page_rows",
    )(page_table.reshape(-1).astype(jnp.int32), src)


def _cmp_finish_kernel(fs_ref, posb_ref, w2_ref, b2_ref, gain_ref, kd_ref, vd_ref, *, nch):
    for p in range(4):
        c, pp = p // 2, p % 2
        first = fs_ref[:, pl.ds(p * 2 * LANES, LANES)]
        second = pltpu.roll(fs_ref[:, pl.ds(p * 2 * LANES + LANES, LANES)], nch - 1, 0)
        hid = jax.nn.gelu(first + second + posb_ref[c:c + 1, :]).astype(BF16)
        for e in range(2):
            out = _dot(hid, w2_ref[c, e]) + b2_ref[c:c + 1, :]
            if c == 0:
                ms = jnp.mean(out * out, axis=-1, keepdims=True)
                kd_ref[pp * 2 + e] = (out * lax.rsqrt(ms + RMS_EPS) * gain_ref[...]).astype(BF16)
            else:
                vd_ref[pp * 2 + e] = out.astype(BF16)


def cmp_finish(fs, posb, w2dup, b2, gain):
    S, nch, _ = fs.shape
    kern = functools.partial(_cmp_finish_kernel, nch=nch)
    small = pl.BlockSpec((SUBLANES, LANES), lambda s: (0, 0))
    return pl.pallas_call(
        kern,
        grid=(S,),
        in_specs=[pl.BlockSpec((None, nch, 1024), lambda s: (s, 0, 0)), small,
                  pl.BlockSpec((2, 2, LANES, LANES), lambda s: (0, 0, 0, 0)), small,
                  pl.BlockSpec((1, LANES), lambda s: (0, 0))],
        out_specs=[pl.BlockSpec((None, 4, nch, LANES), lambda s: (s, 0, 0, 0))] * 2,
        out_shape=[jax.ShapeDtypeStruct((S, 4, nch, LANES), BF16)] * 2,
        compiler_params=_cparams("parallel"),
        name="cmp_finish",
    )(fs, posb, w2dup, b2, gain)


def _head_queries(q, lo):
    out = []
    for g in range(4):
        blk = q[:, (g // 2) * LANES:(g // 2 + 1) * LANES]
        keep = lo if g % 2 == 0 else jnp.logical_not(lo)
        out.append(jnp.where(keep, blk, jnp.zeros_like(blk)))
    return out


def _pair_out(o, lo):
    return jnp.concatenate([jnp.where(lo, o[0], o[1]), jnp.where(lo, o[2], o[3])], axis=1)


def _topk_mask(score, n_sel, k_eff, axis):
    j = _iota(score.shape, axis)
    rank = jnp.zeros(score.shape, jnp.int32)
    for i in range(n_sel):
        si = score[i:i + 1, :] if axis == 0 else score[:, i:i + 1]
        ahead = (si > score) | ((si == score) & (j > i))
        rank = rank + jnp.where(ahead, 1, 0)
    return rank < k_eff


def _cmp_attn_kernel(q_ref, kd_ref, vd_ref, bias_ref, ov_ref, o_ref, sel_ref, *,
                     tq, ncp, n_sel, k_eff, pos0, transposed):
    t0 = pos0 + pl.program_id(2) * tq
    t = t0 + _iota((tq, ncp), 0)
    c = _iota((tq, ncp), 1)
    valid = t - (c * CMP_STRIDE + CMP_BLOCK - 1) >= 0
    lo = _lo_mask()
    kd = kd_ref[...]
    vd = vd_ref[...]
    qms = _head_queries(q_ref[...], lo)
    sc = [jnp.where(valid, _dot(qm, kd, NT) + bias_ref[g], NEG) for g, qm in enumerate(qms)]
    m = [jnp.max(x, axis=1, keepdims=True) for x in sc]
    e = [jnp.where(valid, jnp.exp(x - y), 0.0) for x, y in zip(sc, m)]
    den = [jnp.sum(x, axis=1, keepdims=True) for x in e]
    p = [x / jnp.where(y > 0, y, 1.0) for x, y in zip(e, den)]
    outs = [_dot(x.astype(BF16), vd) for x in p]
    psum = (p[0] + p[1]) + (p[2] + p[3])
    o_ref[...] = _pair_out(outs, lo)
    hi, lw = _split(psum)
    if transposed:
        imp = _dot(ov_ref[...], hi, NT) + _dot(ov_ref[...], lw, NT)
        j = _iota(imp.shape, 0)
        tt = t0 + _iota(imp.shape, 1)
    else:
        imp = _dot(hi, ov_ref[...]) + _dot(lw, ov_ref[...])
        j = _iota(imp.shape, 1)
        tt = t0 + _iota(imp.shape, 0)
    causal = j * SEL_BLOCK <= tt
    cur = lax.shift_right_logical(tt, 6)
    forced = causal & ((j == 0) | (j == cur) | (j == cur - 1))
    score = jnp.where(forced, -NEG, jnp.where(causal, imp, NEG))
    score = jnp.where(j < n_sel, score, 2 * NEG)
    sel = _topk_mask(score, n_sel, k_eff, 0 if transposed else 1)
    sel_ref[...] = jnp.where(sel, 1.0, 0.0)


def cmp_attn(nq, kd, vd, bias, ov, *, S, tq, n_q, n_sel, pos0, transposed):
    ncp = kd.shape[2]
    nsp = ov.shape[0] if transposed else ov.shape[1]
    k_eff = min(N_SEL, n_sel)
    rows = S * n_q * tq
    if transposed:
        sel_spec = pl.BlockSpec((None, None, nsp, tq), lambda b, k, i: (b, k, 0, i))
        sel_shape = jax.ShapeDtypeStruct((S, 4, nsp, n_q * tq), F32)
    else:
        sel_spec = pl.BlockSpec((None, None, tq, nsp), lambda b, k, i: (b, k, i, 0))
        sel_shape = jax.ShapeDtypeStruct((S, 4, n_q * tq, nsp), F32)
    kern = functools.partial(_cmp_attn_kernel, tq=tq, ncp=ncp, n_sel=n_sel, k_eff=k_eff, pos0=pos0,
                             transposed=transposed)
    return pl.pallas_call(
        kern,
        grid=(S, 4, n_q),
        in_specs=[pl.BlockSpec((tq, 2 * LANES), lambda b, k, i: (b * n_q + i, k)),
                  pl.BlockSpec((None, None, ncp, LANES), lambda b, k, i: (b, k, 0, 0)),
                  pl.BlockSpec((None, None, ncp, LANES), lambda b, k, i: (b, k, 0, 0)),
                  pl.BlockSpec((4, tq, ncp), lambda b, k, i: (k, i, 0)),
                  pl.BlockSpec(ov.shape, lambda b, k, i: (0, 0))],
        out_specs=[pl.BlockSpec((tq, 2 * LANES), lambda b, k, i: (b * n_q + i, k)), sel_spec],
        out_shape=[jax.ShapeDtypeStruct((rows, 1024), F32), sel_shape],
        compiler_params=_cparams("parallel", "parallel", "arbitrary"),
        name="cmp_attn",
    )(nq, kd, vd, bias, ov)


def _dup_half(x, odd):
    lo = _lo_mask()
    take_x = jnp.logical_xor(lo, odd)
    return jnp.where(take_x, x, pltpu.roll(x, HEAD_DIM, 1))


def _flash_kernel(qi_ref, kj_ref, dl_ref, first_ref, last_ref, q_ref, k_ref, v_ref, bias_ref, *rest,
                  tq, tk, mode):
    if mode == "slc":
        sel_ref, o_ref, m_ref, l_ref, acc_ref = rest
    else:
        o_ref, m_ref, l_ref, acc_ref = rest
    kvh = pl.program_id(1)
    s = pl.program_id(2)
    qi = qi_ref[s]
    kj = kj_ref[s]

    @pl.when(first_ref[s] == 1)
    def _():
        m_ref[...] = jnp.full_like(m_ref, M_FLOOR)
        l_ref[...] = jnp.zeros_like(l_ref)
        acc_ref[...] = jnp.zeros_like(acc_ref)

    lo = _lo_mask()
    odd = (kvh % 2) == 1
    kd = _dup_half(k_ref[...], odd).astype(BF16)
    vd = _dup_half(v_ref[...], odd).astype(BF16)
    ones = jnp.ones((tk, LANES), BF16)
    if mode == "slc":
        nsp = sel_ref.shape[0]
        blk = kj * (tk // SEL_BLOCK) + lax.shift_right_logical(_iota((nsp, tk), 1), 6)
        expand = jnp.where(_iota((nsp, tk), 0) == blk, 1.0, 0.0).astype(BF16)
    rc = min(tq, LANES)
    units, valids = [], []
    for r0 in range(0, tq, rc):
        rows = pl.ds(r0, rc)
        d = (qi * tq + r0 + _iota((rc, tk), 0)) - (kj * tk + _iota((rc, tk), 1))
        if mode == "slc":
            chosen = _dot(sel_ref[:, rows].astype(BF16), expand, TN)
            valids.append((chosen > 0.5) & (d >= 0))
        else:
            valids.append((d >= 0) & (d < WINDOW))
        for g, qm in enumerate(_head_queries(q_ref[rows, :], lo)):
            units.append((g, rows, qm, len(valids) - 1))
    sc = [_dot(qm, kd, NT) + jnp.where(valids[v], bias_ref[g, rows, :], NEG) for g, rows, qm, v in units]
    m_old = [m_ref[g, rows, :] for g, rows, _, _ in units]
    m_new = [jnp.maximum(mo, jnp.broadcast_to(jnp.max(x, axis=1, keepdims=True), mo.shape))
             for mo, x in zip(m_old, sc)]
    p = [jnp.exp(x - jnp.concatenate([mn] * (tk // LANES), axis=1)).astype(BF16) for x, mn in zip(sc, m_new)]
    alpha = [jnp.exp(mo - mn) for mo, mn in zip(m_old, m_new)]
    rowsum = [_dot(x, ones) for x in p]
    pv = [_dot(x, vd) for x in p]
    for (g, rows, _, _), a, rs, o, mn in zip(units, alpha, rowsum, pv, m_new):
        l_ref[g, rows, :] = a * l_ref[g, rows, :] + rs
        acc_ref[g, rows, :] = a * acc_ref[g, rows, :] + o
        m_ref[g, rows, :] = mn

    @pl.when(last_ref[s] == 1)
    def _():
        outs = []
        for g in range(4):
            den = l_ref[g]
            outs.append(acc_ref[g] / jnp.where(den > 0, den, 1.0))
        o_ref[...] = _pair_out(outs, lo)


def flash_prompt(nq, kv, bias_tiles, sel, *, B, T, mode):
    tq = tk = bias_tiles.shape[2]
    nq_t = T // tq
    nd = bias_tiles.shape[1]
    lookback = None if mode == "slc" else -(-(WINDOW - 1) // tk)
    qs, ks, first, last = _tri_steps(nq_t, lookback)
    dl = jnp.minimum(qs - ks, nd - 1)
    n_steps = int(qs.shape[0])
    in_specs = [pl.BlockSpec((tq, 2 * LANES), lambda b, k, s, qi, kj, dl, f, l: (b * nq_t + qi[s], k)),
                pl.BlockSpec((tk, LANES), lambda b, k, s, qi, kj, dl, f, l: (b * nq_t + kj[s], k // 2)),
                pl.BlockSpec((tk, LANES), lambda b, k, s, qi, kj, dl, f, l: (b * nq_t + kj[s], 2 + k // 2)),
                pl.BlockSpec((4, None, tq, tk), lambda b, k, s, qi, kj, dl, f, l: (k, dl[s], 0, 0))]
    args = [nq, kv, kv, bias_tiles]
    if mode == "slc":
        nsp = sel.shape[2]
        in_specs.append(pl.BlockSpec((None, None, nsp, tq), lambda b, k, s, qi, kj, dl, f, l: (b, k, 0, qi[s])))
        args.append(sel)
    kern = functools.partial(_flash_kernel, tq=tq, tk=tk, mode=mode)
    return pl.pallas_call(
        kern,
        grid_spec=pltpu.PrefetchScalarGridSpec(
            num_scalar_prefetch=5,
            grid=(B, 4, n_steps),
            in_specs=in_specs,
            out_specs=pl.BlockSpec((tq, 2 * LANES), lambda b, k, s, qi, kj, dl, f, l: (b * nq_t + qi[s], k)),
            scratch_shapes=[pltpu.VMEM((4, tq, LANES), F32), pltpu.VMEM((4, tq, LANES), F32),
                            pltpu.VMEM((4, tq, LANES), F32)]),
        out_shape=jax.ShapeDtypeStruct((B * T, 1024), F32),
        compiler_params=_cparams("parallel", "parallel", "arbitrary"),
        name="nsa_" + mode,
    )(qs, ks, dl, first, last, *args)


def _kv_queries(q8, n_heads, n_kv):
    qbd, _ = _block_diag_rows(q8, n_heads)
    wq, wk = n_heads * HEAD_DIM, n_kv * HEAD_DIM
    gshift = int(math.log2(n_heads // n_kv)) + 6
    r = _iota((wq, wk), 0)
    c = _iota((wq, wk), 1)
    fold = jnp.where((lax.shift_right_logical(r, gshift) == lax.shift_right_logical(c, 6))
                     & (jnp.bitwise_and(r, 63) == jnp.bitwise_and(c, 63)), 1.0, 0.0).astype(BF16)
    return _dot(qbd.astype(BF16), fold).astype(BF16)


def _kv_outputs(o, n_heads, n_kv):
    rows = n_heads * ROW_GROUP
    wq, wk = n_heads * HEAD_DIM, n_kv * HEAD_DIM
    gshift = int(math.log2(n_heads // n_kv))
    own_kv = lax.shift_right_logical(_iota((rows, wk), 0), 3 + gshift) == lax.shift_right_logical(_iota((rows, wk), 1), 6)
    r = _iota((wk, wq), 0)
    c = _iota((wk, wq), 1)
    unfold = jnp.where((lax.shift_right_logical(c, gshift + 6) == lax.shift_right_logical(r, 6))
                       & (jnp.bitwise_and(r, 63) == jnp.bitwise_and(c, 63)), 1.0, 0.0).astype(BF16)
    wide = _dot_hl(jnp.where(own_kv, o, 0.0), unfold)
    own = lax.shift_right_logical(_iota((rows, wq), 0), 3) == lax.shift_right_logical(_iota((rows, wq), 1), 6)
    return jnp.sum(jnp.where(own, wide, 0.0).reshape(n_heads, ROW_GROUP, wq), axis=0)


def _softmax_step(sc, vb, m_ref, l_ref, acc_ref, v_dims=NN):
    m_old = m_ref[...]
    m_new = jnp.maximum(m_old, jnp.broadcast_to(jnp.max(sc, axis=1, keepdims=True), m_old.shape))
    p = jnp.exp(sc - jnp.concatenate([m_new] * (sc.shape[1] // LANES), axis=1)).astype(BF16)
    alpha = jnp.exp(m_old - m_new)
    l_ref[...] = alpha * l_ref[...] + _dot(p, jnp.ones((sc.shape[1], LANES), BF16))
    acc = acc_ref[...]
    acc_ref[...] = jnp.concatenate([alpha] * (acc.shape[1] // LANES), axis=1) * acc + _dot(p, vb, v_dims)
    m_ref[...] = m_new


def _new_rows(ref, cols):
    blk = ref[:, cols]
    return jnp.concatenate([blk, jnp.zeros((LANES - ROW_GROUP, blk.shape[1]), F32)], axis=0).astype(BF16)


def _slc_decode_kernel(pt_ref, q_ref, kvn_ref, sel_ref, bias_ref, *rest, n_heads, n_kv, n_pages, n_new, ppb):
    page_refs = rest[:ppb]
    o_ref, m_ref, l_ref, acc_ref, qbd_ref = rest[ppb:]
    p = pl.program_id(1)
    rows = n_heads * ROW_GROUP
    wk = n_kv * HEAD_DIM
    group = n_heads // n_kv
    nsp = sel_ref.shape[2]
    sel_rows = jnp.concatenate([sel_ref[k] for k in range(n_kv) for _ in range(group)], axis=0).astype(BF16)

    def chosen(first_blk, n_keys):
        blk = first_blk + lax.shift_right_logical(_iota((nsp, n_keys), 1), 6)
        expand = jnp.where(_iota((nsp, n_keys), 0) == blk, 1.0, 0.0).astype(BF16)
        return _dot(sel_rows, expand) > 0.5

    @pl.when(p == 0)
    def _():
        m_ref[...] = jnp.full_like(m_ref, M_FLOOR)
        l_ref[...] = jnp.zeros_like(l_ref)
        acc_ref[...] = jnp.zeros_like(acc_ref)
        qbd = _kv_queries(q_ref[...].astype(F32), n_heads, n_kv)
        qbd_ref[...] = qbd
        r_in = jnp.bitwise_and(_iota((rows, LANES), 0), ROW_GROUP - 1)
        col = _iota((rows, LANES), 1)
        kb = _new_rows(kvn_ref, slice(0, wk))
        vb = _new_rows(kvn_ref, slice(wk, 2 * wk))
        valid = (chosen(n_pages * (LANES // SEL_BLOCK), LANES)
                 & (col >= ROW_GROUP - n_new) & (col < ROW_GROUP) & (col <= r_in))
        sc = _dot(qbd, kb, NT) + jnp.where(valid, bias_ref[:, pl.ds(0, LANES)], NEG)
        _softmax_step(sc, vb, m_ref, l_ref, acc_ref)

    @pl.when(p > 0)
    def _():
        kt = jnp.concatenate([ref[pl.ds(0, wk), :] for ref in page_refs], axis=1).astype(BF16)
        vt = jnp.concatenate([ref[pl.ds(wk, wk), :] for ref in page_refs], axis=1).astype(BF16)
        base = n_pages - p * ppb
        valid = chosen(base * (LANES // SEL_BLOCK), ppb * LANES)
        sc = _dot(qbd_ref[...], kt) + jnp.where(valid, bias_ref[...], NEG)
        _softmax_step(sc, vt, m_ref, l_ref, acc_ref, NT)

    @pl.when(p == n_pages // ppb)
    def _():
        den = l_ref[:, pl.ds(0, 1)]
        o_ref[...] = _kv_outputs(acc_ref[...] / jnp.where(den > 0, den, 1.0), n_heads, n_kv)


def slc_decode(nq, kv_new, sel, bias, cache, page_table, layer, n_new):
    S, n_pages = page_table.shape
    n_kv = cache.shape[2] // (2 * HEAD_DIM)
    n_heads = nq.shape[1] // HEAD_DIM
    rows = n_heads * ROW_GROUP
    nsp = sel.shape[3]
    pt = page_table.reshape(-1).astype(jnp.int32)
    ppb = (bias.shape[1] - n_pages * LANES) // LANES
    assert n_pages % ppb == 0
    n_steps = 1 + n_pages // ppb

    def page_map(i):
        def index(s, p, pt_ref):
            logical = n_pages - jnp.maximum(p, 1) * ppb + i
            return (pt_ref[s * n_pages + logical], layer, 0, 0)
        return index

    kern = functools.partial(_slc_decode_kernel, n_heads=n_heads, n_kv=n_kv, n_pages=n_pages, n_new=n_new,
                             ppb=ppb)
    return pl.pallas_call(
        kern,
        grid_spec=pltpu.PrefetchScalarGridSpec(
            num_scalar_prefetch=1,
            grid=(S, n_steps),
            in_specs=[pl.BlockSpec((ROW_GROUP, nq.shape[1]), lambda s, p, pt_ref: (s, 0)),
                      pl.BlockSpec((ROW_GROUP, kv_new.shape[1]), lambda s, p, pt_ref: (s, 0)),
                      pl.BlockSpec((None, n_kv, ROW_GROUP, nsp), lambda s, p, pt_ref: (s, 0, 0, 0)),
                      pl.BlockSpec((rows, ppb * LANES),
                                   lambda s, p, pt_ref: (0, jnp.where(p == 0, n_pages // ppb, n_pages // ppb - p)))]
            + [pl.BlockSpec((None, None, 2 * n_kv * HEAD_DIM, LANES), page_map(i)) for i in range(ppb)],
            out_specs=pl.BlockSpec((ROW_GROUP, nq.shape[1]), lambda s, p, pt_ref: (s, 0)),
            scratch_shapes=[pltpu.VMEM((rows, LANES), F32), pltpu.VMEM((rows, LANES), F32),
                            pltpu.VMEM((rows, n_kv * HEAD_DIM), F32),
                            pltpu.VMEM((rows, n_kv * HEAD_DIM), BF16)]),
        out_shape=jax.ShapeDtypeStruct((S * ROW_GROUP, nq.shape[1]), F32),
        compiler_params=_cparams("parallel", "arbitrary"),
        name="slc_decode",
    )(pt, nq, kv_new, sel, bias, *([cache] * ppb))


def _win_decode_kernel(q_ref, kvn_ref, bias_ref, kv_ref, o_ref, m_ref, l_ref, acc_ref, *,
                       n_heads, n_kv, n_new, layer_rows):
    rows = n_heads * ROW_GROUP
    wk = n_kv * HEAD_DIM
    m_ref[...] = jnp.full_like(m_ref, M_FLOOR)
    l_ref[...] = jnp.zeros_like(l_ref)
    acc_ref[...] = jnp.zeros_like(acc_ref)
    qbd = _kv_queries(q_ref[...].astype(F32), n_heads, n_kv)
    i_q = jnp.bitwise_and(_iota((rows, layer_rows), 0), ROW_GROUP - 1) - (ROW_GROUP - n_new)
    j = _iota((rows, layer_rows), 1)
    dist = layer_rows + i_q - j
    valid = (dist >= 0) & (dist < WINDOW)
    kb = kv_ref[:, pl.ds(0, wk)].astype(BF16)
    vb = kv_ref[:, pl.ds(wk, wk)].astype(BF16)
    sc = _dot(qbd, kb, NT) + jnp.where(valid, bias_ref[:, pl.ds(0, layer_rows)], NEG)
    _softmax_step(sc, vb, m_ref, l_ref, acc_ref)
    r_in = jnp.bitwise_and(_iota((rows, LANES), 0), ROW_GROUP - 1)
    col = _iota((rows, LANES), 1)
    valid = (col >= ROW_GROUP - n_new) & (col < ROW_GROUP) & (col <= r_in)
    kb = _new_rows(kvn_ref, slice(0, wk))
    vb = _new_rows(kvn_ref, slice(wk, 2 * wk))
    sc = _dot(qbd, kb, NT) + jnp.where(valid, bias_ref[:, pl.ds(layer_rows, LANES)], NEG)
    _softmax_step(sc, vb, m_ref, l_ref, acc_ref)
    den = l_ref[:, pl.ds(0, 1)]
    o_ref[...] = _kv_outputs(acc_ref[...] / jnp.where(den > 0, den, 1.0), n_heads, n_kv)


def win_decode(nq, kv_new, bias, cache_win, layer, n_new):
    S, _, wc, width = cache_win.shape
    n_kv = width // (2 * HEAD_DIM)
    n_heads = nq.shape[1] // HEAD_DIM
    rows = n_heads * ROW_GROUP
    kern = functools.partial(_win_decode_kernel, n_heads=n_heads, n_kv=n_kv, n_new=n_new, layer_rows=wc)
    return pl.pallas_call(
        kern,
        grid=(S,),
        in_specs=[pl.BlockSpec((ROW_GROUP, nq.shape[1]), lambda s: (s, 0)),
                  pl.BlockSpec((ROW_GROUP, width), lambda s: (s, 0)),
                  pl.BlockSpec((rows, wc + LANES), lambda s: (0, 0)),
                  pl.BlockSpec((None, None, wc, width), lambda s: (s, layer, 0, 0))],
        out_specs=pl.BlockSpec((ROW_GROUP, nq.shape[1]), lambda s: (s, 0)),
        out_shape=jax.ShapeDtypeStruct((S * ROW_GROUP, nq.shape[1]), F32),
        scratch_shapes=[pltpu.VMEM((rows, LANES), F32), pltpu.VMEM((rows, LANES), F32),
                        pltpu.VMEM((rows, n_kv * HEAD_DIM), F32)],
        compiler_params=_cparams("parallel"),
        name="win_decode",
    )(nq, kv_new, bias, cache_win)


def _nsa_combine_kernel(oc_ref, os_ref, ow_ref, gate_ref, o_ref, *, lane0, n_heads):
    hi, lw = _split(jax.nn.sigmoid(gate_ref[...]))
    width = n_heads * HEAD_DIM
    acc = jnp.zeros(oc_ref.shape, F32)
    for br, ref in enumerate((oc_ref, os_ref, ow_ref)):
        src = lane0 + br * n_heads + lax.shift_right_logical(_iota((LANES, width), 1), 6)
        expand = jnp.where(_iota((LANES, width), 0) == src, 1.0, 0.0).astype(BF16)
        acc = acc + (_dot(hi, expand) + _dot(lw, expand)) * ref[...]
    o_ref[...] = acc.astype(o_ref.dtype)


def nsa_combine(o_cmp, o_slc, o_win, blk_c, *, gate_block, lane0):
    M, width = o_cmp.shape
    tm = _pick(M, (512, 256, 128, 64, 32, 16, 8))
    kern = functools.partial(_nsa_combine_kernel, lane0=lane0, n_heads=width // HEAD_DIM)
    ospec = pl.BlockSpec((tm, width), lambda i: (i, 0))
    return pl.pallas_call(
        kern,
        grid=(M // tm,),
        in_specs=[ospec, ospec, ospec, pl.BlockSpec((tm, LANES), lambda i: (i, gate_block))],
        out_specs=ospec,
        out_shape=jax.ShapeDtypeStruct((M, width), BF16),
        compiler_params=_cparams("parallel"),
        name="nsa_combine",
    )(o_cmp, o_slc, o_win, blk_c)


RW_COLS = 3584


def _rwkv_prep_kernel(c_ref, head_ref, mu_ref, vec_ref, w2_ref, a2_ref, g2_ref,
                      r_ref, lw_ref, k_ref, v_ref, kk_ref, b_ref, bonus_ref, g_ref,
                      xs_ref, carry_ref, *, tm, tps, whole, n_new):
    i = pl.program_id(0)
    cols = c_ref[...]
    if whole:
        cols = cols + head_ref[...]
        head = jnp.zeros((SUBLANES, RW_COLS), F32)
    else:
        head = jnp.where(i % tps == 0, head_ref[0], carry_ref[...])
    xs_ref[pl.ds(0, SUBLANES), :] = head
    xs_ref[pl.ds(SUBLANES, tm), :] = cols
    prev = xs_ref[pl.ds(SUBLANES - 1, tm), :]
    if not whole:
        carry_ref[...] = xs_ref[pl.ds(tm, SUBLANES), :]
    xs_ref[pl.ds(SUBLANES, tm), :] = cols + (prev - cols) * mu_ref[...]
    small = xs_ref[pl.ds(SUBLANES, tm), pl.ds(3328, 256)]
    th = jnp.tanh(small).astype(BF16)
    sm = small.astype(BF16)
    sg = jax.nn.sigmoid(xs_ref[pl.ds(SUBLANES, tm), pl.ds(3072, 256)]).astype(BF16)
    seg = _seg_mat(LANES, HEAD_DIM, 1.0)
    if whole:
        real = jnp.bitwise_and(_iota((tm, LANES), 0), ROW_GROUP - 1) >= ROW_GROUP - n_new
    y_all = _dot(th, w2_ref[...])
    a_all = _dot(sm, a2_ref[...])
    g_ref[...] = _dot(sg, g2_ref[...])
    for j in range(8):
        cs = pl.ds(j * LANES, LANES)
        r = xs_ref[pl.ds(SUBLANES, tm), pl.ds(j * LANES, LANES)]
        k = xs_ref[pl.ds(SUBLANES, tm), pl.ds(1024 + j * LANES, LANES)]
        v = xs_ref[pl.ds(SUBLANES, tm), pl.ds(2048 + j * LANES, LANES)]
        y = vec_ref[0:1, cs] + y_all[:, j * LANES:(j + 1) * LANES]
        w_log = -(jnp.maximum(-y, 0.0) + jnp.log(1.0 + jnp.exp(-jnp.abs(y)))) - 0.5
        lw = -jnp.exp(w_log)
        a = jax.nn.sigmoid(vec_ref[1:2, cs] + a_all[:, j * LANES:(j + 1) * LANES])
        kk = k * vec_ref[2:3, cs]
        kk = kk * lax.rsqrt(jnp.maximum(_dot_hl(kk * kk, seg), 1e-24))
        k2 = k * (1.0 + (a - 1.0) * vec_ref[3:4, cs])
        b = kk * a
        bonus = _dot_hl(r * k2 * vec_ref[4:5, cs], seg) * v
        if whole:
            r, k2, v, kk, b, lw = [jnp.where(real, t, 0.0) for t in (r, k2, v, kk, b, lw)]
        r_ref[:, cs] = r
        lw_ref[:, cs] = lw
        k_ref[:, cs] = k2
        v_ref[:, cs] = v
        kk_ref[:, cs] = kk
        b_ref[:, cs] = b
        bonus_ref[:, cs] = bonus


def rwkv_prep(blk_c, head, mu, vec, w2p, a2p, g2, *, seq_rows, whole, n_new=0):
    M = blk_c.shape[0]
    if whole:
        tm, tps = M, 1
        head_spec = pl.BlockSpec((tm, RW_COLS), lambda i: (0, 0))
    else:
        tm = _pick(seq_rows, (256, 128, 64))
        tps = seq_rows // tm
        head_spec = pl.BlockSpec((1, SUBLANES, RW_COLS), lambda i: (i // tps, 0, 0))
    kern = functools.partial(_rwkv_prep_kernel, tm=tm, tps=tps, whole=whole, n_new=n_new)
    full = lambda shape: pl.BlockSpec(shape, lambda i: (0,) * len(shape))
    return pl.pallas_call(
        kern,
        grid=(M // tm,),
        in_specs=[pl.BlockSpec((tm, RW_COLS), lambda i: (i, 0)), head_spec, full((1, RW_COLS)),
                  full((SUBLANES, 1024)), full((256, 1024)), full((256, 1024)), full((256, 1024))],
        out_specs=[pl.BlockSpec((tm, 1024), lambda i: (i, 0))] * 8,
        out_shape=[jax.ShapeDtypeStruct((M, 1024), F32)] * 8,
        scratch_shapes=[pltpu.VMEM((tm + SUBLANES, RW_COLS), F32), pltpu.VMEM((SUBLANES, RW_COLS), F32)],
        compiler_params=_cparams("arbitrary"),
        name="rwkv_prep",
    )(blk_c, head, mu, vec, w2p, a2p, g2)


def _rwkv_par_kernel(r_ref, lw_ref, k_ref, v_ref, kk_ref, b_ref, r2_ref, y2_ref, m_ref, g_ref, *, C, npair):
    C2 = 2 * C
    cum = jnp.where(_iota((C, C), 0) >= _iota((C, C), 1), 1.0, 0.0).astype(BF16)
    keep = (_iota((C2, LANES), 0) < C) == (_iota((C2, LANES), 1) < HEAD_DIM)
    rb = _iota((C2, C2), 0)
    cb = _iota((C2, C2), 1)
    same = (rb < C) == (cb < C)
    rr = jnp.bitwise_and(rb, C - 1)
    cc = jnp.bitwise_and(cb, C - 1)
    strict = same & (rr > cc)
    incl = same & (rr >= cc)
    eye2 = jnp.where(rb == cb, 1.0, 0.0)
    eye_l = _iota((LANES, LANES), 0) == _iota((LANES, LANES), 1)

    def stack(x):
        return jnp.where(keep, jnp.concatenate([x, x], axis=0), 0.0)

    group = 8
    for p0 in range(0, npair, group):
        ps = range(p0, min(p0 + group, npair))
        cols = [pl.ds(p * LANES, LANES) for p in ps]
        n = len(cols)
        lw = [lw_ref[:, cs] for cs in cols]
        log_p = [_dot_hl_rhs(cum, x) for x in lw]
        log_end = [x[C - 1:C, :] for x in log_p]
        e_pos = [jnp.exp(x) for x in log_p]
        e_neg = [jnp.exp(-x) for x in log_p]
        e_end = [jnp.exp(le - x) for le, x in zip(log_end, log_p)]
        k = [k_ref[:, cs] for cs in cols]
        b = [b_ref[:, cs] for cs in cols]
        rt = [stack(r_ref[:, cs] * e) for cs, e in zip(cols, e_pos)]
        kt = [stack(x * e) for x, e in zip(k, e_neg)]
        bt = [stack(x * e) for x, e in zip(b, e_neg)]
        at = [stack(-kk_ref[:, cs] * jnp.exp(x - y)) for cs, x, y in zip(cols, log_p, lw)]
        kendb = [stack(x * e).astype(BF16) for x, e in zip(k, e_end)]
        bendb = [stack(x * e).astype(BF16) for x, e in zip(b, e_end)]
        v2b = [stack(v_ref[:, cs]).astype(BF16) for cs in cols]
        atb = [x.astype(BF16) for x in at]
        cross = [_dot(jnp.concatenate([atb[i], rt[i].astype(BF16)], axis=0),
                      jnp.concatenate([kt[i], bt[i]], axis=0).astype(BF16), NT) for i in range(n)]
        a_ak = [jnp.where(strict, x[:C2, :C2], 0.0).astype(BF16) for x in cross]
        a_ab = [jnp.where(strict, x[:C2, C2:], 0.0) for x in cross]
        a_rk = [jnp.where(incl, x[C2:, :C2], 0.0).astype(BF16) for x in cross]
        a_rb = [jnp.where(incl, x[C2:, C2:], 0.0).astype(BF16) for x in cross]
        akv = [_dot(x, y).astype(BF16) for x, y in zip(a_ak, v2b)]
        inv = [eye2 + x for x in a_ab]
        powr = [x.astype(BF16) for x in a_ab]
        for _ in range(int(math.log2(C)) - 1):
            powr = [_dot(x, x).astype(BF16) for x in powr]
            inv = [x + _dot(x.astype(BF16), y) for x, y in zip(inv, powr)]
        aub = [_dot(x.astype(BF16), jnp.concatenate([y, z], axis=1)).astype(BF16)
               for x, y, z in zip(inv, atb, akv)]
        ry = [_dot(x, y) for x, y in zip(a_rb, aub)]
        rkv = [_dot(x, y) for x, y in zip(a_rk, v2b)]
        mg = [_dot(x, y, TN) for x, y in zip(aub, bendb)]
        vk = [_dot(x, y, TN) for x, y in zip(v2b, kendb)]
        for i, cs in enumerate(cols):
            r2_ref[:, cs] = rt[i] + ry[i][:, :LANES]
            y2_ref[:, cs] = rkv[i] + ry[i][:, LANES:]
            decay_end = jnp.where(eye_l, jnp.broadcast_to(jnp.exp(log_end[i]), (LANES, LANES)), 0.0)
            m_ref[:, cs] = decay_end + mg[i][:LANES]
            g_ref[:, cs] = vk[i] + mg[i][LANES:]


def rwkv_par(r, lw, k, v, kk, b, C):
    M, W = r.shape
    npair = W // LANES
    nch = M // C
    kern = functools.partial(_rwkv_par_kernel, C=C, npair=npair)
    ispec = pl.BlockSpec((C, W), lambda i: (i, 0))
    return pl.pallas_call(
        kern,
        grid=(nch,),
        in_specs=[ispec] * 6,
        out_specs=[pl.BlockSpec((None, 2 * C, W), lambda i: (i, 0, 0))] * 2
        + [pl.BlockSpec((None, LANES, W), lambda i: (i, 0, 0))] * 2,
        out_shape=[jax.ShapeDtypeStruct((nch, 2 * C, W), F32)] * 2
        + [jax.ShapeDtypeStruct((nch, LANES, W), F32)] * 2,
        compiler_params=_cparams("parallel"),
        name="rwkv_par",
    )(r, lw, k, v, kk, b)


def _rwkv_seq_kernel(r2_ref, y2_ref, m_ref, g_ref, s0_ref, bonus_ref, gate_ref, ln_ref,
                     o_ref, sf_ref, s_ref, *, C, npair):
    c = pl.program_id(1)

    @pl.when(c == 0)
    def _():
        s_ref[...] = s0_ref[...]

    seg = _seg_mat(LANES, HEAD_DIM, 1.0 / HEAD_DIM)
    cols = [pl.ds(p * LANES, LANES) for p in range(npair)]
    st = [s_ref[:, cs] for cs in cols]
    y2 = [_dot3(r2_ref[:, cs], x, NT) + y2_ref[:, cs] for cs, x in zip(cols, st)]
    s_new = [_dot3(x, m_ref[:, cs]) + g_ref[:, cs] for cs, x in zip(cols, st)]
    for cs, x in zip(cols, s_new):
        s_ref[:, cs] = x
    y = [x[:C] + x[C:] for x in y2]
    mu = [_dot_hl(x, seg) for x in y]
    dev = [x - m for x, m in zip(y, mu)]
    var = [_dot_hl(x * x, seg) for x in dev]
    for cs, d, v in zip(cols, dev, var):
        yn = d * lax.rsqrt(v + GN_EPS) * ln_ref[0:1, cs] + ln_ref[1:2, cs]
        o_ref[:, cs] = ((yn + bonus_ref[:, cs]) * gate_ref[:, cs]).astype(o_ref.dtype)

    @pl.when(c == pl.num_programs(1) - 1)
    def _():
        sf_ref[...] = s_ref[...]


def rwkv_seq(r2, y2, mt, gt, s0, bonus, gate, ln, *, n_seq):
    nch_total, C2, W = r2.shape
    C = C2 // 2
    nch = nch_total // n_seq
    npair = W // LANES
    kern = functools.partial(_rwkv_seq_kernel, C=C, npair=npair)
    cspec = lambda rows: pl.BlockSpec((None, rows, W), lambda s, c: (s * nch + c, 0, 0))
    return pl.pallas_call(
        kern,
        grid=(n_seq, nch),
        in_specs=[cspec(C2), cspec(C2), cspec(LANES), cspec(LANES),
                  pl.BlockSpec((None, LANES, W), lambda s, c: (s, 0, 0)),
                  pl.BlockSpec((C, W), lambda s, c: (s * nch + c, 0)),
                  pl.BlockSpec((C, W), lambda s, c: (s * nch + c, 0)),
                  pl.BlockSpec((SUBLANES, W), lambda s, c: (0, 0))],
        out_specs=[pl.BlockSpec((C, W), lambda s, c: (s * nch + c, 0)),
                   pl.BlockSpec((None, LANES, W), lambda s, c: (s, 0, 0))],
        out_shape=[jax.ShapeDtypeStruct((nch_total * C, W), BF16),
                   jax.ShapeDtypeStruct((n_seq, LANES, W), F32)],
        scratch_shapes=[pltpu.VMEM((LANES, W), F32)],
        compiler_params=_cparams("parallel", "arbitrary"),
        name="rwkv_seq",
    )(r2, y2, mt, gt, s0, bonus, gate, ln)


def _bucket_np(d):
    d = np.maximum(d, 0)
    ratio = np.log(np.maximum(d, 1).astype(np.float32) / np.float32(N_BUCKETS // 2)) / np.float32(
        math.log(MAX_DISTANCE / (N_BUCKETS // 2)))
    large = np.minimum(N_BUCKETS // 2 + (ratio * np.float32(N_BUCKETS - N_BUCKETS // 2)).astype(np.int32),
                       N_BUCKETS - 1)
    return np.where(d < N_BUCKETS // 2, d, large).astype(np.int32)


def _overlap_np(n_cmp, n_sel):
    c0 = np.arange(n_cmp)[:, None] * CMP_STRIDE
    j0 = np.arange(n_sel)[None, :] * SEL_BLOCK
    ov = np.clip(np.minimum(c0 + CMP_BLOCK, j0 + SEL_BLOCK) - np.maximum(c0, j0), 0, None)
    return (ov / CMP_BLOCK).astype(np.float32)


def _bias_table_kernel(tab_ref, idx_ref, o_ref):
    h = pl.program_id(0)
    idx = idx_ref[...]
    acc = jnp.zeros(idx.shape, F32)
    for b in range(N_BUCKETS):
        acc = jnp.where(idx == b, tab_ref[h, b], acc)
    o_ref[...] = acc


def bias_table(tab_h, idx):
    H = tab_h.shape[0]
    R, C = idx.shape
    tr = _pick(R, (256, 128, 64, 32, 16, 8))
    return pl.pallas_call(
        _bias_table_kernel,
        grid=(H, R // tr),
        in_specs=[pl.BlockSpec(memory_space=pltpu.SMEM),
                  pl.BlockSpec((tr, C), lambda h, i: (i, 0))],
        out_specs=pl.BlockSpec((None, tr, C), lambda h, i: (h, i, 0)),
        out_shape=jax.ShapeDtypeStruct((H, R, C), F32),
        compiler_params=_cparams("parallel", "parallel"),
        name="bias_table",
    )(tab_h, idx)


def _make_tables(rel_bias, T, past, n_new, n_heads):
    tab_h = rel_bias.astype(F32).T

    def take(idx):
        idx = np.asarray(idx, np.int32)
        out = bias_table(tab_h, jnp.asarray(idx.reshape(-1, idx.shape[-1])))
        return out.reshape((tab_h.shape[0],) + idx.shape)

    t = {}
    tq = _pick(T, (256, 128))
    nq_t = T // tq
    far = 0
    while _bucket_np(np.array([far]))[0] < N_BUCKETS - 1:
        far += 1
    nd = min(-(-(far + tq - 1) // tq) + 1, nq_t)
    i = np.arange(tq)[:, None]
    j = np.arange(tq)[None, :]
    t["tiles"] = take(np.stack([_bucket_np(dl * tq + i - j) for dl in range(nd)]))
    ncp = T // CMP_STRIDE
    tt = np.arange(T)[:, None]
    cc = np.arange(ncp)[None, :]
    t["cmp_p"] = take(_bucket_np(tt - (cc * CMP_STRIDE + CMP_BLOCK - 1)))
    n_sel = -(-T // SEL_BLOCK)
    nsp = -(-n_sel // SUBLANES) * SUBLANES
    ov = np.zeros((nsp, ncp), np.float32)
    ov[:n_sel, :ncp - 1] = _overlap_np(ncp - 1, n_sel).T
    t["ov_p"] = jnp.asarray(ov, BF16)
    t["n_sel_p"] = n_sel
    t["tq"] = tq
    rows = n_heads * ROW_GROUP
    tpos = past - (ROW_GROUP - n_new) + np.arange(ROW_GROUP)
    ppb = _pick(past // LANES, (4, 2, 1))
    kpos = np.concatenate([np.arange(past), past - (ROW_GROUP - n_new) + np.arange(LANES),
                           np.zeros((ppb - 1) * LANES, np.int64)])
    idx = _bucket_np(tpos[:, None] - kpos[None, :])
    t["slc_d"] = take(idx).reshape(rows, past + ppb * LANES)
    wc = min(WINDOW, past)
    kpos = np.concatenate([past - wc + np.arange(wc), past - (ROW_GROUP - n_new) + np.arange(LANES)])
    t["win_d"] = take(_bucket_np(tpos[:, None] - kpos[None, :])).reshape(rows, wc + LANES)
    ncp_d = past // CMP_STRIDE
    cc = np.arange(ncp_d)[None, :]
    t["cmp_d"] = take(_bucket_np(tpos[:, None] - (cc * CMP_STRIDE + CMP_BLOCK - 1)))
    n_sel_d = -(-(past + n_new) // SEL_BLOCK)
    nsp_d = -(-n_sel_d // LANES) * LANES
    n_cmp_d = (past + n_new) // CMP_STRIDE - 1
    ov = np.zeros((ncp_d, nsp_d), np.float32)
    ov[:n_cmp_d, :n_sel_d] = _overlap_np(n_cmp_d, n_sel_d)
    t["ov_d"] = jnp.asarray(ov, BF16)
    t["n_sel_d"] = n_sel_d
    return t


def _prep_layer(l, P):
    W = {}
    w_in = P["w_in"][l]
    sbw = 1024
    W["wA"] = jnp.concatenate([w_in[:, :sbw] * HEAD_DIM ** -0.5, w_in[:, sbw:3 * sbw]], axis=1).astype(BF16)
    W["wB"] = w_in[:, 3072:5632].astype(BF16)
    D = w_in.shape[0]
    W["wC"] = jnp.concatenate([w_in[:, 5680:8752], w_in[:, 8944:9200], w_in[:, 8752:8944],
                               w_in[:, 5632:5680], jnp.zeros((D, 16), F32)], axis=1).astype(BF16)
    W["wD"] = w_in[:, 9200:].astype(BF16)
    W["norm_mix"] = P["norm_mix_g"][l]
    W["norm_ffn"] = P["norm_ffn_g"][l]
    gains = P["nsa_qk_gain"][l]
    W["gains"] = jnp.pad(jnp.tile(gains, (1, 2)), ((0, 4), (0, 0)))
    W["gain1"] = jnp.tile(gains[1:2], (1, 2))
    w1 = P["cmp_w1"][l].reshape(2, 2, CMP_STRIDE, HEAD_DIM, HEAD_DIM)
    first, second = w1[:, 0], w1[:, 1]
    z = jnp.zeros_like(first)
    W["w1c"] = jnp.concatenate([jnp.concatenate([first, z, second, z], axis=-1),
                                jnp.concatenate([z, first, z, second], axis=-1)], axis=-2).astype(BF16)
    pos = jnp.einsum("cld,clde->ce", P["cmp_pe"][l], P["cmp_w1"][l], precision=lax.Precision.HIGHEST)
    W["posb"] = jnp.pad(jnp.tile(pos + P["cmp_b1"][l], (1, 2)), ((0, 6), (0, 0)))
    w2 = P["cmp_w2"][l]
    w2d = jnp.concatenate([w2, w2], axis=-1)
    z2 = jnp.zeros_like(w2d)
    W["w2dup"] = jnp.stack([jnp.concatenate([w2d, z2], axis=1), jnp.concatenate([z2, w2d], axis=1)],
                           axis=1).astype(BF16)
    W["b2"] = jnp.pad(jnp.tile(P["cmp_b2"][l], (1, 2)), ((0, 6), (0, 0)))
    mu = P["rwkv_mu"][l]
    W["mu"] = jnp.concatenate([mu[:3072], mu[3264:3520], mu[3072:3264], jnp.zeros((64,), F32)])[None]
    W["vec"] = jnp.pad(jnp.stack([P["rwkv_w0"][l], P["rwkv_a0"][l], P["rwkv_kk"][l], P["rwkv_ka"][l],
                                  P["rwkv_rk"][l].reshape(-1)]), ((0, 3), (0, 0)))
    W["w2p"] = jnp.pad(P["rwkv_w2"][l], ((0, 160), (0, 0))).astype(BF16)
    W["a2p"] = jnp.pad(P["rwkv_a2"][l], ((96, 64), (0, 0))).astype(BF16)
    W["g2"] = P["rwkv_g2"][l].astype(BF16)
    W["ln"] = jnp.pad(jnp.stack([P["rwkv_lnx_w"][l], P["rwkv_lnx_b"][l]]), ((0, 6), (0, 0)))
    W["wb"] = P["w_branch"][l].astype(BF16)
    W["wout"] = P["w_out"][l].astype(BF16)
    F = P["ffn_conv_w"].shape[-1]
    Fp = -(-F // 512) * 512
    padc = lambda a: jnp.pad(a, ((0, 0), (0, Fp - F)))
    w_up = P["ffn_w_up"][l]
    W["wu"] = padc(w_up[:, :F]).astype(BF16)
    W["wg"] = padc(w_up[:, F:]).astype(BF16)
    W["cw"] = jnp.pad(padc(P["ffn_conv_w"][l]), ((0, 5), (0, 0)))
    W["cb"] = padc(P["ffn_conv_b"][l][None])
    W["wd"] = jnp.pad(P["ffn_w_down"][l], ((0, Fp - F), (0, 0))).astype(BF16)
    W["F"], W["Fp"] = F, Fp
    return W


def _rw_to_mine(a):
    pad = jnp.zeros(a.shape[:-1] + (64,), a.dtype)
    return jnp.concatenate([a[..., :3072], a[..., 3264:3520], a[..., 3072:3264], pad], axis=-1)


def _rw_from_mine(a):
    return jnp.concatenate([a[..., :3072], a[..., 3328:3520], a[..., 3072:3328]], axis=-1)


def _state_to_pairs(s):
    S, H = s.shape[:2]
    sp = s.reshape(S, H // 2, 2, HEAD_DIM, HEAD_DIM)
    z = jnp.zeros_like(sp[:, :, 0])
    top = jnp.concatenate([sp[:, :, 0], z], axis=-1)
    bot = jnp.concatenate([z, sp[:, :, 1]], axis=-1)
    bd = jnp.concatenate([top, bot], axis=-2)
    return bd.transpose(0, 2, 1, 3).reshape(S, LANES, H // 2 * LANES)


def _state_from_pairs(bd, H):
    S = bd.shape[0]
    b4 = bd.reshape(S, LANES, H // 2, LANES).transpose(0, 2, 1, 3)
    return jnp.stack([b4[:, :, :HEAD_DIM, :HEAD_DIM], b4[:, :, HEAD_DIM:, HEAD_DIM:]], axis=2).reshape(
        S, H, HEAD_DIM, HEAD_DIM)


def _layer(x, W, tb, *, prompt, n_seq, seq_rows, n_new, past, dec):
    blk_a = rms_matmul(x, W["norm_mix"], W["wA"])
    blk_b = rms_matmul(x, W["norm_mix"], W["wB"])
    blk_c = rms_matmul(x, W["norm_mix"], W["wC"])
    blk_d = rms_matmul(x, W["norm_mix"], W["wD"])
    nq, cmp_new, slc_new, win_new = nsa_prep(blk_b, W["gains"])
    M = x.shape[0]
    if prompt:
        o_a = sb_prompt(blk_a, n_seq, seq_rows, 16)
        fs = compress(cmp_new.reshape(M // LANES, 1, LANES, 512), 0, W["w1c"])
        kd, vd = cmp_finish(fs.reshape(n_seq, seq_rows // CMP_STRIDE, 1024), W["posb"], W["w2dup"], W["b2"],
                            W["gain1"])
        tq = tb["tq"]
        o_cmp, sel = cmp_attn(nq, kd, vd, tb["cmp_p"], tb["ov_p"], S=n_seq, tq=tq, n_q=seq_rows // tq,
                              n_sel=tb["n_sel_p"], pos0=0, transposed=True)
        o_slc = flash_prompt(nq, slc_new, tb["tiles"], sel, B=n_seq, T=seq_rows, mode="slc")
        o_win = flash_prompt(nq, win_new, tb["tiles"], None, B=n_seq, T=seq_rows, mode="win")
        head = jnp.zeros((n_seq, SUBLANES, RW_COLS), F32)
        rw = rwkv_prep(blk_c, head, W["mu"], W["vec"], W["w2p"], W["a2p"], W["g2"], seq_rows=seq_rows,
                       whole=False)
        C = RWKV_CHUNK
        s0 = jnp.zeros((n_seq, LANES, 1024), F32)
    else:
        lyr = dec["layer"]
        pt = dec["page_table"]
        o_a = sb_decode(blk_a, dec["cache_sb"], pt, lyr, n_new)
        fs_phys = compress(dec["cache_cmp"], lyr, W["w1c"], transposed=True)
        fs = gather_page_rows(fs_phys, pt)
        kd, vd = cmp_finish(fs, W["posb"], W["w2dup"], W["b2"], W["gain1"])
        o_cmp, sel = cmp_attn(nq, kd, vd, tb["cmp_d"], tb["ov_d"], S=n_seq, tq=ROW_GROUP, n_q=1,
                              n_sel=tb["n_sel_d"], pos0=past - (ROW_GROUP - n_new), transposed=False)
        o_slc = slc_decode(nq, slc_new, sel, tb["slc_d"], dec["cache_slc"], pt, lyr, n_new)
        o_win = win_decode(nq, win_new, tb["win_d"], dec["cache_win"], lyr, n_new)
        shift = _rw_to_mine(dec["shift"][:, lyr])
        head = jnp.zeros((n_seq, ROW_GROUP, RW_COLS), F32).at[:, ROW_GROUP - n_new - 1].set(shift)
        rw = rwkv_prep(blk_c, head.reshape(M, RW_COLS), W["mu"], W["vec"], W["w2p"], W["a2p"], W["g2"],
                       seq_rows=ROW_GROUP, whole=True, n_new=n_new)
        C = ROW_GROUP
        s0 = _state_to_pairs(dec["rwkv"][:, lyr].astype(F32))
    o_b = nsa_combine(o_cmp, o_slc, o_win, blk_c, gate_block=27, lane0=64)
    r, lw, k2, v, kk, b, bonus, g = rw
    r2, y2, mt, gt = rwkv_par(r, lw, k2, v, kk, b, C)
    o_c, s_fin = rwkv_seq(r2, y2, mt, gt, s0, bonus, g, W["ln"], n_seq=n_seq)
    mixed = merge_mix(o_a, o_b, o_c, W["wb"], blk_d)
    x1 = matmul_res(x, mixed, W["wout"])
    F, Fp = W["F"], W["Fp"]
    if prompt:
        x2, tail = conv_ffn(x1, W["norm_ffn"], W["wu"], W["wg"], W["cw"], W["cb"], W["wd"],
                            jnp.zeros((n_seq, SUBLANES, Fp), F32), seq_rows=seq_rows, whole_gate=False)
        conv = tail[:, SUBLANES - 2:, :F]
        last = blk_c.reshape(n_seq, seq_rows, RW_COLS)[:, -1]
    else:
        inj = jnp.zeros((n_seq, ROW_GROUP, Fp), F32).at[:, ROW_GROUP - n_new - 2:ROW_GROUP - n_new, :F].set(
            dec["conv"][:, lyr])
        real = (jnp.arange(M) % ROW_GROUP >= ROW_GROUP - n_new)[:, None]
        x2, gate = conv_ffn(jnp.where(real, x1, 0.0), W["norm_ffn"], W["wu"], W["wg"], W["cw"], W["cb"], W["wd"],
                            inj.reshape(M, Fp), seq_rows=ROW_GROUP, whole_gate=True)
        x2 = jnp.where(real, x2, 0.0)
        conv = gate.reshape(n_seq, ROW_GROUP, Fp)[:, ROW_GROUP - 2:, :F]
        last = blk_c.reshape(n_seq, ROW_GROUP, RW_COLS)[:, -1]
    new = {"sb": blk_a[:, 1024:], "cmp": cmp_new, "slc": slc_new, "win": win_new,
           "rwkv": _state_from_pairs(s_fin, 16), "shift": _rw_from_mine(last), "conv": conv}
    return x2, new


def kernel(x_prompt, x_sample, cache_sb_kv, cache_cmp_kv, cache_slc_kv, cache_win_kv, state_rwkv, state_rwkv_shift, state_conv, page_table, rel_bias, norm_mix_g, norm_ffn_g, w_in, nsa_qk_gain, cmp_pe, cmp_w1, cmp_b1, cmp_w2, cmp_b2, rwkv_mu, rwkv_w0, rwkv_w2, rwkv_a0, rwkv_a2, rwkv_g2, rwkv_kk, rwkv_ka, rwkv_rk, rwkv_lnx_w, rwkv_lnx_b, w_branch, w_out, ffn_w_up, ffn_conv_w, ffn_conv_b, ffn_w_down):
    P = dict(norm_mix_g=norm_mix_g, norm_ffn_g=norm_ffn_g, w_in=w_in, nsa_qk_gain=nsa_qk_gain, cmp_pe=cmp_pe,
             cmp_w1=cmp_w1, cmp_b1=cmp_b1, cmp_w2=cmp_w2, cmp_b2=cmp_b2, rwkv_mu=rwkv_mu, rwkv_w0=rwkv_w0,
             rwkv_w2=rwkv_w2, rwkv_a0=rwkv_a0, rwkv_a2=rwkv_a2, rwkv_g2=rwkv_g2, rwkv_kk=rwkv_kk,
             rwkv_ka=rwkv_ka, rwkv_rk=rwkv_rk, rwkv_lnx_w=rwkv_lnx_w, rwkv_lnx_b=rwkv_lnx_b,
             w_branch=w_branch, w_out=w_out, ffn_w_up=ffn_w_up, ffn_conv_w=ffn_conv_w,
             ffn_conv_b=ffn_conv_b, ffn_w_down=ffn_w_down)
    B, T, D = x_prompt.shape
    S, n_new, _ = x_sample.shape
    depth = w_in.shape[0]
    n_phys, _, page = cache_sb_kv.shape[:3]
    n_pages = page_table.shape[1]
    past = n_pages * page
    n_heads = rel_bias.shape[1]
    assert page == LANES and n_new + 2 <= ROW_GROUP and T % 128 == 0 and past % LANES == 0
    assert cache_win_kv.shape[2] == WINDOW <= past
    tb = _make_tables(rel_bias, T, past, n_new, n_heads)
    dec = {"page_table": page_table,
           "cache_sb": jnp.transpose(cache_sb_kv, (0, 1, 3, 4, 5, 2)).reshape(n_phys, depth, -1, page),
           "cache_cmp": jnp.transpose(cache_cmp_kv, (0, 1, 3, 4, 5, 2)).reshape(n_phys, depth, -1, page),
           "cache_slc": jnp.transpose(cache_slc_kv, (0, 1, 3, 4, 5, 2)).reshape(n_phys, depth, -1, page),
           "cache_win": cache_win_kv.reshape(S, depth, WINDOW, -1),
           "rwkv": state_rwkv, "shift": state_rwkv_shift, "conv": state_conv}
    xp = x_prompt.reshape(B * T, D)
    xs = jnp.zeros((S, ROW_GROUP, D), F32).at[:, ROW_GROUP - n_new:].set(x_sample).reshape(S * ROW_GROUP, D)
    new_p, new_s = [], []
    for l in range(depth):
        W = _prep_layer(l, P)
        xp, st = _layer(xp, W, tb, prompt=True, n_seq=B, seq_rows=T, n_new=0, past=0, dec=None)
        new_p.append(st)
        dec["layer"] = l
        xs, st = _layer(xs, W, tb, prompt=False, n_seq=S, seq_rows=ROW_GROUP, n_new=n_new, past=past, dec=dec)
        new_s.append(st)

    n_win = min(WINDOW, T)

    def stack_p(name, shape):
        return jnp.stack([st[name].reshape((B, -1) + shape) for st in new_p], axis=1)

    def rows_s(a):
        return a.reshape(S, ROW_GROUP, -1)[:, ROW_GROUP - n_new:]

    def stack_s(name, shape):
        return jnp.stack([rows_s(st[name]).reshape((S, n_new) + shape) for st in new_s], axis=1)

    kvh = cache_cmp_kv.shape[4]
    sbh = cache_sb_kv.shape[4]
    p_win = jnp.stack([st["win"].reshape(B, T, 2, kvh, HEAD_DIM)[:, T - n_win:] for st in new_p], axis=1)
    s_win = jnp.stack([jnp.concatenate([cache_win_kv[:, l], rows_s(st["win"]).reshape(S, n_new, 2, kvh, HEAD_DIM)],
                                       axis=1)[:, n_new:] for l, st in enumerate(new_s)], axis=1)
    outs = (xp.reshape(B, T, D), rows_s(xs),
            stack_p("sb", (2, sbh, HEAD_DIM)), stack_p("cmp", (2, kvh, HEAD_DIM)),
            stack_p("slc", (2, kvh, HEAD_DIM)), p_win,
            jnp.stack([st["rwkv"] for st in new_p], axis=1), jnp.stack([st["shift"] for st in new_p], axis=1),
            jnp.stack([st["conv"] for st in new_p], axis=1),
            stack_s("sb", (2, sbh, HEAD_DIM)), stack_s("cmp", (2, kvh, HEAD_DIM)),
            stack_s("slc", (2, kvh, HEAD_DIM)), s_win,
            jnp.stack([st["rwkv"] for st in new_s], axis=1), jnp.stack([st["shift"] for st in new_s], axis=1),
            jnp.stack([st["conv"] for st in new_s], axis=1))
    return outs
```
